```python
import math
import jax, jax.numpy as jnp
from jax import lax
import numpy as np

D_MODEL = 2048
BATCH = 4
SEQ = 2048
DEPTH = 2
DEC_BATCH = 2
DEC_SEQ = 4096
PAST_LEN = 128

HEAD_DIM = 128
A_Q_HEADS = 4
A_KV_HEADS = 2
A_GROUP = A_Q_HEADS // A_KV_HEADS
A_HALF_WINDOW = 128
B_Q_HEADS = 6
B_KV_HEADS = 2
B_GROUP = B_Q_HEADS // B_KV_HEADS
B_QUERY_BLOCK = 128
ROPE_THETA = 10000.0
GRID_W = 64
C_PAIRS = ((128, 1), (512, 4), (2048, 16))
N_PAIRS = 3
C_SLOTS = 2
C_KV_PER_PAIR = 1
C_GROUP = C_SLOTS // C_KV_PER_PAIR
C_Q_HEADS = N_PAIRS * C_SLOTS
C_KV_HEADS = N_PAIRS * C_KV_PER_PAIR
NUM_BUCKETS = 32
REL_MAX_DISTANCE = 2048
D_FF = 5632
FFN_RESIDUAL = 0.5
N_BRANCHES = 3
NORM_EPS = 1e-6
MASK_VALUE = -1e30
IN_SIZES = (A_Q_HEADS * HEAD_DIM, A_KV_HEADS * HEAD_DIM, A_KV_HEADS * HEAD_DIM,
            B_Q_HEADS * HEAD_DIM, B_KV_HEADS * HEAD_DIM, B_KV_HEADS * HEAD_DIM,
            C_Q_HEADS * HEAD_DIM, C_KV_HEADS * HEAD_DIM, C_KV_HEADS * HEAD_DIM)
IN_COLS = sum(IN_SIZES)
IN_SPLIT_POINTS = tuple(int(v) for v in np.cumsum(IN_SIZES)[:-1])

kernel_name = 'hybrid_gated_parallel_encoder'


def rms_norm(x, gain):
    xf = x.astype(jnp.float32)
    y = xf * lax.rsqrt(jnp.mean(xf * xf, axis=-1, keepdims=True) + NORM_EPS)
    return (y * gain.astype(jnp.float32)).astype(x.dtype)


def swiglu(h, w13, w2):
    gate, up = jnp.split(h @ w13, 2, axis=-1)
    return (jax.nn.silu(gate) * up) @ w2


def t5_bucket(rel):
    half = NUM_BUCKETS // 2
    max_exact = half // 2
    ret = jnp.where(rel > 0, half, 0)
    n = jnp.abs(rel)
    nf = jnp.maximum(n, 1).astype(jnp.float32)
    large = max_exact + (jnp.log(nf / max_exact) / math.log(REL_MAX_DISTANCE / max_exact)
                         * (half - max_exact)).astype(jnp.int32)
    large = jnp.minimum(large, half - 1)
    return ret + jnp.where(n < max_exact, n, large)


def banded_gqa(q, k, v, half_window, dilation, bias_table, sink):
    n, L, kvh, g, hd = q.shape
    blk = half_window
    nb = -(-L // blk)
    lp = nb * blk
    q = jnp.pad(q, ((0, 0), (0, lp - L), (0, 0), (0, 0), (0, 0)))
    pad_kv = ((0, 0), (blk, lp - L + blk), (0, 0), (0, 0))
    kp = jnp.pad(k, pad_kv).reshape(n, nb + 2, blk, kvh, hd)
    vp = jnp.pad(v, pad_kv).reshape(n, nb + 2, blk, kvh, hd)
    kw = jnp.concatenate([kp[:, :-2], kp[:, 1:-1], kp[:, 2:]], axis=2).astype(jnp.float32)
    vw = jnp.concatenate([vp[:, :-2], vp[:, 1:-1], vp[:, 2:]], axis=2).astype(jnp.float32)
    qc = q.reshape(n, nb, blk, kvh, g, hd).astype(jnp.float32)
    logits = jnp.einsum('ncqkgd,ncskd->nckgqs', qc, kw) / math.sqrt(hd)
    rel = jnp.arange(3 * blk)[None, :] - blk - jnp.arange(blk)[:, None]
    bias = jnp.transpose(bias_table[t5_bucket(rel * dilation)], (2, 3, 0, 1)).astype(jnp.float32)
    key_pos = jnp.arange(nb)[:, None] * blk - blk + jnp.arange(3 * blk)[None, :]
    valid = (jnp.abs(rel) <= half_window)[None] & ((key_pos >= 0) & (key_pos < L))[:, None, :]
    logits = jnp.where(valid[None, :, None, None], logits + bias, MASK_VALUE)
    m = jnp.max(logits, axis=-1)
    if sink is not None:
        sink_b = sink.astype(jnp.float32)[None, None, :, :, None]
        m = jnp.maximum(m, sink_b)
    p = jnp.exp(logits - m[..., None])
    denom = jnp.sum(p, axis=-1)
    if sink is not None:
        denom = denom + jnp.exp(sink_b - m)
    out = jnp.einsum('nckgqs,ncskd->ncqkgd', p, vw)
    denom_t = jnp.transpose(denom, (0, 1, 4, 2, 3))
    out = (out / denom_t[..., None]).reshape(n, lp, kvh, g, hd)[:, :L]
    lse = jnp.transpose(m + jnp.log(denom), (0, 1, 4, 2, 3)).reshape(n, lp, kvh, g)[:, :L]
    return out, lse


def head_rms_norm(x, gain):
    xf = x.astype(jnp.float32)
    y = xf * lax.rsqrt(jnp.mean(xf * xf, axis=-1, keepdims=True) + NORM_EPS)
    return y * gain.astype(jnp.float32)


def axial_rope_angles(s):
    rows = s // GRID_W
    row_ids = jnp.repeat(jnp.arange(rows), GRID_W).astype(jnp.float32)
    col_ids = jnp.tile(jnp.arange(GRID_W), rows).astype(jnp.float32)
    axis_dim = HEAD_DIM // 2
    inv_freq = ROPE_THETA ** (-jnp.arange(0, axis_dim, 2, dtype=jnp.float32) / axis_dim)
    ang = jnp.stack([row_ids[:, None] * inv_freq, col_ids[:, None] * inv_freq], axis=1)
    return jnp.cos(ang), jnp.sin(ang)


def apply_axial_rope(x, cos, sin):
    s = x.shape[1]
    quarter = HEAD_DIM // 4
    xr = x.reshape(x.shape[:-1] + (2, 2, quarter))
    bshape = (1, s) + (1,) * (x.ndim - 3) + (2, quarter)
    c = cos.reshape(bshape)
    sn = sin.reshape(bshape)
    x1 = xr[..., 0, :]
    x2 = xr[..., 1, :]
    rot = jnp.stack([x1 * c - x2 * sn, x2 * c + x1 * sn], axis=-2)
    return rot.reshape(x.shape)


def block_sweep_gqa(q, k, v):
    b, s, kvh, g, hd = q.shape
    nb = s // B_QUERY_BLOCK
    qb = jnp.moveaxis(q.reshape(b, nb, B_QUERY_BLOCK, kvh, g, hd), 1, 0)
    kf = k.astype(jnp.float32)
    vf = v.astype(jnp.float32)

    def one_block(qblk):
        logits = jnp.einsum('bqkgd,bskd->bkgqs', qblk.astype(jnp.float32), kf) / math.sqrt(hd)
        p = jax.nn.softmax(logits, axis=-1)
        return jnp.einsum('bkgqs,bskd->bqkgd', p, vf)

    out = lax.map(one_block, qb)
    return jnp.moveaxis(out, 0, 1).reshape(b, s, kvh, g, hd)


def _stride_split(z, r):
    b, s = z.shape[:2]
    z = z.reshape((b, s // r, r) + z.shape[2:])
    return jnp.swapaxes(z, 1, 2).reshape((b * r, s // r) + z.shape[3:])


def _stride_merge(z, b, r):
    m = z.shape[1]
    z = z.reshape((b, r, m) + z.shape[2:])
    return jnp.swapaxes(z, 1, 2).reshape((b, r * m) + z.shape[3:])


def dilated_mixture(q, k, v, bias_table):
    b = q.shape[0]
    outs, lses = [], []
    for i, (window, dilation) in enumerate(C_PAIRS):
        o, l = banded_gqa(_stride_split(q[:, :, i], dilation), _stride_split(k[:, :, i], dilation),
                          _stride_split(v[:, :, i], dilation), window // (2 * dilation), dilation,
                          bias_table[:, i], None)
        outs.append(_stride_merge(o, b, dilation))
        lses.append(_stride_merge(l, b, dilation))
    weights = jax.nn.softmax(jnp.stack(lses, axis=0), axis=0)
    return jnp.sum(weights[..., None] * jnp.stack(outs, axis=0), axis=0)


def encoder_layer(x, ffn1_norm, ffn1_w13, ffn1_w2, mix_norm, w_in, q_gain_b, k_gain_b, sink_a,
                  w_gate, b_gate, w_br_a, w_br_b, w_br_c, w_o, ffn2_norm, ffn2_w13, ffn2_w2, rel_bias):
    b, s, _ = x.shape
    dt = x.dtype
    x = x + FFN_RESIDUAL * swiglu(rms_norm(x, ffn1_norm), ffn1_w13, ffn1_w2)
    h = rms_norm(x, mix_norm)
    z = h @ w_in
    aq, ak, av, bq, bk, bv, cq, ck, cv = jnp.split(z, IN_SPLIT_POINTS, axis=-1)
    ya, _ = banded_gqa(aq.reshape(b, s, A_KV_HEADS, A_GROUP, HEAD_DIM),
                       ak.reshape(b, s, A_KV_HEADS, HEAD_DIM),
                       av.reshape(b, s, A_KV_HEADS, HEAD_DIM),
                       A_HALF_WINDOW, 1,
                       rel_bias[:, :A_Q_HEADS].reshape(NUM_BUCKETS, A_KV_HEADS, A_GROUP),
                       sink_a.reshape(A_KV_HEADS, A_GROUP))
    cos, sin = axial_rope_angles(s)
    qb = apply_axial_rope(head_rms_norm(bq.reshape(b, s, B_KV_HEADS, B_GROUP, HEAD_DIM), q_gain_b), cos, sin)
    kb = apply_axial_rope(head_rms_norm(bk.reshape(b, s, B_KV_HEADS, HEAD_DIM), k_gain_b), cos, sin)
    yb = block_sweep_gqa(qb, kb, bv.reshape(b, s, B_KV_HEADS, HEAD_DIM))
    yc = dilated_mixture(cq.reshape(b, s, N_PAIRS, C_KV_PER_PAIR, C_GROUP, HEAD_DIM),
                         ck.reshape(b, s, N_PAIRS, C_KV_PER_PAIR, HEAD_DIM),
                         cv.reshape(b, s, N_PAIRS, C_KV_PER_PAIR, HEAD_DIM),
                         rel_bias[:, A_Q_HEADS:].reshape(NUM_BUCKETS, N_PAIRS, C_KV_PER_PAIR, C_GROUP))
    gates = jax.nn.sigmoid((h @ w_gate + b_gate).astype(jnp.float32)).reshape(b, s, N_BRANCHES, D_MODEL)
    br_a = (ya.reshape(b, s, -1).astype(dt) @ w_br_a).astype(jnp.float32)
    br_b = (yb.reshape(b, s, -1).astype(dt) @ w_br_b).astype(jnp.float32)
    br_c = (yc.reshape(b, s, -1).astype(dt) @ w_br_c).astype(jnp.float32)
    merged = gates[:, :, 0] * br_a + gates[:, :, 1] * br_b + gates[:, :, 2] * br_c
    x = x + merged.astype(dt) @ w_o
    x = x + FFN_RESIDUAL * swiglu(rms_norm(x, ffn2_norm), ffn2_w13, ffn2_w2)
    return x


def encoder_trunk(x, ffn1_norm, ffn1_w13, ffn1_w2, mix_norm, w_in, q_gain_b, k_gain_b, sink_a,
                  w_gate, b_gate, w_br_a, w_br_b, w_br_c, w_o, ffn2_norm, ffn2_w13, ffn2_w2,
                  rel_bias, final_norm):
    for li in range(DEPTH):
        x = encoder_layer(x, ffn1_norm[li], ffn1_w13[li], ffn1_w2[li], mix_norm[li], w_in[li],
                          q_gain_b[li], k_gain_b[li], sink_a[li], w_gate[li], b_gate[li],
                          w_br_a[li], w_br_b[li], w_br_c[li], w_o[li],
                          ffn2_norm[li], ffn2_w13[li], ffn2_w2[li], rel_bias)
    return rms_norm(x, final_norm)


def _normal(key, shape, scale):
    return jax.random.normal(key, shape, jnp.float32) * scale


def setup_inputs(seed: int = 0) -> dict:
    key = jax.random.key(seed)
    ks = jax.random.split(key, 21)
    D, L = D_MODEL, DEPTH
    return {
        'x_prompt': _normal(ks[0], (BATCH, SEQ, D), 1.0),
        'x_sample': _normal(ks[1], (DEC_BATCH, DEC_SEQ, D), 1.0),
        'ffn1_norm': 1.0 + _normal(ks[2], (L, D), 0.01),
        'ffn1_w13': _normal(ks[3], (L, D, 2 * D_FF), D ** -0.5),
        'ffn1_w2': _normal(ks[4], (L, D_FF, D), D_FF ** -0.5),
        'mix_norm': 1.0 + _normal(ks[5], (L, D), 0.01),
        'w_in': _normal(ks[6], (L, D, IN_COLS), D ** -0.5),
        'q_gain_b': 1.0 + _normal(ks[7], (L, HEAD_DIM), 0.01),
        'k_gain_b': 1.0 + _normal(ks[8], (L, HEAD_DIM), 0.01),
        'sink_a': _normal(ks[9], (L, A_Q_HEADS), 0.5),
        'w_gate': _normal(ks[10], (L, D, N_BRANCHES * D), D ** -0.5),
        'b_gate': _normal(ks[11], (L, N_BRANCHES * D), 0.02),
        'w_br_a': _normal(ks[12], (L, A_Q_HEADS * HEAD_DIM, D), (A_Q_HEADS * HEAD_DIM) ** -0.5),
        'w_br_b': _normal(ks[13], (L, B_Q_HEADS * HEAD_DIM, D), (B_Q_HEADS * HEAD_DIM) ** -0.5),
        'w_br_c': _normal(ks[14], (L, C_SLOTS * HEAD_DIM, D), (C_SLOTS * HEAD_DIM) ** -0.5),
        'w_o': _normal(ks[15], (L, D, D), D ** -0.5),
        'ffn2_norm': 1.0 + _normal(ks[16], (L, D), 0.01),
        'ffn2_w13': _normal(ks[17], (L, D, 2 * D_FF), D ** -0.5),
        'ffn2_w2': _normal(ks[18], (L, D_FF, D), D_FF ** -0.5),
        'rel_bias': _normal(ks[19], (NUM_BUCKETS, A_Q_HEADS + C_Q_HEADS), 0.1),
        'final_norm': 1.0 + _normal(ks[20], (D,), 0.01),
    }


def reference(x_prompt, x_sample, ffn1_norm, ffn1_w13, ffn1_w2, mix_norm, w_in, q_gain_b, k_gain_b,
              sink_a, w_gate, b_gate, w_br_a, w_br_b, w_br_c, w_o, ffn2_norm, ffn2_w13, ffn2_w2,
              rel_bias, final_norm):
    y_prompt = encoder_trunk(x_prompt, ffn1_norm, ffn1_w13, ffn1_w2, mix_norm, w_in, q_gain_b, k_gain_b,
                             sink_a, w_gate, b_gate, w_br_a, w_br_b, w_br_c, w_o, ffn2_norm, ffn2_w13,
                             ffn2_w2, rel_bias, final_norm)
    y_sample = encoder_trunk(x_sample, ffn1_norm, ffn1_w13, ffn1_w2, mix_norm, w_in, q_gain_b, k_gain_b,
                             sink_a, w_gate, b_gate, w_br_a, w_br_b, w_br_c, w_o, ffn2_norm, ffn2_w13,
                             ffn2_w2, rel_bias, final_norm)
    return (y_prompt, y_sample)
```

```python
import functools
import math

import jax
import jax.numpy as jnp
from jax import lax
from jax.experimental import pallas as pl
from jax.experimental.pallas import tpu as pltpu

F32 = jnp.float32
BF16 = jnp.bfloat16

HEAD_DIM = 128
NORM_EPS = 1e-6
MASK_VALUE = -1e30
QK_SCALE = 1.0 / math.sqrt(HEAD_DIM)
FFN_RESIDUAL = 0.5
ROPE_THETA = 10000.0
GRID_W = 64
NUM_BUCKETS = 32
REL_MAX_DISTANCE = 2048

A_Q, A_K, A_V = 0, 4, 6
B_Q, B_K, B_V = 8, 14, 16
C_Q, C_K, C_V = 18, 24, 27
N_HEAD_BLOCKS = 30
A_HEADS = 4
A_HALF_WINDOW = 128
C_HALF_WINDOW = 64
C_DILATIONS = (1, 4, 16)

V7X_VMEM_BYTES = 64 * 1024 * 1024
MIB = 1024 * 1024


def _cparams(n_grid, vmem_bytes):
    assert vmem_bytes < V7X_VMEM_BYTES, vmem_bytes
    return pltpu.CompilerParams(dimension_semantics=("arbitrary",) * n_grid,
                                vmem_limit_bytes=int(vmem_bytes))


def _rms(x, gain):
    ms = jnp.mean(x * x, axis=-1, keepdims=True)
    return x * lax.rsqrt(ms + NORM_EPS) * gain


def _ffn_kernel(*refs, n_steps, final):
    if final:
        x_ref, g_ref, wg_ref, wu_ref, w2_ref, fin_ref, o_ref, xn_ref = refs
    else:
        x_ref, g_ref, wg_ref, wu_ref, w2_ref, o_ref, xn_ref = refs
    j = pl.program_id(1)

    @pl.when(j == 0)
    def _():
        xn_ref[...] = _rms(x_ref[...], g_ref[...]).astype(BF16)

    xn = xn_ref[...]
    gate = jnp.dot(xn, wg_ref[...], preferred_element_type=F32)
    up = jnp.dot(xn, wu_ref[...], preferred_element_type=F32)
    act = (gate * jax.nn.sigmoid(gate) * up).astype(BF16)
    contrib = jnp.dot(act, w2_ref[...], preferred_element_type=F32)

    @pl.when(j == 0)
    def _():
        o_ref[...] = contrib

    @pl.when(j > 0)
    def _():
        o_ref[...] += contrib

    @pl.when(j == n_steps - 1)
    def _():
        y = x_ref[...] + FFN_RESIDUAL * o_ref[...]
        if final:
            y = _rms(y, fin_ref[...])
        o_ref[...] = y


def _ffn(x, gain, w13, w2, final_gain=None, *, tm=512, tf=512):
    m, d = x.shape
    d_ff = w2.shape[0]
    assert m % tm == 0 and d_ff % tf == 0
    n_steps = d_ff // tf
    final = final_gain is not None
    in_specs = [
        pl.BlockSpec((tm, d), lambda i, j: (i, 0)),
        pl.BlockSpec((1, d), lambda i, j: (0, 0)),
        pl.BlockSpec((d, tf), lambda i, j: (0, j)),
        pl.BlockSpec((d, tf), lambda i, j: (0, j + n_steps)),
        pl.BlockSpec((tf, d), lambda i, j: (j, 0)),
    ]
    args = [x, gain.reshape(1, d), w13, w13, w2]
    if final:
        in_specs.append(pl.BlockSpec((1, d), lambda i, j: (0, 0)))
        args.append(final_gain.reshape(1, d))
    vmem = (2 * tm * d * 4) * 2 + tm * d * 2 + 2 * (3 * d * tf * 2) \
        + 2 * tm * tf * 4 + tm * tf * 2 + 2 * tm * d * 4 + 2 * MIB
    return pl.pallas_call(
        functools.partial(_ffn_kernel, n_steps=n_steps, final=final),
        grid=(m // tm, n_steps),
        in_specs=in_specs,
        out_specs=pl.BlockSpec((tm, d), lambda i, j: (i, 0)),
        out_shape=jax.ShapeDtypeStruct((m, d), F32),
        scratch_shapes=[pltpu.VMEM((tm, d), BF16)],
        compiler_params=_cparams(2, vmem),
        name="ffn_final" if final else "ffn",
    )(*args)


def _swap_quarter_pairs(x):
    q = HEAD_DIM // 4
    lane = lax.broadcasted_iota(jnp.int32, x.shape, 1)
    from_right = pltpu.roll(x, HEAD_DIM - q, axis=1)
    from_left = pltpu.roll(x, q, axis=1)
    return jnp.where((lane % (2 * q)) < q, from_right, from_left)


def _mixin_kernel(x_ref, g_ref, w_ref, qg_ref, kg_ref, cos_ref, sin_ref, h_ref, z_ref):
    h = _rms(x_ref[...], g_ref[...]).astype(BF16)
    h_ref[...] = h
    cos = cos_ref[...]
    sin = sin_ref[...]

    def norm_rope(v, gain):
        ms = jnp.mean(v * v, axis=-1, keepdims=True)
        y = v * lax.rsqrt(ms + NORM_EPS) * gain
        return y * cos + _swap_quarter_pairs(y) * sin

    for c in range(N_HEAD_BLOCKS // 2):
        zc = jnp.dot(h, w_ref[:, c * 256:(c + 1) * 256], preferred_element_type=F32)
        for hh in range(2):
            blk = 2 * c + hh
            v = zc[:, hh * HEAD_DIM:(hh + 1) * HEAD_DIM]
            if B_Q <= blk < B_K:
                v = norm_rope(v, qg_ref[...]) * QK_SCALE
            elif B_K <= blk < B_V:
                v = norm_rope(v, kg_ref[...])
            elif blk < A_K or C_Q <= blk < C_K:
                v = v * QK_SCALE
            z_ref[:, blk * HEAD_DIM:(blk + 1) * HEAD_DIM] = v.astype(BF16)


def _mix_in(x, gain, w_in, q_gain, k_gain, cos_tab, sin_tab, *, tm=512):
    m, d = x.shape
    s = cos_tab.shape[0]
    n_cols = w_in.shape[1]
    assert m % tm == 0 and s % tm == 0 and n_cols == N_HEAD_BLOCKS * HEAD_DIM
    pos_tiles = s // tm
    vmem = 2 * tm * d * 4 + 2 * tm * d * 2 + 2 * tm * n_cols * 2 + d * n_cols * 2 \
        + 4 * tm * HEAD_DIM * 4 + tm * d * 4 + 8 * tm * 256 * 4 + 2 * MIB
    return pl.pallas_call(
        _mixin_kernel,
        grid=(m // tm,),
        in_specs=[
            pl.BlockSpec((tm, d), lambda i: (i, 0)),
            pl.BlockSpec((1, d), lambda i: (0, 0)),
            pl.BlockSpec((d, n_cols), lambda i: (0, 0), pipeline_mode=pl.Buffered(1)),
            pl.BlockSpec((1, HEAD_DIM), lambda i: (0, 0)),
            pl.BlockSpec((1, HEAD_DIM), lambda i: (0, 0)),
            pl.BlockSpec((tm, HEAD_DIM), lambda i: (i % pos_tiles, 0)),
            pl.BlockSpec((tm, HEAD_DIM), lambda i: (i % pos_tiles, 0)),
        ],
        out_specs=[
            pl.BlockSpec((tm, d), lambda i: (i, 0)),
            pl.BlockSpec((tm, n_cols), lambda i: (i, 0)),
        ],
        out_shape=[jax.ShapeDtypeStruct((m, d), BF16), jax.ShapeDtypeStruct((m, n_cols), BF16)],
        compiler_params=_cparams(1, vmem),
        name="mix_in",
    )(x, gain.reshape(1, d), w_in, q_gain.reshape(1, HEAD_DIM), k_gain.reshape(1, HEAD_DIM),
      cos_tab, sin_tab)


def _t5_bucket(rel):
    half = NUM_BUCKETS // 2
    max_exact = half // 2
    ret = jnp.where(rel > 0, half, 0)
    n = jnp.abs(rel)
    nf = jnp.maximum(n, 1).astype(jnp.float32)
    large = max_exact + (jnp.log(nf / max_exact) / math.log(REL_MAX_DISTANCE / max_exact)
                         * (half - max_exact)).astype(jnp.int32)
    large = jnp.minimum(large, half - 1)
    return ret + jnp.where(n < max_exact, n, large)


def _band_rel(tq, halo):
    return jnp.arange(tq + 2 * halo)[None, :] - halo - jnp.arange(tq)[:, None]


def _band_kernel(*refs, halo, seq_len, n_heads, head0, n_table_cols, with_sink):
    if with_sink:
        (bkt_ref, tab_ref, sink_ref, q0_ref, q1_ref, kl_ref, km_ref, kr_ref, vl_ref, vm_ref, vr_ref,
         o_ref, bias_ref) = refs
        lse_ref = None
    else:
        (bkt_ref, tab_ref, q0_ref, q1_ref, kl_ref, km_ref, kr_ref, vl_ref, vm_ref, vr_ref,
         o_ref, lse_ref, bias_ref) = refs
        sink_ref = None
    tq = q0_ref.shape[1]
    nk = tq + 2 * halo
    first = (pl.program_id(0) == 0) & (pl.program_id(1) == 0) & (pl.program_id(2) == 0)

    @pl.when(first)
    def _():
        bkt = bkt_ref[...]
        rel = (lax.broadcasted_iota(jnp.int32, (tq, nk), 1) - halo
               - lax.broadcasted_iota(jnp.int32, (tq, nk), 0))
        in_band = jnp.abs(rel) <= halo
        tiles = [jnp.zeros((tq, nk), F32) for _ in range(n_heads)]
        for t in range(NUM_BUCKETS):
            hit = bkt == t
            for h in range(n_heads):
                tiles[h] = jnp.where(hit, tab_ref[t * n_table_cols + head0 + h], tiles[h])
        for h in range(n_heads):
            bias_ref[h] = jnp.where(in_band, tiles[h], MASK_VALUE)

    qi = pl.program_id(2)
    key_pos = qi * tq - halo + lax.broadcasted_iota(jnp.int32, (1, nk), 1)
    key_ok = (key_pos >= 0) & (key_pos < seq_len)
    k = jnp.concatenate([kl_ref[0], km_ref[0], kr_ref[0]], axis=0)
    v = jnp.concatenate([vl_ref[0], vm_ref[0], vr_ref[0]], axis=0)
    for g, q_ref in enumerate((q0_ref, q1_ref)):
        head = pl.program_id(1) * 2 + g if with_sink else g
        logits = lax.dot_general(q_ref[0], k, (((1,), (1,)), ((), ())), preferred_element_type=F32)
        logits = jnp.where(key_ok, logits + bias_ref[head], MASK_VALUE)
        m = jnp.max(logits, axis=-1, keepdims=True)
        if with_sink:
            sink = sink_ref[head]
            m = jnp.maximum(m, sink)
        p = jnp.exp(logits - m)
        denom = jnp.sum(p, axis=-1, keepdims=True)
        if with_sink:
            denom = denom + jnp.exp(sink - m)
        out = jnp.dot(p.astype(BF16), v, preferred_element_type=F32) / denom
        o_ref[0, :, g * HEAD_DIM:(g + 1) * HEAD_DIM] = out.astype(o_ref.dtype)
        if lse_ref is not None:
            lse_ref[0, :, g * HEAD_DIM:(g + 1) * HEAD_DIM] = jnp.broadcast_to(
                m + jnp.log(denom), (tq, HEAD_DIM))


def _band_specs(tq, halo, q_col, k_col, v_col, n_halo_blocks):
    per = tq // halo

    def tile(col):
        return pl.BlockSpec((1, tq, HEAD_DIM), lambda b, c, i: (b, i, col(c)))

    def left(col):
        return pl.BlockSpec((1, halo, HEAD_DIM), lambda b, c, i: (b, jnp.maximum(i * per - 1, 0), col(c)))

    def right(col):
        return pl.BlockSpec((1, halo, HEAD_DIM),
                            lambda b, c, i: (b, jnp.minimum((i + 1) * per, n_halo_blocks - 1), col(c)))

    return [tile(lambda c: q_col(c)), tile(lambda c: q_col(c) + 1),
            left(k_col), tile(k_col), right(k_col), left(v_col), tile(v_col), right(v_col)]


def _band_vmem(tq, halo, n_heads, n_out):
    nk = tq + 2 * halo
    return (n_heads * tq * nk * 4 + 2 * tq * nk * 4 + 2 * (2 * tq + 2 * nk) * HEAD_DIM * 2
            + 2 * n_out * tq * 256 * 4 + 6 * tq * nk * 4 + 2 * nk * HEAD_DIM * 2 + 2 * MIB)


def _attn_a(z3, rel_bias_flat, sink, *, tq=256):
    b, s, _ = z3.shape
    halo = A_HALF_WINDOW
    assert s % tq == 0 and tq % halo == 0
    nk = tq + 2 * halo
    bkt = _t5_bucket(_band_rel(tq, halo)).astype(jnp.int32)
    smem = pl.BlockSpec(memory_space=pltpu.SMEM)
    specs = _band_specs(tq, halo, lambda c: A_Q + 2 * c, lambda c: A_K + c, lambda c: A_V + c, s // halo)
    return pl.pallas_call(
        functools.partial(_band_kernel, halo=halo, seq_len=s, n_heads=A_HEADS, head0=0,
                          n_table_cols=rel_bias_flat.shape[0] // NUM_BUCKETS, with_sink=True),
        grid=(b, 2, s // tq),
        in_specs=[pl.BlockSpec((tq, nk), lambda b_, c, i: (0, 0)), smem, smem] + specs,
        out_specs=pl.BlockSpec((1, tq, 2 * HEAD_DIM), lambda b_, c, i: (b_, i, c)),
        out_shape=jax.ShapeDtypeStruct((b, s, A_HEADS * HEAD_DIM), BF16),
        scratch_shapes=[pltpu.VMEM((A_HEADS, tq, nk), F32)],
        compiler_params=_cparams(3, _band_vmem(tq, halo, A_HEADS, 1)),
        name="attn_a",
    )(bkt, rel_bias_flat, sink, *([z3] * 8))


def _attn_c_pair(z3, rel_bias_flat, pair, *, tq=128):
    b, s, n_cols = z3.shape
    r = C_DILATIONS[pair]
    halo = C_HALF_WINDOW
    sub = s // r
    assert sub % tq == 0 and tq % halo == 0
    nk = tq + 2 * halo
    zr = z3.reshape(b, sub, r * n_cols)
    bkt = _t5_bucket(_band_rel(tq, halo) * r).astype(jnp.int32)
    smem = pl.BlockSpec(memory_space=pltpu.SMEM)
    specs = _band_specs(tq, halo,
                        lambda c: c * N_HEAD_BLOCKS + C_Q + 2 * pair,
                        lambda c: c * N_HEAD_BLOCKS + C_K + pair,
                        lambda c: c * N_HEAD_BLOCKS + C_V + pair, sub // halo)
    out_spec = pl.BlockSpec((1, tq, 2 * HEAD_DIM), lambda b_, c, i: (b_, i, c))
    out, lse = pl.pallas_call(
        functools.partial(_band_kernel, halo=halo, seq_len=sub, n_heads=2, head0=A_HEADS + 2 * pair,
                          n_table_cols=rel_bias_flat.shape[0] // NUM_BUCKETS, with_sink=False),
        grid=(b, r, sub // tq),
        in_specs=[pl.BlockSpec((tq, nk), lambda b_, c, i: (0, 0)), smem] + specs,
        out_specs=[out_spec, out_spec],
        out_shape=[jax.ShapeDtypeStruct((b, sub, r * 2 * HEAD_DIM), F32)] * 2,
        scratch_shapes=[pltpu.VMEM((2, tq, nk), F32)],
        compiler_params=_cparams(3, _band_vmem(tq, halo, 2, 2)),
        name=f"attn_c{pair}",
    )(bkt, rel_bias_flat, *([zr] * 8))
    return out.reshape(b, s, 2 * HEAD_DIM), lse.reshape(b, s, 2 * HEAD_DIM)


def _c_merge_kernel(o0, o1, o2, l0, l1, l2, y_ref):
    la, lb, lc = l0[...], l1[...], l2[...]
    m = jnp.maximum(jnp.maximum(la, lb), lc)
    ea, eb, ec = jnp.exp(la - m), jnp.exp(lb - m), jnp.exp(lc - m)
    tot = ea + eb + ec
    y = (ea / tot) * o0[...] + (eb / tot) * o1[...] + (ec / tot) * o2[...]
    y_ref[...] = y.astype(y_ref.dtype)


def _attn_c(z3, rel_bias_flat, *, tm=1024):
    b, s, _ = z3.shape
    outs, lses = zip(*[_attn_c_pair(z3, rel_bias_flat, p) for p in range(len(C_DILATIONS))])
    m = b * s
    w = 2 * HEAD_DIM
    flat = [a.reshape(m, w) for a in outs + lses]
    spec = pl.BlockSpec((tm, w), lambda i: (i, 0))
    return pl.pallas_call(
        _c_merge_kernel,
        grid=(m // tm,),
        in_specs=[spec] * 6,
        out_specs=spec,
        out_shape=jax.ShapeDtypeStruct((m, w), BF16),
        compiler_params=_cparams(1, 2 * 6 * tm * w * 4 + 2 * tm * w * 2 + 8 * tm * w * 4 + 2 * MIB),
        name="attn_c_merge",
    )(*flat)


def _attn_b_kernel(q0_ref, q1_ref, q2_ref, k_ref, v_ref, o_ref, *, tk):
    tq = q0_ref.shape[1]
    s = k_ref.shape[1]
    q = jnp.concatenate([q0_ref[0], q1_ref[0], q2_ref[0]], axis=0)
    m = l = acc = None
    for c in range(s // tk):
        kc = k_ref[0, c * tk:(c + 1) * tk, :]
        vc = v_ref[0, c * tk:(c + 1) * tk, :]
        logits = lax.dot_general(q, kc, (((1,), (1,)), ((), ())), preferred_element_type=F32)
        cmax = jnp.max(logits, axis=-1, keepdims=True)
        if c == 0:
            m = cmax
            p = jnp.exp(logits - m)
            l = jnp.sum(p, axis=-1, keepdims=True)
            acc = jnp.dot(p.astype(BF16), vc, preferred_element_type=F32)
        else:
            m_new = jnp.maximum(m, cmax)
            alpha = jnp.exp(m - m_new)
            p = jnp.exp(logits - m_new)
            l = alpha * l + jnp.sum(p, axis=-1, keepdims=True)
            acc = alpha * acc + jnp.dot(p.astype(BF16), vc, preferred_element_type=F32)
            m = m_new
    out = acc / l
    for g in range(3):
        o_ref[0, :, g * HEAD_DIM:(g + 1) * HEAD_DIM] = out[g * tq:(g + 1) * tq].astype(o_ref.dtype)


def _attn_b(z3, *, tq=256, tk=512):
    b, s, _ = z3.shape
    assert s % tq == 0 and s % tk == 0

    def qspec(g):
        return pl.BlockSpec((1, tq, HEAD_DIM), lambda b_, c, i: (b_, i, B_Q + 3 * c + g))

    vmem = 2 * 3 * tq * HEAD_DIM * 2 + 2 * 2 * s * HEAD_DIM * 2 + 2 * tq * 384 * 2 \
        + 8 * 3 * tq * tk * 4 + 8 * 3 * tq * HEAD_DIM * 4 + 2 * MIB
    return pl.pallas_call(
        functools.partial(_attn_b_kernel, tk=tk),
        grid=(b, 2, s // tq),
        in_specs=[qspec(0), qspec(1), qspec(2),
                  pl.BlockSpec((1, s, HEAD_DIM), lambda b_, c, i: (b_, 0, B_K + c)),
                  pl.BlockSpec((1, s, HEAD_DIM), lambda b_, c, i: (b_, 0, B_V + c))],
        out_specs=pl.BlockSpec((1, tq, 3 * HEAD_DIM), lambda b_, c, i: (b_, i, c)),
        out_shape=jax.ShapeDtypeStruct((b, s, 6 * HEAD_DIM), BF16),
        compiler_params=_cparams(3, vmem),
        name="attn_b",
    )(z3, z3, z3, z3, z3)


def _merge_kernel(x_ref, h_ref, ya_ref, yb_ref, yc_ref, wga_ref, wgb_ref, wgc_ref,
                  bga_ref, bgb_ref, bgc_ref, wa_ref, wb_ref, wc_ref, wo_ref, o_ref, *, n_steps):
    j = pl.program_id(1)
    h = h_ref[...]

    def branch(y_ref, w_ref, wg_ref, bg_ref):
        gate = jax.nn.sigmoid(jnp.dot(h, wg_ref[...], preferred_element_type=F32) + bg_ref[...])
        return gate * jnp.dot(y_ref[...], w_ref[...], preferred_element_type=F32)

    merged = (branch(ya_ref, wa_ref, wga_ref, bga_ref) + branch(yb_ref, wb_ref, wgb_ref, bgb_ref)
              + branch(yc_ref, wc_ref, wgc_ref, bgc_ref))
    contrib = jnp.dot(merged.astype(BF16), wo_ref[...], preferred_element_type=F32)

    @pl.when(j == 0)
    def _():
        o_ref[...] = contrib

    @pl.when(j > 0)
    def _():
        o_ref[...] += contrib

    @pl.when(j == n_steps - 1)
    def _():
        o_ref[...] = x_ref[...] + o_ref[...]


def _merge(x, h, ya, yb, yc, w_gate, b_gate, w_br_a, w_br_b, w_br_c, w_o, *, tm=512, tn=512):
    m, d = x.shape
    assert m % tm == 0 and d % tn == 0
    n_steps = d // tn
    ka, kb, kc = ya.shape[1], yb.shape[1], yc.shape[1]

    def row(width):
        return pl.BlockSpec((tm, width), lambda i, j: (i, 0))

    def gate_w(br):
        return pl.BlockSpec((d, tn), lambda i, j: (0, br * n_steps + j))

    def gate_b(br):
        return pl.BlockSpec((1, tn), lambda i, j: (0, br * n_steps + j))

    def br_w(k):
        return pl.BlockSpec((k, tn), lambda i, j: (0, j))

    bg = b_gate.reshape(1, 3 * d)
    vmem = 2 * 2 * tm * d * 4 + 2 * tm * d * 2 + 2 * tm * (ka + kb + kc) * 2 \
        + 2 * (3 * d + ka + kb + kc + d) * tn * 2 + 8 * tm * tn * 4 + 2 * tm * d * 4 + 2 * MIB
    return pl.pallas_call(
        functools.partial(_merge_kernel, n_steps=n_steps),
        grid=(m // tm, n_steps),
        in_specs=[row(d), row(d), row(ka), row(kb), row(kc),
                  gate_w(0), gate_w(1), gate_w(2), gate_b(0), gate_b(1), gate_b(2),
                  br_w(ka), br_w(kb), br_w(kc),
                  pl.BlockSpec((tn, d), lambda i, j: (j, 0))],
        out_specs=row(d),
        out_shape=jax.ShapeDtypeStruct((m, d), F32),
        compiler_params=_cparams(2, vmem),
        name="merge",
    )(x, h, ya, yb, yc, w_gate, w_gate, w_gate, bg, bg, bg, w_br_a, w_br_b, w_br_c, w_o)


def _rope_tables(s):
    quarter = HEAD_DIM // 4
    t = jnp.arange(s)
    row_ids = (t // GRID_W).astype(F32)
    col_ids = (t % GRID_W).astype(F32)
    axis_dim = HEAD_DIM // 2
    inv_freq = ROPE_THETA ** (-jnp.arange(0, axis_dim, 2, dtype=F32) / axis_dim)
    ang_r = row_ids[:, None] * inv_freq
    ang_c = col_ids[:, None] * inv_freq
    cos = jnp.concatenate([jnp.cos(ang_r), jnp.cos(ang_r), jnp.cos(ang_c), jnp.cos(ang_c)], axis=1)
    sin = jnp.concatenate([-jnp.sin(ang_r), jnp.sin(ang_r), -jnp.sin(ang_c), jnp.sin(ang_c)], axis=1)
    assert cos.shape == (s, 4 * quarter)
    return cos, sin


def _trunk(x, p):
    b, s, d = x.shape
    depth = p["w_in"].shape[0]
    cos_tab, sin_tab = _rope_tables(s)
    rel_flat = p["rel_bias"].reshape(-1)
    xf = x.reshape(b * s, d)
    for li in range(depth):
        xf = _ffn(xf, p["ffn1_norm"][li], p["ffn1_w13"][li], p["ffn1_w2"][li])
        h, z = _mix_in(xf, p["mix_norm"][li], p["w_in"][li], p["q_gain_b"][li], p["k_gain_b"][li],
                       cos_tab, sin_tab)
        z3 = z.reshape(b, s, z.shape[1])
        ya = _attn_a(z3, rel_flat, p["sink_a"][li]).reshape(b * s, -1)
        yb = _attn_b(z3).reshape(b * s, -1)
        yc = _attn_c(z3, rel_flat)
        xf = _merge(xf, h, ya, yb, yc, p["w_gate"][li], p["b_gate"][li], p["w_br_a"][li],
                    p["w_br_b"][li], p["w_br_c"][li], p["w_o"][li])
        fin = p["final_norm"] if li == depth - 1 else None
        xf = _ffn(xf, p["ffn2_norm"][li], p["ffn2_w13"][li], p["ffn2_w2"][li], fin)
    return xf.reshape(b, s, d)


_MATMUL_WEIGHTS = ("ffn1_w13", "ffn1_w2", "w_in", "w_gate", "w_br_a", "w_br_b", "w_br_c", "w_o",
                   "ffn2_w13", "ffn2_w2")


def kernel(x_prompt, x_sample, ffn1_norm, ffn1_w13, ffn1_w2, mix_norm, w_in, q_gain_b, k_gain_b, sink_a,
           w_gate, b_gate, w_br_a, w_br_b, w_br_c, w_o, ffn2_norm, ffn2_w13, ffn2_w2, rel_bias, final_norm):
    p = dict(ffn1_norm=ffn1_norm, ffn1_w13=ffn1_w13, ffn1_w2=ffn1_w2, mix_norm=mix_norm, w_in=w_in,
             q_gain_b=q_gain_b, k_gain_b=k_gain_b, sink_a=sink_a, w_gate=w_gate, b_gate=b_gate,
             w_br_a=w_br_a, w_br_b=w_br_b, w_br_c=w_br_c, w_o=w_o, ffn2_norm=ffn2_norm,
             ffn2_w13=ffn2_w13, ffn2_w2=ffn2_w2, rel_bias=rel_bias, final_norm=final_norm)
    for name in _MATMUL_WEIGHTS:
        p[name] = p[name].astype(BF16)
    return _trunk(x_prompt, p), _trunk(x_sample, p)
```

```python
import functools
import math

import jax
import jax.numpy as jnp
from jax import lax
from jax.experimental import pallas as pl
from jax.experimental.pallas import tpu as pltpu

F32 = jnp.float32
BF16 = jnp.bfloat16

HEAD_DIM = 128
NORM_EPS = 1e-6
MASK_VALUE = -1e30
QK_SCALE = 1.0 / math.sqrt(HEAD_DIM)
FFN_RESIDUAL = 0.5
ROPE_THETA = 10000.0
GRID_W = 64
NUM_BUCKETS = 32
REL_MAX_DISTANCE = 2048

A_Q, A_K, A_V = 0, 4, 6
B_Q, B_K, B_V = 8, 14, 16
C_Q, C_K, C_V = 18, 24, 27
N_HEAD_BLOCKS = 30
A_HEADS = 4
A_HALF_WINDOW = 128
C_HALF_WINDOW = 64
C_DILATIONS = (1, 4, 16)
PACKED = 4 * HEAD_DIM
ROW_CHUNK = 256

V7X_VMEM_BYTES = 64 * 1024 * 1024
MIB = 1024 * 1024


def _cparams(n_grid, vmem_bytes):
    assert vmem_bytes < V7X_VMEM_BYTES, vmem_bytes
    return pltpu.CompilerParams(dimension_semantics=("arbitrary",) * n_grid,
                                vmem_limit_bytes=int(vmem_bytes))


def _rms(x, gain):
    ms = jnp.mean(x * x, axis=-1, keepdims=True)
    return x * lax.rsqrt(ms + NORM_EPS) * gain


def _ffn_kernel(*refs, n_steps, n_chunk, final):
    if final:
        x_ref, g_ref, wg_ref, wu_ref, w2_ref, fin_ref, o_ref, xn_ref = refs
    else:
        x_ref, g_ref, wg_ref, wu_ref, w2_ref, o_ref, xn_ref = refs
    j = pl.program_id(1)

    row_chunks = [slice(r, r + ROW_CHUNK) for r in range(0, x_ref.shape[0], ROW_CHUNK)]

    @pl.when(j == 0)
    def _():
        for rows in row_chunks:
            xn_ref[rows, :] = _rms(x_ref[rows, :], g_ref[...]).astype(BF16)
        o_ref[...] = jnp.zeros_like(o_ref)

    xn = xn_ref[...]
    gate = jnp.dot(xn, wg_ref[...], preferred_element_type=F32)
    up = jnp.dot(xn, wu_ref[...], preferred_element_type=F32)
    act = (gate * jax.nn.sigmoid(gate) * up).astype(BF16)
    for n in range(o_ref.shape[1] // n_chunk):
        cols = slice(n * n_chunk, (n + 1) * n_chunk)
        o_ref[:, cols] += jnp.dot(act, w2_ref[:, cols], preferred_element_type=F32)

    @pl.when(j == n_steps - 1)
    def _():
        for rows in row_chunks:
            y = x_ref[rows, :] + FFN_RESIDUAL * o_ref[rows, :]
            if final:
                y = _rms(y, fin_ref[...])
            o_ref[rows, :] = y


def _ffn(x, gain, w13, w2, li, final_gain=None, *, tm=1024, tf=512, n_chunk=512):
    m, d = x.shape
    d_ff = w2.shape[1]
    n_chunk = min(n_chunk, d)
    assert m % tm == 0 and d_ff % tf == 0 and d % n_chunk == 0
    n_steps = d_ff // tf
    final = final_gain is not None
    in_specs = [
        pl.BlockSpec((tm, d), lambda i, j: (i, 0), pipeline_mode=pl.Buffered(1)),
        pl.BlockSpec((1, d), lambda i, j: (0, 0)),
        pl.BlockSpec((None, d, tf), lambda i, j: (li, 0, j)),
        pl.BlockSpec((None, d, tf), lambda i, j: (li, 0, j + n_steps)),
        pl.BlockSpec((None, tf, d), lambda i, j: (li, j, 0)),
    ]
    args = [x, gain.reshape(1, d), w13, w13, w2]
    if final:
        in_specs.append(pl.BlockSpec((1, d), lambda i, j: (0, 0)))
        args.append(final_gain.reshape(1, d))
    vmem = tm * d * 4 + 2 * tm * d * 4 + tm * d * 2 + 2 * (3 * d * tf * 2) \
        + 3 * tm * tf * 4 + tm * tf * 2 + 2 * tm * n_chunk * 4 + 3 * ROW_CHUNK * d * 4 + 2 * MIB
    return pl.pallas_call(
        functools.partial(_ffn_kernel, n_steps=n_steps, n_chunk=n_chunk, final=final),
        grid=(m // tm, n_steps),
        in_specs=in_specs,
        out_specs=pl.BlockSpec((tm, d), lambda i, j: (i, 0)),
        out_shape=jax.ShapeDtypeStruct((m, d), F32),
        scratch_shapes=[pltpu.VMEM((tm, d), BF16)],
        compiler_params=_cparams(2, vmem),
        name="ffn_final" if final else "ffn",
    )(*args)


def _swap_quarter_pairs(x):
    q = HEAD_DIM // 4
    lane = lax.broadcasted_iota(jnp.int32, x.shape, 1)
    from_right = pltpu.roll(x, HEAD_DIM - q, axis=1)
    from_left = pltpu.roll(x, q, axis=1)
    return jnp.where((lane % (2 * q)) < q, from_right, from_left)


def _mixin_kernel(x_ref, g_ref, w_ref, qg_ref, kg_ref, cos_ref, sin_ref,
                  h_ref, za_ref, zbq_ref, zbkv_ref, zc0_ref, zc1_ref, zc2_ref, cs_ref):
    tm = x_ref.shape[0]
    h = _rms(x_ref[...], g_ref[...]).astype(BF16)
    h_ref[...] = h
    cos = cos_ref[...]
    sin = sin_ref[...]

    def norm_rope(v, gain):
        ms = jnp.mean(v * v, axis=-1, keepdims=True)
        y = v * lax.rsqrt(ms + NORM_EPS) * gain
        return y * cos + _swap_quarter_pairs(y) * sin

    def lanes(slot):
        return slice(slot * HEAD_DIM, (slot + 1) * HEAD_DIM)

    for c in range(N_HEAD_BLOCKS // 2):
        zc = jnp.dot(h, w_ref[:, c * 256:(c + 1) * 256], preferred_element_type=F32)
        for hh in range(2):
            blk = 2 * c + hh
            v = zc[:, lanes(hh)]
            if blk < A_K:
                za_ref[0, blk // 2, :, lanes(blk % 2)] = (v * QK_SCALE).astype(BF16)
            elif blk < A_V:
                za_ref[0, blk - A_K, :, lanes(2)] = v.astype(BF16)
            elif blk < B_Q:
                za_ref[0, blk - A_V, :, lanes(3)] = v.astype(BF16)
            elif blk < B_K:
                q = norm_rope(v, qg_ref[...]) * QK_SCALE
                zbq_ref[0, (blk - B_Q) // 3, :, lanes((blk - B_Q) % 3)] = q.astype(BF16)
            elif blk < B_V:
                zbkv_ref[0, blk - B_K, :, lanes(0)] = norm_rope(v, kg_ref[...]).astype(BF16)
            elif blk < C_Q:
                zbkv_ref[0, blk - B_V, :, lanes(1)] = v.astype(BF16)
            elif blk < C_K:
                cs_ref[blk - C_Q] = v * QK_SCALE
            else:
                cs_ref[blk - C_Q] = v

    n_pairs = len(C_DILATIONS)
    for pair, (r, zc_ref) in enumerate(zip(C_DILATIONS, (zc0_ref, zc1_ref, zc2_ref))):
        staged = (2 * pair, 2 * pair + 1, 2 * n_pairs + pair, 3 * n_pairs + pair)
        for slot, src in enumerate(staged):
            for c in range(r):
                rows = pl.ds(c, tm // r, stride=r) if r > 1 else slice(None)
                zc_ref[0, c, :, lanes(slot)] = cs_ref[src, rows, :].astype(BF16)


def _mix_in(x, gain, w_in, li, q_gain, k_gain, cos_tab, sin_tab, *, tm=512):
    m, d = x.shape
    s = cos_tab.shape[0]
    b = m // s
    n_cols = w_in.shape[2]
    assert m % tm == 0 and s % tm == 0 and n_cols == N_HEAD_BLOCKS * HEAD_DIM
    assert all(tm % (16 * r) == 0 for r in C_DILATIONS)
    pos_tiles = s // tm

    def packed(n, width, rows):
        return pl.BlockSpec((1, n, rows, width), lambda i: (i // pos_tiles, 0, i % pos_tiles, 0))

    out_specs = [pl.BlockSpec((tm, d), lambda i: (i, 0)),
                 packed(2, PACKED, tm), packed(2, 3 * HEAD_DIM, tm), packed(2, 2 * HEAD_DIM, tm)]
    out_shape = [jax.ShapeDtypeStruct((m, d), BF16),
                 jax.ShapeDtypeStruct((b, 2, s, PACKED), BF16),
                 jax.ShapeDtypeStruct((b, 2, s, 3 * HEAD_DIM), BF16),
                 jax.ShapeDtypeStruct((b, 2, s, 2 * HEAD_DIM), BF16)]
    for r in C_DILATIONS:
        out_specs.append(packed(r, PACKED, tm // r))
        out_shape.append(jax.ShapeDtypeStruct((b, r, s // r, PACKED), BF16))
    c_cols = (N_HEAD_BLOCKS - C_Q) * HEAD_DIM
    vmem = 2 * tm * d * 4 + 2 * tm * d * 2 + 2 * tm * n_cols * 2 + d * n_cols * 2 \
        + 4 * tm * HEAD_DIM * 4 + tm * c_cols * 4 + tm * d * 4 + 8 * tm * 256 * 4 + 2 * MIB
    return pl.pallas_call(
        _mixin_kernel,
        grid=(m // tm,),
        in_specs=[
            pl.BlockSpec((tm, d), lambda i: (i, 0)),
            pl.BlockSpec((1, d), lambda i: (0, 0)),
            pl.BlockSpec((None, d, n_cols), lambda i: (li, 0, 0), pipeline_mode=pl.Buffered(1)),
            pl.BlockSpec((1, HEAD_DIM), lambda i: (0, 0)),
            pl.BlockSpec((1, HEAD_DIM), lambda i: (0, 0)),
            pl.BlockSpec((tm, HEAD_DIM), lambda i: (i % pos_tiles, 0)),
            pl.BlockSpec((tm, HEAD_DIM), lambda i: (i % pos_tiles, 0)),
        ],
        out_specs=out_specs,
        out_shape=out_shape,
        scratch_shapes=[pltpu.VMEM((N_HEAD_BLOCKS - C_Q, tm, HEAD_DIM), F32)],
        compiler_params=_cparams(1, vmem),
        name="mix_in",
    )(x, gain.reshape(1, d), w_in, q_gain.reshape(1, HEAD_DIM), k_gain.reshape(1, HEAD_DIM),
      cos_tab, sin_tab)


def _t5_bucket(rel):
    half = NUM_BUCKETS // 2
    max_exact = half // 2
    ret = jnp.where(rel > 0, half, 0)
    n = jnp.abs(rel)
    nf = jnp.maximum(n, 1).astype(jnp.float32)
    large = max_exact + (jnp.log(nf / max_exact) / math.log(REL_MAX_DISTANCE / max_exact)
                         * (half - max_exact)).astype(jnp.int32)
    large = jnp.minimum(large, half - 1)
    return ret + jnp.where(n < max_exact, n, large)


def _band_rel(tq, halo):
    return jnp.arange(tq + 2 * halo)[None, :] - halo - jnp.arange(tq)[:, None]


def _band_kernel(*refs, halo, seq_len, n_heads, head0, n_table_cols, with_sink):
    if with_sink:
        bkt_ref, tab_ref, sink_ref, left_ref, main_ref, right_ref, o_ref, bias_ref = refs
        lse_ref = None
    else:
        bkt_ref, tab_ref, left_ref, main_ref, right_ref, o_ref, lse_ref, bias_ref = refs
        sink_ref = None
    tq = main_ref.shape[2]
    nk = tq + 2 * halo
    first = (pl.program_id(0) == 0) & (pl.program_id(1) == 0) & (pl.program_id(2) == 0)

    @pl.when(first)
    def _():
        bkt = bkt_ref[...]
        rel = (lax.broadcasted_iota(jnp.int32, (tq, nk), 1) - halo
               - lax.broadcasted_iota(jnp.int32, (tq, nk), 0))
        in_band = jnp.abs(rel) <= halo
        tiles = [jnp.zeros((tq, nk), F32) for _ in range(n_heads)]
        for t in range(NUM_BUCKETS):
            hit = bkt == t
            for h in range(n_heads):
                tiles[h] = jnp.where(hit, tab_ref[t * n_table_cols + head0 + h], tiles[h])
        for h in range(n_heads):
            bias_ref[h] = jnp.where(in_band, tiles[h], MASK_VALUE)

    def lanes(slot):
        return slice(slot * HEAD_DIM, (slot + 1) * HEAD_DIM)

    def keys(slot):
        return jnp.concatenate([left_ref[0, 0, :, lanes(slot)], main_ref[0, 0, :, lanes(slot)],
                                right_ref[0, 0, :, lanes(slot)]], axis=0)

    qi = pl.program_id(2)
    key_pos = qi * tq - halo + lax.broadcasted_iota(jnp.int32, (1, nk), 1)
    key_ok = (key_pos >= 0) & (key_pos < seq_len)
    k = keys(2)
    v = keys(3)
    for g in range(2):
        head = pl.program_id(1) * 2 + g if with_sink else g
        logits = lax.dot_general(main_ref[0, 0, :, lanes(g)], k, (((1,), (1,)), ((), ())),
                                 preferred_element_type=F32)
        logits = jnp.where(key_ok, logits + bias_ref[head], MASK_VALUE)
        m = jnp.max(logits, axis=-1, keepdims=True)
        if with_sink:
            sink = sink_ref[head]
            m = jnp.maximum(m, sink)
        p = jnp.exp(logits - m)
        denom = jnp.sum(p, axis=-1, keepdims=True)
        if with_sink:
            denom = denom + jnp.exp(sink - m)
        out = jnp.dot(p.astype(BF16), v, preferred_element_type=F32) / denom
        if with_sink:
            o_ref[0, :, lanes(g)] = out.astype(o_ref.dtype)
        else:
            o_ref[0, 0, :, lanes(g)] = out
            lse_ref[0, 0, :, lanes(g)] = jnp.broadcast_to(m + jnp.log(denom), (tq, HEAD_DIM))


def _band_specs(tq, halo, n_halo_blocks):
    per = tq // halo
    return [
        pl.BlockSpec((1, 1, halo, PACKED), lambda b, c, i: (b, c, jnp.maximum(i * per - 1, 0), 0)),
        pl.BlockSpec((1, 1, tq, PACKED), lambda b, c, i: (b, c, i, 0)),
        pl.BlockSpec((1, 1, halo, PACKED),
                     lambda b, c, i: (b, c, jnp.minimum((i + 1) * per, n_halo_blocks - 1), 0)),
    ]


def _band_vmem(tq, halo, n_heads, out_bytes):
    nk = tq + 2 * halo
    return (n_heads * tq * nk * 4 + 2 * tq * nk * 4 + 2 * nk * PACKED * 2 + 2 * out_bytes
            + 8 * tq * nk * 4 + 4 * nk * HEAD_DIM * 2 + 2 * MIB)


def _attn_a(za, rel_bias_flat, sink, *, tq=256):
    b, n_kv, s, _ = za.shape
    halo = A_HALF_WINDOW
    assert s % tq == 0 and tq % halo == 0
    nk = tq + 2 * halo
    bkt = _t5_bucket(_band_rel(tq, halo)).astype(jnp.int32)
    smem = pl.BlockSpec(memory_space=pltpu.SMEM)
    return pl.pallas_call(
        functools.partial(_band_kernel, halo=halo, seq_len=s, n_heads=A_HEADS, head0=0,
                          n_table_cols=rel_bias_flat.shape[0] // NUM_BUCKETS, with_sink=True),
        grid=(b, n_kv, s // tq),
        in_specs=[pl.BlockSpec((tq, nk), lambda b_, c, i: (0, 0)), smem, smem]
        + _band_specs(tq, halo, s // halo),
        out_specs=pl.BlockSpec((1, tq, 2 * HEAD_DIM), lambda b_, c, i: (b_, i, c)),
        out_shape=jax.ShapeDtypeStruct((b, s, A_HEADS * HEAD_DIM), BF16),
        scratch_shapes=[pltpu.VMEM((A_HEADS, tq, nk), F32)],
        compiler_params=_cparams(3, _band_vmem(tq, halo, A_HEADS, tq * 256 * 2)),
        name="attn_a",
    )(bkt, rel_bias_flat, sink, za, za, za)


def _attn_c_pair(zc, rel_bias_flat, pair):
    b, r, sub, _ = zc.shape
    halo = C_HALF_WINDOW
    tq = min(256, sub)
    assert sub % tq == 0 and tq % halo == 0
    nk = tq + 2 * halo
    bkt = _t5_bucket(_band_rel(tq, halo) * r).astype(jnp.int32)
    smem = pl.BlockSpec(memory_space=pltpu.SMEM)
    out_spec = pl.BlockSpec((1, 1, tq, 2 * HEAD_DIM), lambda b_, c, i: (b_, c, i, 0))
    return pl.pallas_call(
        functools.partial(_band_kernel, halo=halo, seq_len=sub, n_heads=2, head0=A_HEADS + 2 * pair,
                          n_table_cols=rel_bias_flat.shape[0] // NUM_BUCKETS, with_sink=False),
        grid=(b, r, sub // tq),
        in_specs=[pl.BlockSpec((tq, nk), lambda b_, c, i: (0, 0)), smem] + _band_specs(tq, halo, sub // halo),
        out_specs=[out_spec, out_spec],
        out_shape=[jax.ShapeDtypeStruct((b, r, sub, 2 * HEAD_DIM), F32)] * 2,
        scratch_shapes=[pltpu.VMEM((2, tq, nk), F32)],
        compiler_params=_cparams(3, _band_vmem(tq, halo, 2, 2 * tq * 256 * 4)),
        name=f"attn_c{pair}",
    )(bkt, rel_bias_flat, zc, zc, zc)


def _c_merge_kernel(*refs):
    n = len(C_DILATIONS)
    o_refs, l_refs, y_ref = refs[:n], refs[n:2 * n], refs[2 * n]
    buf_ref = refs[2 * n + 1]
    tm = y_ref.shape[0]
    slots = iter(range(buf_ref.shape[0]))

    def token_order(ref, r, lanes):
        if r == 1:
            return ref[0, 0, :, lanes]
        slot = next(slots)
        for c in range(r):
            buf_ref[slot, pl.ds(c, tm // r, stride=r), :] = ref[0, c, :, lanes]
        return buf_ref[slot]

    for g in range(y_ref.shape[1] // HEAD_DIM):
        lanes = slice(g * HEAD_DIM, (g + 1) * HEAD_DIM)
        outs = [token_order(ref, r, lanes) for ref, r in zip(o_refs, C_DILATIONS)]
        lses = [token_order(ref, r, lanes) for ref, r in zip(l_refs, C_DILATIONS)]
        m = functools.reduce(jnp.maximum, lses)
        es = [jnp.exp(l - m) for l in lses]
        tot = functools.reduce(jnp.add, es)
        y = functools.reduce(jnp.add, [(e / tot) * o for e, o in zip(es, outs)])
        y_ref[:, lanes] = y.astype(y_ref.dtype)


def _attn_c(zcs, rel_bias_flat, *, tm=512):
    outs, lses = zip(*[_attn_c_pair(zc, rel_bias_flat, p) for p, zc in enumerate(zcs)])
    b, _, s, w = outs[0].shape
    m = b * s
    assert s % tm == 0
    pos_tiles = s // tm

    def spec(r):
        return pl.BlockSpec((1, r, tm // r, w), lambda i: (i // pos_tiles, 0, i % pos_tiles, 0))

    n_slots = 2 * (w // HEAD_DIM) * sum(r > 1 for r in C_DILATIONS)
    return pl.pallas_call(
        _c_merge_kernel,
        grid=(m // tm,),
        in_specs=[spec(r) for r in C_DILATIONS] * 2,
        out_specs=pl.BlockSpec((tm, w), lambda i: (i, 0)),
        out_shape=jax.ShapeDtypeStruct((m, w), BF16),
        scratch_shapes=[pltpu.VMEM((n_slots, tm, HEAD_DIM), F32)],
        compiler_params=_cparams(1, 2 * 6 * tm * w * 4 + (n_slots + 16) * tm * HEAD_DIM * 4 + 2 * MIB),
        name="attn_c_merge",
    )(*outs, *lses)


def _attn_b_kernel(q_ref, kv_ref, o_ref, *, tk):
    tq = q_ref.shape[2]
    s = kv_ref.shape[2]
    q = jnp.concatenate([q_ref[0, 0, :, g * HEAD_DIM:(g + 1) * HEAD_DIM] for g in range(3)], axis=0)
    m = l = acc = None
    for c in range(s // tk):
        kc = kv_ref[0, 0, c * tk:(c + 1) * tk, :HEAD_DIM]
        vc = kv_ref[0, 0, c * tk:(c + 1) * tk, HEAD_DIM:]
        logits = lax.dot_general(q, kc, (((1,), (1,)), ((), ())), preferred_element_type=F32)
        cmax = jnp.max(logits, axis=-1, keepdims=True)
        if c == 0:
            m = cmax
            p = jnp.exp(logits - m)
            l = jnp.sum(p, axis=-1, keepdims=True)
            acc = jnp.dot(p.astype(BF16), vc, preferred_element_type=F32)
        else:
            m_new = jnp.maximum(m, cmax)
            alpha = jnp.exp(m - m_new)
            p = jnp.exp(logits - m_new)
            l = alpha * l + jnp.sum(p, axis=-1, keepdims=True)
            acc = alpha * acc + jnp.dot(p.astype(BF16), vc, preferred_element_type=F32)
            m = m_new
    out = acc / l
    for g in range(3):
        o_ref[0, :, g * HEAD_DIM:(g + 1) * HEAD_DIM] = out[g * tq:(g + 1) * tq].astype(o_ref.dtype)


def _attn_b(zbq, zbkv, *, tq=256, tk=512):
    b, n_kv, s, _ = zbq.shape
    assert s % tq == 0 and s % tk == 0
    vmem = 2 * tq * 384 * 2 + 2 * s * 256 * 2 + 2 * tq * 384 * 2 \
        + 8 * 3 * tq * tk * 4 + 8 * 3 * tq * HEAD_DIM * 4 + 2 * MIB
    return pl.pallas_call(
        functools.partial(_attn_b_kernel, tk=tk),
        grid=(b, n_kv, s // tq),
        in_specs=[pl.BlockSpec((1, 1, tq, 3 * HEAD_DIM), lambda b_, c, i: (b_, c, i, 0)),
                  pl.BlockSpec((1, 1, s, 2 * HEAD_DIM), lambda b_, c, i: (b_, c, 0, 0))],
        out_specs=pl.BlockSpec((1, tq, 3 * HEAD_DIM), lambda b_, c, i: (b_, i, c)),
        out_shape=jax.ShapeDtypeStruct((b, s, n_kv * 3 * HEAD_DIM), BF16),
        compiler_params=_cparams(3, vmem),
        name="attn_b",
    )(zbq, zbkv)


def _merge_kernel(x_ref, h_ref, ya_ref, yb_ref, yc_ref, wga_ref, wgb_ref, wgc_ref,
                  bga_ref, bgb_ref, bgc_ref, wa_ref, wb_ref, wc_ref, wo_ref, o_ref, *, n_steps, n_chunk):
    j = pl.program_id(1)
    h = h_ref[...]

    def branch(y_ref, w_ref, wg_ref, bg_ref):
        gate = jax.nn.sigmoid(jnp.dot(h, wg_ref[...], preferred_element_type=F32) + bg_ref[...])
        return gate * jnp.dot(y_ref[...], w_ref[...], preferred_element_type=F32)

    @pl.when(j == 0)
    def _():
        o_ref[...] = jnp.zeros_like(o_ref)

    merged = (branch(ya_ref, wa_ref, wga_ref, bga_ref) + branch(yb_ref, wb_ref, wgb_ref, bgb_ref)
              + branch(yc_ref, wc_ref, wgc_ref, bgc_ref)).astype(BF16)
    for n in range(o_ref.shape[1] // n_chunk):
        cols = slice(n * n_chunk, (n + 1) * n_chunk)
        o_ref[:, cols] += jnp.dot(merged, wo_ref[:, cols], preferred_element_type=F32)

    @pl.when(j == n_steps - 1)
    def _():
        o_ref[...] = x_ref[...] + o_ref[...]


def _merge(x, h, ya, yb, yc, w_gate, b_gate, w_br_a, w_br_b, w_br_c, w_o, li, *, tm=512, tn=512):
    m, d = x.shape
    tn = min(tn, d)
    assert m % tm == 0 and d % tn == 0
    n_steps = d // tn
    ka, kb, kc = ya.shape[1], yb.shape[1], yc.shape[1]

    def row(width):
        return pl.BlockSpec((tm, width), lambda i, j: (i, 0))

    def gate_w(br):
        return pl.BlockSpec((None, d, tn), lambda i, j: (li, 0, br * n_steps + j))

    def gate_b(br):
        return pl.BlockSpec((None, 1, tn), lambda i, j: (li, 0, br * n_steps + j))

    def br_w(k):
        return pl.BlockSpec((None, k, tn), lambda i, j: (li, 0, j))

    bg = b_gate.reshape(b_gate.shape[0], 1, 3 * d)
    vmem = 2 * 2 * tm * d * 4 + 2 * tm * d * 2 + 2 * tm * (ka + kb + kc) * 2 \
        + 2 * (3 * d + ka + kb + kc + d) * tn * 2 + 8 * tm * tn * 4 + 2 * tm * tn * 4 + 2 * MIB
    return pl.pallas_call(
        functools.partial(_merge_kernel, n_steps=n_steps, n_chunk=tn),
        grid=(m // tm, n_steps),
        in_specs=[row(d), row(d), row(ka), row(kb), row(kc),
                  gate_w(0), gate_w(1), gate_w(2), gate_b(0), gate_b(1), gate_b(2),
                  br_w(ka), br_w(kb), br_w(kc),
                  pl.BlockSpec((None, tn, d), lambda i, j: (li, j, 0))],
        out_specs=row(d),
        out_shape=jax.ShapeDtypeStruct((m, d), F32),
        compiler_params=_cparams(2, vmem),
        name="merge",
    )(x, h, ya, yb, yc, w_gate, w_gate, w_gate, bg, bg, bg, w_br_a, w_br_b, w_br_c, w_o)


def _rope_tables(s):
    t = jnp.arange(s)
    row_ids = (t // GRID_W).astype(F32)
    col_ids = (t % GRID_W).astype(F32)
    axis_dim = HEAD_DIM // 2
    inv_freq = ROPE_THETA ** (-jnp.arange(0, axis_dim, 2, dtype=F32) / axis_dim)
    ang_r = row_ids[:, None] * inv_freq
    ang_c = col_ids[:, None] * inv_freq
    cos = jnp.concatenate([jnp.cos(ang_r), jnp.cos(ang_r), jnp.cos(ang_c), jnp.cos(ang_c)], axis=1)
    sin = jnp.concatenate([-jnp.sin(ang_r), jnp.sin(ang_r), -jnp.sin(ang_c), jnp.sin(ang_c)], axis=1)
    return cos, sin


def _trunk(x, p):
    b, s, d = x.shape
    depth = p["w_in"].shape[0]
    cos_tab, sin_tab = _rope_tables(s)
    rel_flat = p["rel_bias"].reshape(-1)
    xf = x.reshape(b * s, d)
    for li in range(depth):
        xf = _ffn(xf, p["ffn1_norm"][li], p["ffn1_w13"], p["ffn1_w2"], li)
        h, za, zbq, zbkv, *zcs = _mix_in(xf, p["mix_norm"][li], p["w_in"], li, p["q_gain_b"][li],
                                        p["k_gain_b"][li], cos_tab, sin_tab)
        ya = _attn_a(za, rel_flat, p["sink_a"][li]).reshape(b * s, -1)
        yb = _attn_b(zbq, zbkv).reshape(b * s, -1)
        yc = _attn_c(zcs, rel_flat)
        xf = _merge(xf, h, ya, yb, yc, p["w_gate"], p["b_gate"], p["w_br_a"], p["w_br_b"], p["w_br_c"],
                    p["w_o"], li)
        fin = p["final_norm"] if li == depth - 1 else None
        xf = _ffn(xf, p["ffn2_norm"][li], p["ffn2_w13"], p["ffn2_w2"], li, fin)
    return xf.reshape(b, s, d)


_MATMUL_WEIGHTS = ("ffn1_w13", "ffn1_w2", "w_in", "w_gate", "w_br_a", "w_br_b", "w_br_c", "w_o",
                   "ffn2_w13", "ffn2_w2")


def kernel(x_prompt, x_sample, ffn1_norm, ffn1_w13, ffn1_w2, mix_norm, w_in, q_gain_b, k_gain_b, sink_a,
           w_gate, b_gate, w_br_a, w_br_b, w_br_c, w_o, ffn2_norm, ffn2_w13, ffn2_w2, rel_bias, final_norm):
    p = dict(ffn1_norm=ffn1_norm, ffn1_w13=ffn1_w13, ffn1_w2=ffn1_w2, mix_norm=mix_norm, w_in=w_in,
             q_gain_b=q_gain_b, k_gain_b=k_gain_b, sink_a=sink_a, w_gate=w_gate, b_gate=b_gate,
             w_br_a=w_br_a, w_br_b=w_br_b, w_br_c=w_br_c, w_o=w_o, ffn2_norm=ffn2_norm,
             ffn2_w13=ffn2_w13, ffn2_w2=ffn2_w2, rel_bias=rel_bias, final_norm=final_norm)
    for name in _MATMUL_WEIGHTS:
        p[name] = p[name].astype(BF16)
    return _trunk(x_prompt, p), _trunk(x_sample, p)
```

```python
import functools
import math

import jax
import jax.numpy as jnp
from jax import lax
from jax.experimental import pallas as pl
from jax.experimental.pallas import tpu as pltpu

F32 = jnp.float32
BF16 = jnp.bfloat16

HEAD_DIM = 128
NORM_EPS = 1e-6
MASK_VALUE = -1e30
QK_SCALE = 1.0 / math.sqrt(HEAD_DIM)
LOG2E = math.log2(math.e)
FFN_RESIDUAL = 0.5
ROPE_THETA = 10000.0
GRID_W = 64
NUM_BUCKETS = 32
REL_MAX_DISTANCE = 2048

A_Q, A_K, A_V = 0, 4, 6
B_Q, B_K, B_V = 8, 14, 16
C_Q, C_K, C_V = 18, 24, 27
N_HEAD_BLOCKS = 30
A_HEADS = 4
A_HALF_WINDOW = 128
C_HALF_WINDOW = 64
C_DILATIONS = (1, 4, 16)
PACKED = 4 * HEAD_DIM
ROW_CHUNK = 256

V7X_VMEM_BYTES = 64 * 1024 * 1024
MIB = 1024 * 1024


def _cparams(n_grid, vmem_bytes):
    assert vmem_bytes < V7X_VMEM_BYTES, vmem_bytes
    return pltpu.CompilerParams(dimension_semantics=("arbitrary",) * n_grid,
                                vmem_limit_bytes=int(vmem_bytes))


def _rms(x, gain):
    ms = jnp.mean(x * x, axis=-1, keepdims=True)
    return x * lax.rsqrt(ms + NORM_EPS) * gain


def _lanes(slot):
    return slice(slot * HEAD_DIM, (slot + 1) * HEAD_DIM)


def _ffn_kernel(*refs, n_steps, n_chunk, final):
    if final:
        x_ref, g_ref, wg_ref, wu_ref, w2_ref, fin_ref, o_ref, xn_ref = refs
    else:
        x_ref, g_ref, wg_ref, wu_ref, w2_ref, o_ref, xn_ref = refs
    j = pl.program_id(1)
    row_chunks = [slice(r, r + ROW_CHUNK) for r in range(0, x_ref.shape[0], ROW_CHUNK)]

    @pl.when(j == 0)
    def _():
        for rows in row_chunks:
            xn_ref[rows, :] = _rms(x_ref[rows, :], g_ref[...]).astype(BF16)
        o_ref[...] = jnp.zeros_like(o_ref)

    xn = xn_ref[...]
    gate = jnp.dot(xn, wg_ref[...], preferred_element_type=F32)
    up = jnp.dot(xn, wu_ref[...], preferred_element_type=F32)
    act = (gate * jax.nn.sigmoid(gate) * up).astype(BF16)
    for n in range(o_ref.shape[1] // n_chunk):
        cols = slice(n * n_chunk, (n + 1) * n_chunk)
        o_ref[:, cols] += jnp.dot(act, w2_ref[:, cols], preferred_element_type=F32)

    @pl.when(j == n_steps - 1)
    def _():
        for rows in row_chunks:
            y = x_ref[rows, :] + FFN_RESIDUAL * o_ref[rows, :]
            if final:
                y = _rms(y, fin_ref[...])
            o_ref[rows, :] = y


def _ffn(x, gain, w13, w2, li, final_gain=None, *, tm=1024, tf=512, n_chunk=512):
    m, d = x.shape
    d_ff = w2.shape[1]
    n_chunk = min(n_chunk, d)
    assert m % tm == 0 and d_ff % tf == 0 and d % n_chunk == 0 and tm % ROW_CHUNK == 0
    n_steps = d_ff // tf
    final = final_gain is not None
    x_bufs = 1 if tm * d * 4 >= 8 * MIB else 2
    in_specs = [
        pl.BlockSpec((tm, d), lambda i, j: (i, 0), pipeline_mode=pl.Buffered(x_bufs)),
        pl.BlockSpec((1, d), lambda i, j: (0, 0)),
        pl.BlockSpec((None, d, tf), lambda i, j: (li, 0, j)),
        pl.BlockSpec((None, d, tf), lambda i, j: (li, 0, j + n_steps)),
        pl.BlockSpec((None, tf, d), lambda i, j: (li, j, 0)),
    ]
    args = [x, gain.reshape(1, d), w13, w13, w2]
    if final:
        in_specs.append(pl.BlockSpec((1, d), lambda i, j: (0, 0)))
        args.append(final_gain.reshape(1, d))
    vmem = x_bufs * tm * d * 4 + 2 * tm * d * 4 + tm * d * 2 + 2 * (3 * d * tf * 2) \
        + 3 * tm * tf * 4 + tm * tf * 2 + 2 * tm * n_chunk * 4 + 3 * ROW_CHUNK * d * 4 + 2 * MIB
    return pl.pallas_call(
        functools.partial(_ffn_kernel, n_steps=n_steps, n_chunk=n_chunk, final=final),
        grid=(m // tm, n_steps),
        in_specs=in_specs,
        out_specs=pl.BlockSpec((tm, d), lambda i, j: (i, 0)),
        out_shape=jax.ShapeDtypeStruct((m, d), F32),
        scratch_shapes=[pltpu.VMEM((tm, d), BF16)],
        compiler_params=_cparams(2, vmem),
        name=("ffnfin" if final else "ffn") + f"_{tm}_{tf}_{n_chunk}",
    )(*args)


def _swap_quarter_pairs(x):
    q = HEAD_DIM // 4
    lane = lax.broadcasted_iota(jnp.int32, x.shape, 1)
    from_right = pltpu.roll(x, HEAD_DIM - q, axis=1)
    from_left = pltpu.roll(x, q, axis=1)
    return jnp.where((lane % (2 * q)) < q, from_right, from_left)


def _mixin_kernel(x_ref, g_ref, w_ref, qg_ref, kg_ref, cos_ref, sin_ref,
                  h_ref, za_ref, zbq_ref, zbkv_ref, zc0_ref, zc1_ref, zc2_ref, cs_ref, *, b_q_scale, width):
    tm = x_ref.shape[0]
    h = _rms(x_ref[...], g_ref[...]).astype(BF16)
    h_ref[...] = h
    cos = cos_ref[...]
    sin = sin_ref[...]

    def norm_rope(v, gain):
        ms = jnp.mean(v * v, axis=-1, keepdims=True)
        y = v * lax.rsqrt(ms + NORM_EPS) * gain
        return y * cos + _swap_quarter_pairs(y) * sin

    per = width // HEAD_DIM
    for c in range(N_HEAD_BLOCKS // per):
        zc = jnp.dot(h, w_ref[:, c * width:(c + 1) * width], preferred_element_type=F32)
        for hh in range(per):
            blk = per * c + hh
            v = zc[:, _lanes(hh)]
            if blk < A_K:
                za_ref[0, blk // 2, :, _lanes(blk % 2)] = (v * QK_SCALE).astype(BF16)
            elif blk < A_V:
                za_ref[0, blk - A_K, :, _lanes(2)] = v.astype(BF16)
            elif blk < B_Q:
                za_ref[0, blk - A_V, :, _lanes(3)] = v.astype(BF16)
            elif blk < B_K:
                q = norm_rope(v, qg_ref[...]) * b_q_scale
                zbq_ref[0, (blk - B_Q) // 3, :, _lanes((blk - B_Q) % 3)] = q.astype(BF16)
            elif blk < B_V:
                zbkv_ref[0, blk - B_K, :, _lanes(0)] = norm_rope(v, kg_ref[...]).astype(BF16)
            elif blk < C_Q:
                zbkv_ref[0, blk - B_V, :, _lanes(1)] = v.astype(BF16)
            elif blk < C_K:
                cs_ref[blk - C_Q] = v * QK_SCALE
            else:
                cs_ref[blk - C_Q] = v
    for kv in range(zbkv_ref.shape[1]):
        zbkv_ref[0, kv, :, _lanes(2)] = jnp.ones((tm, HEAD_DIM), BF16)

    n_pairs = len(C_DILATIONS)
    for pair, (r, zc_ref) in enumerate(zip(C_DILATIONS, (zc0_ref, zc1_ref, zc2_ref))):
        staged = (2 * pair, 2 * pair + 1, 2 * n_pairs + pair, 3 * n_pairs + pair)
        for slot, src in enumerate(staged):
            for c in range(r):
                rows = pl.ds(c, tm // r, stride=r) if r > 1 else slice(None)
                zc_ref[0, c, :, _lanes(slot)] = cs_ref[src, rows, :].astype(BF16)


def _mix_in(x, gain, w_in, li, q_gain, k_gain, cos_tab, sin_tab, *, b_q_scale, tm=512, width=256):
    m, d = x.shape
    s = cos_tab.shape[0]
    b = m // s
    n_cols = w_in.shape[2]
    assert m % tm == 0 and s % tm == 0 and n_cols == N_HEAD_BLOCKS * HEAD_DIM and n_cols % width == 0
    assert all(tm % (16 * r) == 0 for r in C_DILATIONS)
    pos_tiles = s // tm

    def packed(n, width_, rows):
        return pl.BlockSpec((1, n, rows, width_), lambda i: (i // pos_tiles, 0, i % pos_tiles, 0))

    out_specs = [pl.BlockSpec((tm, d), lambda i: (i, 0)),
                 packed(2, PACKED, tm), packed(2, 3 * HEAD_DIM, tm), packed(2, 3 * HEAD_DIM, tm)]
    out_shape = [jax.ShapeDtypeStruct((m, d), BF16),
                 jax.ShapeDtypeStruct((b, 2, s, PACKED), BF16),
                 jax.ShapeDtypeStruct((b, 2, s, 3 * HEAD_DIM), BF16),
                 jax.ShapeDtypeStruct((b, 2, s, 3 * HEAD_DIM), BF16)]
    for r in C_DILATIONS:
        out_specs.append(packed(r, PACKED, tm // r))
        out_shape.append(jax.ShapeDtypeStruct((b, r, s // r, PACKED), BF16))
    c_cols = (N_HEAD_BLOCKS - C_Q) * HEAD_DIM
    vmem = 2 * tm * d * 4 + 2 * tm * d * 2 + 2 * tm * (n_cols + 2 * HEAD_DIM) * 2 + d * n_cols * 2 \
        + 4 * tm * HEAD_DIM * 4 + tm * c_cols * 4 + tm * d * 4 + 4 * tm * width * 4 \
        + 8 * tm * HEAD_DIM * 4 + 2 * MIB
    return pl.pallas_call(
        functools.partial(_mixin_kernel, b_q_scale=b_q_scale, width=width),
        grid=(m // tm,),
        in_specs=[
            pl.BlockSpec((tm, d), lambda i: (i, 0)),
            pl.BlockSpec((1, d), lambda i: (0, 0)),
            pl.BlockSpec((None, d, n_cols), lambda i: (li, 0, 0), pipeline_mode=pl.Buffered(1)),
            pl.BlockSpec((1, HEAD_DIM), lambda i: (0, 0)),
            pl.BlockSpec((1, HEAD_DIM), lambda i: (0, 0)),
            pl.BlockSpec((tm, HEAD_DIM), lambda i: (i % pos_tiles, 0)),
            pl.BlockSpec((tm, HEAD_DIM), lambda i: (i % pos_tiles, 0)),
        ],
        out_specs=out_specs,
        out_shape=out_shape,
        scratch_shapes=[pltpu.VMEM((N_HEAD_BLOCKS - C_Q, tm, HEAD_DIM), F32)],
        compiler_params=_cparams(1, vmem),
        name=f"mixin_{width}",
    )(x, gain.reshape(1, d), w_in, q_gain.reshape(1, HEAD_DIM), k_gain.reshape(1, HEAD_DIM),
      cos_tab, sin_tab)


def _t5_bucket(rel):
    half = NUM_BUCKETS // 2
    max_exact = half // 2
    ret = jnp.where(rel > 0, half, 0)
    n = jnp.abs(rel)
    nf = jnp.maximum(n, 1).astype(jnp.float32)
    large = max_exact + (jnp.log(nf / max_exact) / math.log(REL_MAX_DISTANCE / max_exact)
                         * (half - max_exact)).astype(jnp.int32)
    large = jnp.minimum(large, half - 1)
    return ret + jnp.where(n < max_exact, n, large)


def _band_rel(tq, halo):
    return jnp.arange(tq + 2 * halo)[None, :] - halo - jnp.arange(tq)[:, None]


def _band_kernel(*refs, halo, seq_len, n_heads, head0, n_table_cols, with_sink):
    if with_sink:
        bkt_ref, tab_ref, sink_ref, left_ref, main_ref, right_ref, o_ref, bias_ref = refs
        lse_ref = None
    else:
        bkt_ref, tab_ref, left_ref, main_ref, right_ref, o_ref, lse_ref, bias_ref = refs
        sink_ref = None
    n_sub, tq = main_ref.shape[1], main_ref.shape[2]
    nk = tq + 2 * halo
    first = (pl.program_id(0) == 0) & (pl.program_id(1) == 0) & (pl.program_id(2) == 0)

    @pl.when(first)
    def _():
        bkt = bkt_ref[...]
        rel = (lax.broadcasted_iota(jnp.int32, (tq, nk), 1) - halo
               - lax.broadcasted_iota(jnp.int32, (tq, nk), 0))
        in_band = jnp.abs(rel) <= halo
        tiles = [jnp.zeros((tq, nk), F32) for _ in range(n_heads)]
        for t in range(NUM_BUCKETS):
            hit = bkt == t
            for h in range(n_heads):
                tiles[h] = jnp.where(hit, tab_ref[t * n_table_cols + head0 + h], tiles[h])
        for h in range(n_heads):
            bias_ref[h] = jnp.where(in_band, tiles[h], MASK_VALUE)

    qi = pl.program_id(2)
    key_pos = qi * tq - halo + lax.broadcasted_iota(jnp.int32, (1, nk), 1)
    key_ok = (key_pos >= 0) & (key_pos < seq_len)
    for sub in range(n_sub):
        def keys(slot):
            return jnp.concatenate([left_ref[0, sub, :, _lanes(slot)], main_ref[0, sub, :, _lanes(slot)],
                                    right_ref[0, sub, :, _lanes(slot)]], axis=0)

        k = keys(2)
        v = keys(3)
        for g in range(2):
            head = (pl.program_id(1) * n_sub + sub) * 2 + g if with_sink else g
            logits = lax.dot_general(main_ref[0, sub, :, _lanes(g)], k, (((1,), (1,)), ((), ())),
                                     preferred_element_type=F32)
            logits = jnp.where(key_ok, logits + bias_ref[head], MASK_VALUE)
            m = jnp.max(logits, axis=-1, keepdims=True)
            if with_sink:
                sink = sink_ref[head]
                m = jnp.maximum(m, sink)
            p = jnp.exp(logits - m)
            denom = jnp.sum(p, axis=-1, keepdims=True)
            if with_sink:
                denom = denom + jnp.exp(sink - m)
            out = jnp.dot(p.astype(BF16), v, preferred_element_type=F32) / denom
            if with_sink:
                o_ref[0, :, _lanes(2 * sub + g)] = out.astype(o_ref.dtype)
            else:
                o_ref[0, sub, :, _lanes(g)] = out
                lse_ref[0, sub, :, _lanes(g)] = jnp.broadcast_to(m + jnp.log(denom), (tq, HEAD_DIM))


def _band_specs(tq, halo, n_sub, n_halo_blocks):
    per = tq // halo
    return [
        pl.BlockSpec((1, n_sub, halo, PACKED), lambda b, c, i: (b, c, jnp.maximum(i * per - 1, 0), 0)),
        pl.BlockSpec((1, n_sub, tq, PACKED), lambda b, c, i: (b, c, i, 0)),
        pl.BlockSpec((1, n_sub, halo, PACKED),
                     lambda b, c, i: (b, c, jnp.minimum((i + 1) * per, n_halo_blocks - 1), 0)),
    ]


def _band_vmem(tq, halo, n_sub, n_heads, out_bytes):
    nk = tq + 2 * halo
    return (n_heads * tq * nk * 4 + 2 * tq * nk * 4 + 2 * n_sub * nk * PACKED * 2 + 2 * out_bytes
            + 10 * tq * nk * 4 + 4 * nk * HEAD_DIM * 2 + 4 * MIB)


def _attn_a(za, rel_bias_flat, sink, *, tq=256, n_sub=1):
    b, n_kv, s, _ = za.shape
    halo = A_HALF_WINDOW
    assert s % tq == 0 and tq % halo == 0 and n_kv % n_sub == 0
    nk = tq + 2 * halo
    bkt = _t5_bucket(_band_rel(tq, halo)).astype(jnp.int32)
    smem = pl.BlockSpec(memory_space=pltpu.SMEM)
    out_w = n_sub * 2 * HEAD_DIM
    return pl.pallas_call(
        functools.partial(_band_kernel, halo=halo, seq_len=s, n_heads=A_HEADS, head0=0,
                          n_table_cols=rel_bias_flat.shape[0] // NUM_BUCKETS, with_sink=True),
        grid=(b, n_kv // n_sub, s // tq),
        in_specs=[pl.BlockSpec((tq, nk), lambda b_, c, i: (0, 0)), smem, smem]
        + _band_specs(tq, halo, n_sub, s // halo),
        out_specs=pl.BlockSpec((1, tq, out_w), lambda b_, c, i: (b_, i, c)),
        out_shape=jax.ShapeDtypeStruct((b, s, A_HEADS * HEAD_DIM), BF16),
        scratch_shapes=[pltpu.VMEM((A_HEADS, tq, nk), F32)],
        compiler_params=_cparams(3, _band_vmem(tq, halo, n_sub, A_HEADS, tq * out_w * 2)),
        name=f"attna_{tq}_{n_sub}",
    )(bkt, rel_bias_flat, sink, za, za, za)


def _attn_c_pair(zc, rel_bias_flat, pair, *, tq=256, n_sub=1):
    b, r, sub, _ = zc.shape
    halo = C_HALF_WINDOW
    tq = min(tq, sub)
    n_sub = min(n_sub, r)
    assert sub % tq == 0 and tq % halo == 0 and r % n_sub == 0
    nk = tq + 2 * halo
    bkt = _t5_bucket(_band_rel(tq, halo) * r).astype(jnp.int32)
    smem = pl.BlockSpec(memory_space=pltpu.SMEM)
    out_spec = pl.BlockSpec((1, n_sub, tq, 2 * HEAD_DIM), lambda b_, c, i: (b_, c, i, 0))
    return pl.pallas_call(
        functools.partial(_band_kernel, halo=halo, seq_len=sub, n_heads=2, head0=A_HEADS + 2 * pair,
                          n_table_cols=rel_bias_flat.shape[0] // NUM_BUCKETS, with_sink=False),
        grid=(b, r // n_sub, sub // tq),
        in_specs=[pl.BlockSpec((tq, nk), lambda b_, c, i: (0, 0)), smem]
        + _band_specs(tq, halo, n_sub, sub // halo),
        out_specs=[out_spec, out_spec],
        out_shape=[jax.ShapeDtypeStruct((b, r, sub, 2 * HEAD_DIM), F32)] * 2,
        scratch_shapes=[pltpu.VMEM((2, tq, nk), F32)],
        compiler_params=_cparams(3, _band_vmem(tq, halo, n_sub, 2, 2 * n_sub * tq * 256 * 4)),
        name=f"attnc{pair}_{tq}_{n_sub}",
    )(bkt, rel_bias_flat, zc, zc, zc)


def _c_merge_kernel(*refs):
    n = len(C_DILATIONS)
    o_refs, l_refs, y_ref = refs[:n], refs[n:2 * n], refs[2 * n]
    buf_ref = refs[2 * n + 1]
    tm = y_ref.shape[0]
    slots = iter(range(buf_ref.shape[0]))

    def token_order(ref, r, lanes):
        if r == 1:
            return ref[0, 0, :, lanes]
        slot = next(slots)
        for c in range(r):
            buf_ref[slot, pl.ds(c, tm // r, stride=r), :] = ref[0, c, :, lanes]
        return buf_ref[slot]

    for g in range(y_ref.shape[1] // HEAD_DIM):
        lanes = _lanes(g)
        outs = [token_order(ref, r, lanes) for ref, r in zip(o_refs, C_DILATIONS)]
        lses = [token_order(ref, r, lanes) for ref, r in zip(l_refs, C_DILATIONS)]
        m = functools.reduce(jnp.maximum, lses)
        es = [jnp.exp(l - m) for l in lses]
        tot = functools.reduce(jnp.add, es)
        y = functools.reduce(jnp.add, [(e / tot) * o for e, o in zip(es, outs)])
        y_ref[:, lanes] = y.astype(y_ref.dtype)


def _attn_c(zcs, rel_bias_flat, *, tm=512, tq=256, n_sub=1):
    outs, lses = zip(*[_attn_c_pair(zc, rel_bias_flat, p, tq=tq, n_sub=n_sub) for p, zc in enumerate(zcs)])
    b, _, s, w = outs[0].shape
    m = b * s
    assert s % tm == 0
    pos_tiles = s // tm

    def spec(r):
        return pl.BlockSpec((1, r, tm // r, w), lambda i: (i // pos_tiles, 0, i % pos_tiles, 0))

    n_slots = 2 * (w // HEAD_DIM) * sum(r > 1 for r in C_DILATIONS)
    return pl.pallas_call(
        _c_merge_kernel,
        grid=(m // tm,),
        in_specs=[spec(r) for r in C_DILATIONS] * 2,
        out_specs=pl.BlockSpec((tm, w), lambda i: (i, 0)),
        out_shape=jax.ShapeDtypeStruct((m, w), BF16),
        scratch_shapes=[pltpu.VMEM((n_slots, tm, HEAD_DIM), F32)],
        compiler_params=_cparams(1, 2 * 6 * tm * w * 4 + (n_slots + 16) * tm * HEAD_DIM * 4 + 2 * MIB),
        name="attnc_merge",
    )(*outs, *lses)


def _attn_b_kernel(q_ref, kv_ref, o_ref, *, tk, fused, split):
    tq = q_ref.shape[2]
    s = kv_ref.shape[2]
    if split:
        qs = [q_ref[0, 0, :, _lanes(g)] for g in range(3)]
    else:
        qs = [jnp.concatenate([q_ref[0, 0, :, _lanes(g)] for g in range(3)], axis=0)]
    expf = jnp.exp2 if fused else jnp.exp
    v_lanes = slice(HEAD_DIM, 3 * HEAD_DIM) if fused else _lanes(1)
    m = [None] * len(qs)
    l = [None] * len(qs)
    acc = [None] * len(qs)
    for c in range(s // tk):
        rows = slice(c * tk, (c + 1) * tk)
        kc = kv_ref[0, 0, rows, _lanes(0)]
        vc = kv_ref[0, 0, rows, v_lanes]
        for i, q in enumerate(qs):
            logits = lax.dot_general(q, kc, (((1,), (1,)), ((), ())), preferred_element_type=F32)
            cmax = jnp.max(logits, axis=-1, keepdims=True)
            m_new = cmax if c == 0 else jnp.maximum(m[i], cmax)
            p = expf(logits - m_new)
            pv = jnp.dot(p.astype(BF16), vc, preferred_element_type=F32)
            if c == 0:
                acc[i] = pv
                if not fused:
                    l[i] = jnp.sum(p, axis=-1, keepdims=True)
            else:
                alpha = expf(m[i] - m_new)
                acc[i] = alpha * acc[i] + pv
                if not fused:
                    l[i] = alpha * l[i] + jnp.sum(p, axis=-1, keepdims=True)
            m[i] = m_new
    outs = [a[:, _lanes(0)] / (a[:, HEAD_DIM:HEAD_DIM + 1] if fused else l_) for a, l_ in zip(acc, l)]
    for g in range(3):
        out = outs[g] if split else outs[0][g * tq:(g + 1) * tq]
        o_ref[0, :, _lanes(g)] = out.astype(o_ref.dtype)


def _attn_b(zbq, zbkv, *, tq=256, tk=512, fused=True, split=False):
    b, n_kv, s, _ = zbq.shape
    assert s % tq == 0 and s % tk == 0
    vmem = 2 * tq * 384 * 2 + 2 * s * 384 * 2 + 2 * tq * 384 * 2 \
        + 8 * 3 * tq * tk * 4 + 12 * 3 * tq * HEAD_DIM * 4 + 2 * MIB
    return pl.pallas_call(
        functools.partial(_attn_b_kernel, tk=tk, fused=fused, split=split),
        grid=(b, n_kv, s // tq),
        in_specs=[pl.BlockSpec((1, 1, tq, 3 * HEAD_DIM), lambda b_, c, i: (b_, c, i, 0)),
                  pl.BlockSpec((1, 1, s, 3 * HEAD_DIM), lambda b_, c, i: (b_, c, 0, 0))],
        out_specs=pl.BlockSpec((1, tq, 3 * HEAD_DIM), lambda b_, c, i: (b_, i, c)),
        out_shape=jax.ShapeDtypeStruct((b, s, n_kv * 3 * HEAD_DIM), BF16),
        compiler_params=_cparams(3, vmem),
        name=f"attnb_{tq}_{tk}_{int(fused)}{int(split)}",
    )(zbq, zbkv)


def _merge_kernel(x_ref, h_ref, ya_ref, yb_ref, yc_ref, wga_ref, wgb_ref, wgc_ref,
                  bga_ref, bgb_ref, bgc_ref, wa_ref, wb_ref, wc_ref, wo_ref, o_ref, *, n_steps, n_chunk):
    j = pl.program_id(1)
    h = h_ref[...]

    def branch(y_ref, w_ref, wg_ref, bg_ref):
        gate = jax.nn.sigmoid(jnp.dot(h, wg_ref[...], preferred_element_type=F32) + bg_ref[...])
        return gate * jnp.dot(y_ref[...], w_ref[...], preferred_element_type=F32)

    @pl.when(j == 0)
    def _():
        o_ref[...] = jnp.zeros_like(o_ref)

    merged = (branch(ya_ref, wa_ref, wga_ref, bga_ref) + branch(yb_ref, wb_ref, wgb_ref, bgb_ref)
              + branch(yc_ref, wc_ref, wgc_ref, bgc_ref)).astype(BF16)
    for n in range(o_ref.shape[1] // n_chunk):
        cols = slice(n * n_chunk, (n + 1) * n_chunk)
        o_ref[:, cols] += jnp.dot(merged, wo_ref[:, cols], preferred_element_type=F32)

    @pl.when(j == n_steps - 1)
    def _():
        for r in range(0, x_ref.shape[0], ROW_CHUNK):
            rows = slice(r, r + ROW_CHUNK)
            o_ref[rows, :] = x_ref[rows, :] + o_ref[rows, :]


def _merge(x, h, ya, yb, yc, w_gate, b_gate, w_br_a, w_br_b, w_br_c, w_o, li, *, tm=512, tn=512, n_chunk=512):
    m, d = x.shape
    tn = min(tn, d)
    n_chunk = min(n_chunk, d)
    assert m % tm == 0 and d % tn == 0 and d % n_chunk == 0 and tm % ROW_CHUNK == 0
    n_steps = d // tn
    ka, kb, kc = ya.shape[1], yb.shape[1], yc.shape[1]
    row_bufs = 1 if tm * d * 4 >= 8 * MIB else 2

    def row(width):
        return pl.BlockSpec((tm, width), lambda i, j: (i, 0), pipeline_mode=pl.Buffered(row_bufs))

    def gate_w(br):
        return pl.BlockSpec((None, d, tn), lambda i, j: (li, 0, br * n_steps + j))

    def gate_b(br):
        return pl.BlockSpec((None, 1, tn), lambda i, j: (li, 0, br * n_steps + j))

    def br_w(k):
        return pl.BlockSpec((None, k, tn), lambda i, j: (li, 0, j))

    bg = b_gate.reshape(b_gate.shape[0], 1, 3 * d)
    vmem = row_bufs * (tm * d * 4 + tm * d * 2 + tm * (ka + kb + kc) * 2) + 2 * tm * d * 4 \
        + 2 * (3 * d + ka + kb + kc + d) * tn * 2 + 7 * tm * tn * 4 + 2 * tm * n_chunk * 4 \
        + 3 * ROW_CHUNK * d * 4 + 2 * MIB
    return pl.pallas_call(
        functools.partial(_merge_kernel, n_steps=n_steps, n_chunk=n_chunk),
        grid=(m // tm, n_steps),
        in_specs=[row(d), row(d), row(ka), row(kb), row(kc),
                  gate_w(0), gate_w(1), gate_w(2), gate_b(0), gate_b(1), gate_b(2),
                  br_w(ka), br_w(kb), br_w(kc),
                  pl.BlockSpec((None, tn, d), lambda i, j: (li, j, 0))],
        out_specs=pl.BlockSpec((tm, d), lambda i, j: (i, 0)),
        out_shape=jax.ShapeDtypeStruct((m, d), F32),
        compiler_params=_cparams(2, vmem),
        name=f"merge_{tm}_{tn}_{n_chunk}",
    )(x, h, ya, yb, yc, w_gate, w_gate, w_gate, bg, bg, bg, w_br_a, w_br_b, w_br_c, w_o)


def _rope_tables(s):
    t = jnp.arange(s)
    row_ids = (t // GRID_W).astype(F32)
    col_ids = (t % GRID_W).astype(F32)
    axis_dim = HEAD_DIM // 2
    inv_freq = ROPE_THETA ** (-jnp.arange(0, axis_dim, 2, dtype=F32) / axis_dim)
    ang_r = row_ids[:, None] * inv_freq
    ang_c = col_ids[:, None] * inv_freq
    cos = jnp.concatenate([jnp.cos(ang_r), jnp.cos(ang_r), jnp.cos(ang_c), jnp.cos(ang_c)], axis=1)
    sin = jnp.concatenate([-jnp.sin(ang_r), jnp.sin(ang_r), -jnp.sin(ang_c), jnp.sin(ang_c)], axis=1)
    return cos, sin


_FFN1_CFG = {(0, 0): dict(tm=1024, tf=512, n_chunk=512), (0, 1): dict(tm=512, tf=512, n_chunk=2048),
             (1, 0): dict(tm=1024, tf=256, n_chunk=512), (1, 1): dict(tm=512, tf=512, n_chunk=256)}
_FFN2_CFG = {(0, 0): dict(tm=512, tf=512, n_chunk=512), (0, 1): dict(tm=1024, tf=512, n_chunk=512),
             (1, 0): dict(tm=1024, tf=512, n_chunk=256), (1, 1): dict(tm=512, tf=512, n_chunk=512)}
_MERGE_CFG = {(0, 0): dict(tm=512, tn=512, n_chunk=512), (0, 1): dict(tm=1024, tn=256, n_chunk=256),
              (1, 0): dict(tm=512, tn=256, n_chunk=512), (1, 1): dict(tm=512, tn=512, n_chunk=1024)}
_ATTN_A_CFG = {(0, 0): dict(tq=256, n_sub=1), (0, 1): dict(tq=256, n_sub=2),
               (1, 0): dict(tq=512, n_sub=2), (1, 1): dict(tq=128, n_sub=2)}
_ATTN_C_CFG = {(0, 0): dict(tq=256, n_sub=1), (0, 1): dict(tq=256, n_sub=4),
               (1, 0): dict(tq=256, n_sub=16), (1, 1): dict(tq=512, n_sub=4)}
_ATTN_B_CFG = {(0, 0): dict(tq=256, tk=512, fused=False), (0, 1): dict(tq=256, tk=256, fused=True),
               (1, 0): dict(tq=256, tk=256, fused=True), (1, 1): dict(tq=256, tk=512, fused=True)}
_MIXIN_WIDTH = {0: 256, 1: 768}


def _trunk(x, p, trunk):
    b, s, d = x.shape
    depth = p["w_in"].shape[0]
    cos_tab, sin_tab = _rope_tables(s)
    rel_flat = p["rel_bias"].reshape(-1)
    xf = x.reshape(b * s, d)
    for li in range(depth):
        site = (trunk, li)
        xf = _ffn(xf, p["ffn1_norm"][li], p["ffn1_w13"], p["ffn1_w2"], li, **_FFN1_CFG[site])
        b_cfg = _ATTN_B_CFG[site]
        h, za, zbq, zbkv, *zcs = _mix_in(
            xf, p["mix_norm"][li], p["w_in"], li, p["q_gain_b"][li], p["k_gain_b"][li], cos_tab, sin_tab,
            b_q_scale=QK_SCALE * (LOG2E if b_cfg["fused"] else 1.0), width=_MIXIN_WIDTH[li])
        ya = _attn_a(za, rel_flat, p["sink_a"][li], **_ATTN_A_CFG[site]).reshape(b * s, -1)
        yb = _attn_b(zbq, zbkv, **b_cfg).reshape(b * s, -1)
        yc = _attn_c(zcs, rel_flat, **_ATTN_C_CFG[site])
        xf = _merge(xf, h, ya, yb, yc, p["w_gate"], p["b_gate"], p["w_br_a"], p["w_br_b"], p["w_br_c"],
                    p["w_o"], li, **_MERGE_CFG[site])
        fin = p["final_norm"] if li == depth - 1 else None
        xf = _ffn(xf, p["ffn2_norm"][li], p["ffn2_w13"], p["ffn2_w2"], li, fin, **_FFN2_CFG[site])
    return xf.reshape(b, s, d)


_MATMUL_WEIGHTS = ("ffn1_w13", "ffn1_w2", "w_in", "w_gate", "w_br_a", "w_br_b", "w_br_c", "w_o",
                   "ffn2_w13", "ffn2_w2")


def kernel(x_prompt, x_sample, ffn1_norm, ffn1_w13, ffn1_w2, mix_norm, w_in, q_gain_b, k_gain_b, sink_a,
           w_gate, b_gate, w_br_a, w_br_b, w_br_c, w_o, ffn2_norm, ffn2_w13, ffn2_w2, rel_bias, final_norm):
    p = dict(ffn1_norm=ffn1_norm, ffn1_w13=ffn1_w13, ffn1_w2=ffn1_w2, mix_norm=mix_norm, w_in=w_in,
             q_gain_b=q_gain_b, k_gain_b=k_gain_b, sink_a=sink_a, w_gate=w_gate, b_gate=b_gate,
             w_br_a=w_br_a, w_br_b=w_br_b, w_br_c=w_br_c, w_o=w_o, ffn2_norm=ffn2_norm,
             ffn2_w13=ffn2_w13, ffn2_w2=ffn2_w2, rel_bias=rel_bias, final_norm=final_norm)
    for name in _MATMUL_WEIGHTS:
        p[name] = p[name].astype(BF16)
    return _trunk(x_prompt, p, 0), _trunk(x_sample, p, 1)
```

```python
import functools
import math

import jax
import jax.numpy as jnp
from jax import lax
from jax.experimental import pallas as pl
from jax.experimental.pallas import tpu as pltpu

F32 = jnp.float32
BF16 = jnp.bfloat16

HEAD_DIM = 128
NORM_EPS = 1e-6
MASK_VALUE = -1e30
QK_SCALE = 1.0 / math.sqrt(HEAD_DIM)
LOG2E = math.log2(math.e)
FFN_RESIDUAL = 0.5
ROPE_THETA = 10000.0
GRID_W = 64
NUM_BUCKETS = 32
REL_MAX_DISTANCE = 2048

A_Q, A_K, A_V = 0, 4, 6
B_Q, B_K, B_V = 8, 14, 16
C_Q, C_K, C_V = 18, 24, 27
N_HEAD_BLOCKS = 30
A_HEADS = 4
A_HALF_WINDOW = 128
C_HALF_WINDOW = 64
C_DILATIONS = (1, 4, 16)
PACKED = 4 * HEAD_DIM
BAND_SUBTILE = 128
ROW_CHUNK = 256

V7X_VMEM_BYTES = 64 * 1024 * 1024
MIB = 1024 * 1024


def _cparams(n_grid, vmem_bytes):
    assert vmem_bytes < V7X_VMEM_BYTES, vmem_bytes
    return pltpu.CompilerParams(dimension_semantics=("arbitrary",) * n_grid,
                                vmem_limit_bytes=int(vmem_bytes))


def _rms(x, gain):
    ms = jnp.mean(x * x, axis=-1, keepdims=True)
    return x * lax.rsqrt(ms + NORM_EPS) * gain


def _lanes(slot):
    return slice(slot * HEAD_DIM, (slot + 1) * HEAD_DIM)


def _ffn_kernel(*refs, n_steps, n_chunk, final):
    if final:
        x_ref, g_ref, wg_ref, wu_ref, w2_ref, fin_ref, o_ref, xn_ref = refs
    else:
        x_ref, g_ref, wg_ref, wu_ref, w2_ref, o_ref, xn_ref = refs
    j = pl.program_id(1)
    row_chunks = [slice(r, r + ROW_CHUNK) for r in range(0, x_ref.shape[0], ROW_CHUNK)]

    @pl.when(j == 0)
    def _():
        for rows in row_chunks:
            x = x_ref[rows, :]
            xn_ref[rows, :] = _rms(x, g_ref[...]).astype(BF16)
            o_ref[rows, :] = x

    xn = xn_ref[...]
    gate = jnp.dot(xn, wg_ref[...], preferred_element_type=F32)
    up = jnp.dot(xn, wu_ref[...], preferred_element_type=F32)
    act = (gate * jax.nn.sigmoid(gate) * up * FFN_RESIDUAL).astype(BF16)
    for n in range(o_ref.shape[1] // n_chunk):
        cols = slice(n * n_chunk, (n + 1) * n_chunk)
        o_ref[:, cols] += jnp.dot(act, w2_ref[:, cols], preferred_element_type=F32)

    if final:
        @pl.when(j == n_steps - 1)
        def _():
            for rows in row_chunks:
                o_ref[rows, :] = _rms(o_ref[rows, :], fin_ref[...])


def _ffn(x, gain, w13, w2, li, final_gain=None, *, tm=512, tf=512, n_chunk=512):
    m, d = x.shape
    d_ff = w2.shape[1]
    n_chunk = min(n_chunk, d)
    assert m % tm == 0 and d_ff % tf == 0 and d % n_chunk == 0 and tm % ROW_CHUNK == 0
    n_steps = d_ff // tf
    final = final_gain is not None
    in_specs = [
        pl.BlockSpec((tm, d), lambda i, j: (i, 0)),
        pl.BlockSpec((1, d), lambda i, j: (0, 0)),
        pl.BlockSpec((None, d, tf), lambda i, j: (li, 0, j)),
        pl.BlockSpec((None, d, tf), lambda i, j: (li, 0, j + n_steps)),
        pl.BlockSpec((None, tf, d), lambda i, j: (li, j, 0)),
    ]
    args = [x, gain.reshape(1, d), w13, w13, w2]
    if final:
        in_specs.append(pl.BlockSpec((1, d), lambda i, j: (0, 0)))
        args.append(final_gain.reshape(1, d))
    vmem = 2 * tm * d * 4 + 2 * tm * d * 4 + tm * d * 2 + 2 * (3 * d * tf * 2) \
        + 3 * tm * tf * 4 + tm * tf * 2 + 3 * tm * n_chunk * 4 + 3 * ROW_CHUNK * d * 4 + 2 * MIB
    return pl.pallas_call(
        functools.partial(_ffn_kernel, n_steps=n_steps, n_chunk=n_chunk, final=final),
        grid=(m // tm, n_steps),
        in_specs=in_specs,
        out_specs=pl.BlockSpec((tm, d), lambda i, j: (i, 0)),
        out_shape=jax.ShapeDtypeStruct((m, d), F32),
        scratch_shapes=[pltpu.VMEM((tm, d), BF16)],
        compiler_params=_cparams(2, vmem),
        name="ffn_final" if final else "ffn",
    )(*args)


def _swap_quarter_pairs(x):
    q = HEAD_DIM // 4
    lane = lax.broadcasted_iota(jnp.int32, x.shape, 1)
    from_right = pltpu.roll(x, HEAD_DIM - q, axis=1)
    from_left = pltpu.roll(x, q, axis=1)
    return jnp.where((lane % (2 * q)) < q, from_right, from_left)


def _mixin_kernel(x_ref, g_ref, w_ref, qg_ref, kg_ref, cos_ref, sin_ref,
                  h_ref, za_ref, zbq_ref, zbkv_ref, zc0_ref, zc1_ref, zc2_ref, cs_ref):
    tm = x_ref.shape[0]
    h = _rms(x_ref[...], g_ref[...]).astype(BF16)
    h_ref[...] = h
    cos = cos_ref[...]
    sin = sin_ref[...]

    def norm_rope(v, gain):
        ms = jnp.mean(v * v, axis=-1, keepdims=True)
        y = v * lax.rsqrt(ms + NORM_EPS) * gain
        return y * cos + _swap_quarter_pairs(y) * sin

    for c in range(N_HEAD_BLOCKS // 2):
        zc = jnp.dot(h, w_ref[:, c * 256:(c + 1) * 256], preferred_element_type=F32)
        for hh in range(2):
            blk = 2 * c + hh
            v = zc[:, _lanes(hh)]
            if blk < A_K:
                za_ref[0, blk // 2, :, _lanes(blk % 2)] = (v * QK_SCALE).astype(BF16)
            elif blk < A_V:
                za_ref[0, blk - A_K, :, _lanes(2)] = v.astype(BF16)
            elif blk < B_Q:
                za_ref[0, blk - A_V, :, _lanes(3)] = v.astype(BF16)
            elif blk < B_K:
                q = norm_rope(v, qg_ref[...]) * (QK_SCALE * LOG2E)
                zbq_ref[0, (blk - B_Q) // 3, :, _lanes((blk - B_Q) % 3)] = q.astype(BF16)
            elif blk < B_V:
                zbkv_ref[0, blk - B_K, :, _lanes(0)] = norm_rope(v, kg_ref[...]).astype(BF16)
            elif blk < C_Q:
                zbkv_ref[0, blk - B_V, :, _lanes(1)] = v.astype(BF16)
            elif blk < C_K:
                cs_ref[blk - C_Q] = v * QK_SCALE
            else:
                cs_ref[blk - C_Q] = v
    for kv in range(zbkv_ref.shape[1]):
        zbkv_ref[0, kv, :, _lanes(2)] = jnp.ones((tm, HEAD_DIM), BF16)

    n_pairs = len(C_DILATIONS)
    for pair, (r, zc_ref) in enumerate(zip(C_DILATIONS, (zc0_ref, zc1_ref, zc2_ref))):
        staged = (2 * pair, 2 * pair + 1, 2 * n_pairs + pair, 3 * n_pairs + pair)
        for slot, src in enumerate(staged):
            for c in range(r):
                rows = pl.ds(c, tm // r, stride=r) if r > 1 else slice(None)
                zc_ref[0, c, :, _lanes(slot)] = cs_ref[src, rows, :].astype(BF16)


def _mix_in(x, gain, w_in, li, q_gain, k_gain, cos_tab, sin_tab, *, tm=512):
    m, d = x.shape
    s = cos_tab.shape[0]
    b = m // s
    n_cols = w_in.shape[2]
    assert m % tm == 0 and s % tm == 0 and n_cols == N_HEAD_BLOCKS * HEAD_DIM
    assert all(tm % (16 * r) == 0 for r in C_DILATIONS)
    pos_tiles = s // tm

    def packed(n, width, rows):
        return pl.BlockSpec((1, n, rows, width), lambda i: (i // pos_tiles, 0, i % pos_tiles, 0))

    out_specs = [pl.BlockSpec((tm, d), lambda i: (i, 0)),
                 packed(2, PACKED, tm), packed(2, 3 * HEAD_DIM, tm), packed(2, 3 * HEAD_DIM, tm)]
    out_shape = [jax.ShapeDtypeStruct((m, d), BF16),
                 jax.ShapeDtypeStruct((b, 2, s, PACKED), BF16),
                 jax.ShapeDtypeStruct((b, 2, s, 3 * HEAD_DIM), BF16),
                 jax.ShapeDtypeStruct((b, 2, s, 3 * HEAD_DIM), BF16)]
    for r in C_DILATIONS:
        out_specs.append(packed(r, PACKED, tm // r))
        out_shape.append(jax.ShapeDtypeStruct((b, r, s // r, PACKED), BF16))
    c_cols = (N_HEAD_BLOCKS - C_Q) * HEAD_DIM
    vmem = 2 * tm * d * 4 + 2 * tm * d * 2 + 2 * tm * (n_cols + 2 * HEAD_DIM) * 2 + d * n_cols * 2 \
        + 4 * tm * HEAD_DIM * 4 + tm * c_cols * 4 + tm * d * 4 + 4 * tm * 256 * 4 \
        + 8 * tm * HEAD_DIM * 4 + 2 * MIB
    return pl.pallas_call(
        _mixin_kernel,
        grid=(m // tm,),
        in_specs=[
            pl.BlockSpec((tm, d), lambda i: (i, 0)),
            pl.BlockSpec((1, d), lambda i: (0, 0)),
            pl.BlockSpec((None, d, n_cols), lambda i: (li, 0, 0), pipeline_mode=pl.Buffered(1)),
            pl.BlockSpec((1, HEAD_DIM), lambda i: (0, 0)),
            pl.BlockSpec((1, HEAD_DIM), lambda i: (0, 0)),
            pl.BlockSpec((tm, HEAD_DIM), lambda i: (i % pos_tiles, 0)),
            pl.BlockSpec((tm, HEAD_DIM), lambda i: (i % pos_tiles, 0)),
        ],
        out_specs=out_specs,
        out_shape=out_shape,
        scratch_shapes=[pltpu.VMEM((N_HEAD_BLOCKS - C_Q, tm, HEAD_DIM), F32)],
        compiler_params=_cparams(1, vmem),
        name="mix_in",
    )(x, gain.reshape(1, d), w_in, q_gain.reshape(1, HEAD_DIM), k_gain.reshape(1, HEAD_DIM),
      cos_tab, sin_tab)


def _t5_bucket(rel):
    half = NUM_BUCKETS // 2
    max_exact = half // 2
    ret = jnp.where(rel > 0, half, 0)
    n = jnp.abs(rel)
    nf = jnp.maximum(n, 1).astype(jnp.float32)
    large = max_exact + (jnp.log(nf / max_exact) / math.log(REL_MAX_DISTANCE / max_exact)
                         * (half - max_exact)).astype(jnp.int32)
    large = jnp.minimum(large, half - 1)
    return ret + jnp.where(n < max_exact, n, large)


def _band_rel(ts, halo):
    return jnp.arange(ts + 2 * halo)[None, :] - halo - jnp.arange(ts)[:, None]


def _band_kernel(*refs, halo, ts, seq_len, n_heads, head0, n_table_cols, with_sink):
    if with_sink:
        bkt_ref, tab_ref, sink_ref, left_ref, main_ref, right_ref, o_ref, bias_ref = refs
        lse_ref = None
    else:
        bkt_ref, tab_ref, left_ref, main_ref, right_ref, o_ref, lse_ref, bias_ref = refs
        sink_ref = None
    n_sub, tq = main_ref.shape[1], main_ref.shape[2]
    nk = ts + 2 * halo
    first = (pl.program_id(0) == 0) & (pl.program_id(1) == 0) & (pl.program_id(2) == 0)

    @pl.when(first)
    def _():
        bkt = bkt_ref[...]
        rel = (lax.broadcasted_iota(jnp.int32, (ts, nk), 1) - halo
               - lax.broadcasted_iota(jnp.int32, (ts, nk), 0))
        in_band = jnp.abs(rel) <= halo
        tiles = [jnp.zeros((ts, nk), F32) for _ in range(n_heads)]
        for t in range(NUM_BUCKETS):
            hit = bkt == t
            for h in range(n_heads):
                tiles[h] = jnp.where(hit, tab_ref[t * n_table_cols + head0 + h], tiles[h])
        for h in range(n_heads):
            bias_ref[h] = jnp.where(in_band, tiles[h], MASK_VALUE)

    qi = pl.program_id(2)
    for sub in range(n_sub):
        def keys(slot):
            return jnp.concatenate([left_ref[0, sub, :, _lanes(slot)], main_ref[0, sub, :, _lanes(slot)],
                                    right_ref[0, sub, :, _lanes(slot)]], axis=0)

        k = keys(2)
        v = keys(3)
        for t in range(tq // ts):
            rows = slice(t * ts, (t + 1) * ts)
            kt = k[t * ts:t * ts + nk]
            vt = v[t * ts:t * ts + nk]
            key_pos = qi * tq + t * ts - halo + lax.broadcasted_iota(jnp.int32, (1, nk), 1)
            key_ok = (key_pos >= 0) & (key_pos < seq_len)
            for g in range(2):
                head = (pl.program_id(1) * n_sub + sub) * 2 + g if with_sink else g
                logits = lax.dot_general(main_ref[0, sub, rows, _lanes(g)], kt, (((1,), (1,)), ((), ())),
                                         preferred_element_type=F32)
                logits = jnp.where(key_ok, logits + bias_ref[head], MASK_VALUE)
                m = jnp.max(logits, axis=-1, keepdims=True)
                if with_sink:
                    sink = sink_ref[head]
                    m = jnp.maximum(m, sink)
                p = jnp.exp(logits - m)
                denom = jnp.sum(p, axis=-1, keepdims=True)
                if with_sink:
                    denom = denom + jnp.exp(sink - m)
                out = jnp.dot(p.astype(BF16), vt, preferred_element_type=F32) / denom
                if with_sink:
                    o_ref[0, rows, _lanes(2 * sub + g)] = out.astype(o_ref.dtype)
                else:
                    o_ref[0, sub, rows, _lanes(g)] = out
                    lse_ref[0, sub, rows, _lanes(g)] = jnp.broadcast_to(m + jnp.log(denom), (ts, HEAD_DIM))


def _band_specs(tq, halo, n_sub, n_halo_blocks):
    per = tq // halo
    return [
        pl.BlockSpec((1, n_sub, halo, PACKED), lambda b, c, i: (b, c, jnp.maximum(i * per - 1, 0), 0)),
        pl.BlockSpec((1, n_sub, tq, PACKED), lambda b, c, i: (b, c, i, 0)),
        pl.BlockSpec((1, n_sub, halo, PACKED),
                     lambda b, c, i: (b, c, jnp.minimum((i + 1) * per, n_halo_blocks - 1), 0)),
    ]


def _band_vmem(tq, ts, halo, n_sub, n_heads, out_bytes):
    nk = ts + 2 * halo
    return (n_heads * ts * nk * 4 + 2 * ts * nk * 4 + 2 * n_sub * (tq + 2 * halo) * PACKED * 2 + 2 * out_bytes
            + 10 * ts * nk * 4 + 4 * (tq + 2 * halo) * HEAD_DIM * 2 + 4 * MIB)


def _attn_a(za, rel_bias_flat, sink, *, tq=512, n_sub=2):
    b, n_kv, s, _ = za.shape
    halo = A_HALF_WINDOW
    ts = BAND_SUBTILE
    assert s % tq == 0 and tq % halo == 0 and n_kv % n_sub == 0 and tq % ts == 0
    nk = ts + 2 * halo
    bkt = _t5_bucket(_band_rel(ts, halo)).astype(jnp.int32)
    smem = pl.BlockSpec(memory_space=pltpu.SMEM)
    out_w = n_sub * 2 * HEAD_DIM
    return pl.pallas_call(
        functools.partial(_band_kernel, halo=halo, ts=ts, seq_len=s, n_heads=A_HEADS, head0=0,
                          n_table_cols=rel_bias_flat.shape[0] // NUM_BUCKETS, with_sink=True),
        grid=(b, n_kv // n_sub, s // tq),
        in_specs=[pl.BlockSpec((ts, nk), lambda b_, c, i: (0, 0)), smem, smem]
        + _band_specs(tq, halo, n_sub, s // halo),
        out_specs=pl.BlockSpec((1, tq, out_w), lambda b_, c, i: (b_, i, c)),
        out_shape=jax.ShapeDtypeStruct((b, s, A_HEADS * HEAD_DIM), BF16),
        scratch_shapes=[pltpu.VMEM((A_HEADS, ts, nk), F32)],
        compiler_params=_cparams(3, _band_vmem(tq, ts, halo, n_sub, A_HEADS, tq * out_w * 2)),
        name="attn_a",
    )(bkt, rel_bias_flat, sink, za, za, za)


def _attn_c_pair(zc, rel_bias_flat, pair, *, tq=256, n_sub=4):
    b, r, sub, _ = zc.shape
    halo = C_HALF_WINDOW
    tq = min(tq, sub)
    ts = min(BAND_SUBTILE, tq)
    n_sub = min(n_sub, r)
    assert sub % tq == 0 and tq % halo == 0 and r % n_sub == 0 and tq % ts == 0
    nk = ts + 2 * halo
    bkt = _t5_bucket(_band_rel(ts, halo) * r).astype(jnp.int32)
    smem = pl.BlockSpec(memory_space=pltpu.SMEM)
    out_spec = pl.BlockSpec((1, n_sub, tq, 2 * HEAD_DIM), lambda b_, c, i: (b_, c, i, 0))
    return pl.pallas_call(
        functools.partial(_band_kernel, halo=halo, ts=ts, seq_len=sub, n_heads=2, head0=A_HEADS + 2 * pair,
                          n_table_cols=rel_bias_flat.shape[0] // NUM_BUCKETS, with_sink=False),
        grid=(b, r // n_sub, sub // tq),
        in_specs=[pl.BlockSpec((ts, nk), lambda b_, c, i: (0, 0)), smem]
        + _band_specs(tq, halo, n_sub, sub // halo),
        out_specs=[out_spec, out_spec],
        out_shape=[jax.ShapeDtypeStruct((b, r, sub, 2 * HEAD_DIM), F32)] * 2,
        scratch_shapes=[pltpu.VMEM((2, ts, nk), F32)],
        compiler_params=_cparams(3, _band_vmem(tq, ts, halo, n_sub, 2, 2 * n_sub * tq * 256 * 4)),
        name=f"attn_c{pair}",
    )(bkt, rel_bias_flat, zc, zc, zc)


def _c_merge_kernel(*refs):
    n = len(C_DILATIONS)
    o_refs, l_refs, y_ref = refs[:n], refs[n:2 * n], refs[2 * n]
    buf_ref = refs[2 * n + 1]
    tm = y_ref.shape[0]
    slots = iter(range(buf_ref.shape[0]))

    def token_order(ref, r, lanes):
        if r == 1:
            return ref[0, 0, :, lanes]
        slot = next(slots)
        for c in range(r):
            buf_ref[slot, pl.ds(c, tm // r, stride=r), :] = ref[0, c, :, lanes]
        return buf_ref[slot]

    for g in range(y_ref.shape[1] // HEAD_DIM):
        lanes = _lanes(g)
        outs = [token_order(ref, r, lanes) for ref, r in zip(o_refs, C_DILATIONS)]
        lses = [token_order(ref, r, lanes) for ref, r in zip(l_refs, C_DILATIONS)]
        m = functools.reduce(jnp.maximum, lses)
        es = [jnp.exp(l - m) for l in lses]
        tot = functools.reduce(jnp.add, es)
        y = functools.reduce(jnp.add, [(e / tot) * o for e, o in zip(es, outs)])
        y_ref[:, lanes] = y.astype(y_ref.dtype)


def _attn_c(zcs, rel_bias_flat, *, tm=512):
    outs, lses = zip(*[_attn_c_pair(zc, rel_bias_flat, p) for p, zc in enumerate(zcs)])
    b, _, s, w = outs[0].shape
    m = b * s
    assert s % tm == 0
    pos_tiles = s // tm

    def spec(r):
        return pl.BlockSpec((1, r, tm // r, w), lambda i: (i // pos_tiles, 0, i % pos_tiles, 0))

    n_slots = 2 * (w // HEAD_DIM) * sum(r > 1 for r in C_DILATIONS)
    return pl.pallas_call(
        _c_merge_kernel,
        grid=(m // tm,),
        in_specs=[spec(r) for r in C_DILATIONS] * 2,
        out_specs=pl.BlockSpec((tm, w), lambda i: (i, 0)),
        out_shape=jax.ShapeDtypeStruct((m, w), BF16),
        scratch_shapes=[pltpu.VMEM((n_slots, tm, HEAD_DIM), F32)],
        compiler_params=_cparams(1, 2 * 6 * tm * w * 4 + (n_slots + 16) * tm * HEAD_DIM * 4 + 2 * MIB),
        name="attn_c_merge",
    )(*outs, *lses)


def _attn_b_kernel(q_ref, kv_ref, o_ref, *, tk):
    tq = q_ref.shape[2]
    s = kv_ref.shape[2]
    q = jnp.concatenate([q_ref[0, 0, :, _lanes(g)] for g in range(3)], axis=0)
    m = acc = None
    for c in range(s // tk):
        rows = slice(c * tk, (c + 1) * tk)
        logits = lax.dot_general(q, kv_ref[0, 0, rows, _lanes(0)], (((1,), (1,)), ((), ())),
                                 preferred_element_type=F32)
        cmax = jnp.max(logits, axis=-1, keepdims=True)
        m_new = cmax if c == 0 else jnp.maximum(m, cmax)
        p = jnp.exp2(logits - m_new)
        pv = jnp.dot(p.astype(BF16), kv_ref[0, 0, rows, HEAD_DIM:], preferred_element_type=F32)
        acc = pv if c == 0 else jnp.exp2(m - m_new) * acc + pv
        m = m_new
    out = acc[:, _lanes(0)] / acc[:, HEAD_DIM:HEAD_DIM + 1]
    for g in range(3):
        o_ref[0, :, _lanes(g)] = out[g * tq:(g + 1) * tq].astype(o_ref.dtype)


def _attn_b(zbq, zbkv, *, tq=256, tk=256):
    b, n_kv, s, _ = zbq.shape
    assert s % tq == 0 and s % tk == 0
    vmem = 2 * tq * 384 * 2 + 2 * s * 384 * 2 + 2 * tq * 384 * 2 \
        + 8 * 3 * tq * tk * 4 + 12 * 3 * tq * HEAD_DIM * 4 + 2 * MIB
    return pl.pallas_call(
        functools.partial(_attn_b_kernel, tk=tk),
        grid=(b, n_kv, s // tq),
        in_specs=[pl.BlockSpec((1, 1, tq, 3 * HEAD_DIM), lambda b_, c, i: (b_, c, i, 0)),
                  pl.BlockSpec((1, 1, s, 3 * HEAD_DIM), lambda b_, c, i: (b_, c, 0, 0))],
        out_specs=pl.BlockSpec((1, tq, 3 * HEAD_DIM), lambda b_, c, i: (b_, i, c)),
        out_shape=jax.ShapeDtypeStruct((b, s, n_kv * 3 * HEAD_DIM), BF16),
        compiler_params=_cparams(3, vmem),
        name="attn_b",
    )(zbq, zbkv)


def _merge_kernel(x_ref, h_ref, ya_ref, yb_ref, yc_ref, wga_ref, wgb_ref, wgc_ref,
                  bga_ref, bgb_ref, bgc_ref, wa_ref, wb_ref, wc_ref, wo_ref, o_ref, *, n_chunk):
    h = h_ref[...]

    def branch(y_ref, w_ref, wg_ref, bg_ref):
        gate = jax.nn.sigmoid(jnp.dot(h, wg_ref[...], preferred_element_type=F32) + bg_ref[...])
        return gate * jnp.dot(y_ref[...], w_ref[...], preferred_element_type=F32)

    @pl.when(pl.program_id(1) == 0)
    def _():
        o_ref[...] = x_ref[...]

    merged = (branch(ya_ref, wa_ref, wga_ref, bga_ref) + branch(yb_ref, wb_ref, wgb_ref, bgb_ref)
              + branch(yc_ref, wc_ref, wgc_ref, bgc_ref)).astype(BF16)
    for n in range(o_ref.shape[1] // n_chunk):
        cols = slice(n * n_chunk, (n + 1) * n_chunk)
        o_ref[:, cols] += jnp.dot(merged, wo_ref[:, cols], preferred_element_type=F32)


def _merge(x, h, ya, yb, yc, w_gate, b_gate, w_br_a, w_br_b, w_br_c, w_o, li, *, tm=512, tn=512, n_chunk=512):
    m, d = x.shape
    tn = min(tn, d)
    n_chunk = min(n_chunk, d)
    assert m % tm == 0 and d % tn == 0 and d % n_chunk == 0
    n_steps = d // tn
    ka, kb, kc = ya.shape[1], yb.shape[1], yc.shape[1]

    def row(width):
        return pl.BlockSpec((tm, width), lambda i, j: (i, 0))

    def gate_w(br):
        return pl.BlockSpec((None, d, tn), lambda i, j: (li, 0, br * n_steps + j))

    def gate_b(br):
        return pl.BlockSpec((None, 1, tn), lambda i, j: (li, 0, br * n_steps + j))

    def br_w(k):
        return pl.BlockSpec((None, k, tn), lambda i, j: (li, 0, j))

    bg = b_gate.reshape(b_gate.shape[0], 1, 3 * d)
    vmem = 2 * (tm * d * 4 + tm * d * 2 + tm * (ka + kb + kc) * 2) + 2 * tm * d * 4 \
        + 2 * (3 * d + ka + kb + kc + d) * tn * 2 + 7 * tm * tn * 4 + 3 * tm * n_chunk * 4 + 2 * MIB
    return pl.pallas_call(
        functools.partial(_merge_kernel, n_chunk=n_chunk),
        grid=(m // tm, n_steps),
        in_specs=[row(d), row(d), row(ka), row(kb), row(kc),
                  gate_w(0), gate_w(1), gate_w(2), gate_b(0), gate_b(1), gate_b(2),
                  br_w(ka), br_w(kb), br_w(kc),
                  pl.BlockSpec((None, tn, d), lambda i, j: (li, j, 0))],
        out_specs=pl.BlockSpec((tm, d), lambda i, j: (i, 0)),
        out_shape=jax.ShapeDtypeStruct((m, d), F32),
        compiler_params=_cparams(2, vmem),
        name="merge",
    )(x, h, ya, yb, yc, w_gate, w_gate, w_gate, bg, bg, bg, w_br_a, w_br_b, w_br_c, w_o)


def _rope_tables(s):
    t = jnp.arange(s)
    row_ids = (t // GRID_W).astype(F32)
    col_ids = (t % GRID_W).astype(F32)
    axis_dim = HEAD_DIM // 2
    inv_freq = ROPE_THETA ** (-jnp.arange(0, axis_dim, 2, dtype=F32) / axis_dim)
    ang_r = row_ids[:, None] * inv_freq
    ang_c = col_ids[:, None] * inv_freq
    cos = jnp.concatenate([jnp.cos(ang_r), jnp.cos(ang_r), jnp.cos(ang_c), jnp.cos(ang_c)], axis=1)
    sin = jnp.concatenate([-jnp.sin(ang_r), jnp.sin(ang_r), -jnp.sin(ang_c), jnp.sin(ang_c)], axis=1)
    return cos, sin


def _trunk(x, p):
    b, s, d = x.shape
    depth = p["w_in"].shape[0]
    cos_tab, sin_tab = _rope_tables(s)
    rel_flat = p["rel_bias"].reshape(-1)
    xf = x.reshape(b * s, d)
    for li in range(depth):
        xf = _ffn(xf, p["ffn1_norm"][li], p["ffn1_w13"], p["ffn1_w2"], li)
        h, za, zbq, zbkv, *zcs = _mix_in(xf, p["mix_norm"][li], p["w_in"], li, p["q_gain_b"][li],
                                        p["k_gain_b"][li], cos_tab, sin_tab)
        ya = _attn_a(za, rel_flat, p["sink_a"][li]).reshape(b * s, -1)
        yb = _attn_b(zbq, zbkv).reshape(b * s, -1)
        yc = _attn_c(zcs, rel_flat)
        xf = _merge(xf, h, ya, yb, yc, p["w_gate"], p["b_gate"], p["w_br_a"], p["w_br_b"], p["w_br_c"],
                    p["w_o"], li)
        fin = p["final_norm"] if li == depth - 1 else None
        xf = _ffn(xf, p["ffn2_norm"][li], p["ffn2_w13"], p["ffn2_w2"], li, fin)
    return xf.reshape(b, s, d)


_MATMUL_WEIGHTS = ("ffn1_w13", "ffn1_w2", "w_in", "w_gate", "w_br_a", "w_br_b", "w_br_c", "w_o",
                   "ffn2_w13", "ffn2_w2")


def kernel(x_prompt, x_sample, ffn1_norm, ffn1_w13, ffn1_w2, mix_norm, w_in, q_gain_b, k_gain_b, sink_a,
           w_gate, b_gate, w_br_a, w_br_b, w_br_c, w_o, ffn2_norm, ffn2_w13, ffn2_w2, rel_bias, final_norm):
    p = dict(ffn1_norm=ffn1_norm, ffn1_w13=ffn1_w13, ffn1_w2=ffn1_w2, mix_norm=mix_norm, w_in=w_in,
             q_gain_b=q_gain_b, k_gain_b=k_gain_b, sink_a=sink_a, w_gate=w_gate, b_gate=b_gate,
             w_br_a=w_br_a, w_br_b=w_br_b, w_br_c=w_br_c, w_o=w_o, ffn2_norm=ffn2_norm,
             ffn2_w13=ffn2_w13, ffn2_w2=ffn2_w2, rel_bias=rel_bias, final_norm=final_norm)
    for name in _MATMUL_WEIGHTS:
        p[name] = p[name].astype(BF16)
    return _trunk(x_prompt, p), _trunk(x_sample, p)
```

```python
import functools
import math

import jax
import jax.numpy as jnp
from jax import lax
from jax.experimental import pallas as pl
from jax.experimental.pallas import tpu as pltpu

F32 = jnp.float32
BF16 = jnp.bfloat16

HEAD_DIM = 128
NORM_EPS = 1e-6
MASK_VALUE = -1e30
QK_SCALE = 1.0 / math.sqrt(HEAD_DIM)
LOG2E = math.log2(math.e)
FFN_RESIDUAL = 0.5
ROPE_THETA = 10000.0
GRID_W = 64
NUM_BUCKETS = 32
REL_MAX_DISTANCE = 2048

A_Q, A_K, A_V = 0, 4, 6
B_Q, B_K, B_V = 8, 14, 16
C_Q, C_K, C_V = 18, 24, 27
N_HEAD_BLOCKS = 30
A_HEADS = 4
A_HALF_WINDOW = 128
C_HALF_WINDOW = 64
C_DILATIONS = (1, 4, 16)
PACKED = 4 * HEAD_DIM
A_SUBTILE = 256
C_SUBTILE = 128
C_ROWS_PER_STEP = 2048
ROW_CHUNK = 256

V7X_VMEM_BYTES = 64 * 1024 * 1024
MIB = 1024 * 1024


def _cparams(n_grid, vmem_bytes):
    assert vmem_bytes < V7X_VMEM_BYTES, vmem_bytes
    return pltpu.CompilerParams(dimension_semantics=("arbitrary",) * n_grid,
                                vmem_limit_bytes=V7X_VMEM_BYTES)


def _rms(x, gain):
    ms = jnp.mean(x * x, axis=-1, keepdims=True)
    return x * lax.rsqrt(ms + NORM_EPS) * gain


def _lanes(slot):
    return slice(slot * HEAD_DIM, (slot + 1) * HEAD_DIM)


def _ffn_kernel(*refs, n_steps, n_chunk, final):
    if final:
        x_ref, g_ref, wg_ref, wu_ref, w2_ref, fin_ref, o_ref, xn_ref = refs
    else:
        x_ref, g_ref, wg_ref, wu_ref, w2_ref, o_ref, xn_ref = refs
    j = pl.program_id(1)
    row_chunks = [slice(r, r + ROW_CHUNK) for r in range(0, x_ref.shape[0], ROW_CHUNK)]

    @pl.when(j == 0)
    def _():
        for rows in row_chunks:
            x = x_ref[rows, :]
            xn_ref[rows, :] = _rms(x, g_ref[...]).astype(BF16)
            o_ref[rows, :] = x

    xn = xn_ref[...]
    gate = jnp.dot(xn, wg_ref[...], preferred_element_type=F32)
    up = jnp.dot(xn, wu_ref[...], preferred_element_type=F32)
    act = (gate * jax.nn.sigmoid(gate) * up * FFN_RESIDUAL).astype(BF16)
    for n in range(o_ref.shape[1] // n_chunk):
        cols = slice(n * n_chunk, (n + 1) * n_chunk)
        o_ref[:, cols] += jnp.dot(act, w2_ref[:, cols], preferred_element_type=F32)

    if final:
        @pl.when(j == n_steps - 1)
        def _():
            for rows in row_chunks:
                o_ref[rows, :] = _rms(o_ref[rows, :], fin_ref[...])


def _ffn(x, gain, w13, w2, li, final_gain=None, *, tm=512, tf=512, n_chunk=512):
    m, d = x.shape
    d_ff = w2.shape[1]
    n_chunk = min(n_chunk, d)
    assert m % tm == 0 and d_ff % tf == 0 and d % n_chunk == 0 and tm % ROW_CHUNK == 0
    n_steps = d_ff // tf
    final = final_gain is not None
    in_specs = [
        pl.BlockSpec((tm, d), lambda i, j: (i, 0)),
        pl.BlockSpec((1, d), lambda i, j: (0, 0)),
        pl.BlockSpec((None, d, tf), lambda i, j: (li, 0, j)),
        pl.BlockSpec((None, d, tf), lambda i, j: (li, 0, j + n_steps)),
        pl.BlockSpec((None, tf, d), lambda i, j: (li, j, 0)),
    ]
    args = [x, gain.reshape(1, d), w13, w13, w2]
    if final:
        in_specs.append(pl.BlockSpec((1, d), lambda i, j: (0, 0)))
        args.append(final_gain.reshape(1, d))
    vmem = 2 * tm * d * 4 + 2 * tm * d * 4 + tm * d * 2 + 2 * (3 * d * tf * 2) \
        + 3 * tm * tf * 4 + tm * tf * 2 + 3 * tm * n_chunk * 4 + 3 * ROW_CHUNK * d * 4 + 2 * MIB
    return pl.pallas_call(
        functools.partial(_ffn_kernel, n_steps=n_steps, n_chunk=n_chunk, final=final),
        grid=(m // tm, n_steps),
        in_specs=in_specs,
        out_specs=pl.BlockSpec((tm, d), lambda i, j: (i, 0)),
        out_shape=jax.ShapeDtypeStruct((m, d), F32),
        scratch_shapes=[pltpu.VMEM((tm, d), BF16)],
        compiler_params=_cparams(2, vmem),
        name="ffn_final" if final else "ffn",
    )(*args)


def _swap_quarter_pairs(x):
    q = HEAD_DIM // 4
    lane = lax.broadcasted_iota(jnp.int32, x.shape, 1)
    from_right = pltpu.roll(x, HEAD_DIM - q, axis=1)
    from_left = pltpu.roll(x, q, axis=1)
    return jnp.where((lane % (2 * q)) < q, from_right, from_left)


def _mixin_kernel(x_ref, g_ref, w_ref, qg_ref, kg_ref, cos_ref, sin_ref,
                  h_ref, za_ref, zbq_ref, zbkv_ref, zc0_ref, zc1_ref, zc2_ref, cs_ref):
    tm = x_ref.shape[0]
    h = _rms(x_ref[...], g_ref[...]).astype(BF16)
    h_ref[...] = h
    cos = cos_ref[...]
    sin = sin_ref[...]

    def norm_rope(v, gain):
        ms = jnp.mean(v * v, axis=-1, keepdims=True)
        y = v * lax.rsqrt(ms + NORM_EPS) * gain
        return y * cos + _swap_quarter_pairs(y) * sin

    for c in range(N_HEAD_BLOCKS // 2):
        zc = jnp.dot(h, w_ref[:, c * 256:(c + 1) * 256], preferred_element_type=F32)
        for hh in range(2):
            blk = 2 * c + hh
            v = zc[:, _lanes(hh)]
            if blk < A_K:
                za_ref[0, blk // 2, :, _lanes(blk % 2)] = (v * QK_SCALE).astype(BF16)
            elif blk < A_V:
                za_ref[0, blk - A_K, :, _lanes(2)] = v.astype(BF16)
            elif blk < B_Q:
                za_ref[0, blk - A_V, :, _lanes(3)] = v.astype(BF16)
            elif blk < B_K:
                q = norm_rope(v, qg_ref[...]) * (QK_SCALE * LOG2E)
                zbq_ref[0, (blk - B_Q) // 3, :, _lanes((blk - B_Q) % 3)] = q.astype(BF16)
            elif blk < B_V:
                zbkv_ref[0, blk - B_K, :, _lanes(0)] = norm_rope(v, kg_ref[...]).astype(BF16)
            elif blk < C_Q:
                zbkv_ref[0, blk - B_V, :, _lanes(1)] = v.astype(BF16)
            elif blk < C_K:
                cs_ref[blk - C_Q] = v * QK_SCALE
            else:
                cs_ref[blk - C_Q] = v
    for kv in range(zbkv_ref.shape[1]):
        zbkv_ref[0, kv, :, _lanes(2)] = jnp.ones((tm, HEAD_DIM), BF16)

    n_pairs = len(C_DILATIONS)
    for pair, (r, zc_ref) in enumerate(zip(C_DILATIONS, (zc0_ref, zc1_ref, zc2_ref))):
        staged = (2 * pair, 2 * pair + 1, 2 * n_pairs + pair, 3 * n_pairs + pair)
        for slot, src in enumerate(staged):
            for c in range(r):
                rows = pl.ds(c, tm // r, stride=r) if r > 1 else slice(None)
                zc_ref[0, c, :, _lanes(slot)] = cs_ref[src, rows, :].astype(BF16)


def _mix_in(x, gain, w_in, li, q_gain, k_gain, cos_tab, sin_tab, *, tm=512):
    m, d = x.shape
    s = cos_tab.shape[0]
    b = m // s
    n_cols = w_in.shape[2]
    assert m % tm == 0 and s % tm == 0 and n_cols == N_HEAD_BLOCKS * HEAD_DIM
    assert all(tm % (16 * r) == 0 for r in C_DILATIONS)
    pos_tiles = s // tm

    def packed(n, width, rows):
        return pl.BlockSpec((1, n, rows, width), lambda i: (i // pos_tiles, 0, i % pos_tiles, 0))

    out_specs = [pl.BlockSpec((tm, d), lambda i: (i, 0)),
                 packed(2, PACKED, tm), packed(2, 3 * HEAD_DIM, tm), packed(2, 3 * HEAD_DIM, tm)]
    out_shape = [jax.ShapeDtypeStruct((m, d), BF16),
                 jax.ShapeDtypeStruct((b, 2, s, PACKED), BF16),
                 jax.ShapeDtypeStruct((b, 2, s, 3 * HEAD_DIM), BF16),
                 jax.ShapeDtypeStruct((b, 2, s, 3 * HEAD_DIM), BF16)]
    for r in C_DILATIONS:
        out_specs.append(packed(r, PACKED, tm // r))
        out_shape.append(jax.ShapeDtypeStruct((b, r, s // r, PACKED), BF16))
    c_cols = (N_HEAD_BLOCKS - C_Q) * HEAD_DIM
    vmem = 2 * tm * d * 4 + 2 * tm * d * 2 + 2 * tm * (n_cols + 2 * HEAD_DIM) * 2 + d * n_cols * 2 \
        + 4 * tm * HEAD_DIM * 4 + tm * c_cols * 4 + tm * d * 4 + 4 * tm * 256 * 4 \
        + 8 * tm * HEAD_DIM * 4 + 2 * MIB
    return pl.pallas_call(
        _mixin_kernel,
        grid=(m // tm,),
        in_specs=[
            pl.BlockSpec((tm, d), lambda i: (i, 0)),
            pl.BlockSpec((1, d), lambda i: (0, 0)),
            pl.BlockSpec((None, d, n_cols), lambda i: (li, 0, 0), pipeline_mode=pl.Buffered(1)),
            pl.BlockSpec((1, HEAD_DIM), lambda i: (0, 0)),
            pl.BlockSpec((1, HEAD_DIM), lambda i: (0, 0)),
            pl.BlockSpec((tm, HEAD_DIM), lambda i: (i % pos_tiles, 0)),
            pl.BlockSpec((tm, HEAD_DIM), lambda i: (i % pos_tiles, 0)),
        ],
        out_specs=out_specs,
        out_shape=out_shape,
        scratch_shapes=[pltpu.VMEM((N_HEAD_BLOCKS - C_Q, tm, HEAD_DIM), F32)],
        compiler_params=_cparams(1, vmem),
        name="mix_in",
    )(x, gain.reshape(1, d), w_in, q_gain.reshape(1, HEAD_DIM), k_gain.reshape(1, HEAD_DIM),
      cos_tab, sin_tab)


def _t5_bucket(rel):
    half = NUM_BUCKETS // 2
    max_exact = half // 2
    ret = jnp.where(rel > 0, half, 0)
    n = jnp.abs(rel)
    nf = jnp.maximum(n, 1).astype(jnp.float32)
    large = max_exact + (jnp.log(nf / max_exact) / math.log(REL_MAX_DISTANCE / max_exact)
                         * (half - max_exact)).astype(jnp.int32)
    large = jnp.minimum(large, half - 1)
    return ret + jnp.where(n < max_exact, n, large)


def _band_rel(ts, halo):
    return jnp.arange(ts + 2 * halo)[None, :] - halo - jnp.arange(ts)[:, None]


def _band_kernel(*refs, halo, ts, seq_len, n_heads, head0, n_table_cols, with_sink):
    if with_sink:
        bkt_ref, tab_ref, sink_ref, left_ref, main_ref, right_ref, o_ref, bias_ref = refs
        lse_ref = None
    else:
        bkt_ref, tab_ref, left_ref, main_ref, right_ref, o_ref, lse_ref, bias_ref = refs
        sink_ref = None
    n_sub, tq = main_ref.shape[1], main_ref.shape[2]
    nk = ts + 2 * halo
    first = (pl.program_id(0) == 0) & (pl.program_id(1) == 0) & (pl.program_id(2) == 0)

    @pl.when(first)
    def _():
        bkt = bkt_ref[...]
        rel = (lax.broadcasted_iota(jnp.int32, (ts, nk), 1) - halo
               - lax.broadcasted_iota(jnp.int32, (ts, nk), 0))
        in_band = jnp.abs(rel) <= halo
        tiles = [jnp.zeros((ts, nk), F32) for _ in range(n_heads)]
        for t in range(NUM_BUCKETS):
            hit = bkt == t
            for h in range(n_heads):
                tiles[h] = jnp.where(hit, tab_ref[t * n_table_cols + head0 + h], tiles[h])
        for h in range(n_heads):
            bias_ref[h] = jnp.where(in_band, tiles[h], MASK_VALUE)

    qi = pl.program_id(2)
    for sub in range(n_sub):
        def keys(slot):
            return jnp.concatenate([left_ref[0, sub, :, _lanes(slot)], main_ref[0, sub, :, _lanes(slot)],
                                    right_ref[0, sub, :, _lanes(slot)]], axis=0)

        k = keys(2)
        v = keys(3)
        for t in range(tq // ts):
            rows = slice(t * ts, (t + 1) * ts)
            kt = k[t * ts:t * ts + nk]
            vt = v[t * ts:t * ts + nk]
            key_pos = qi * tq + t * ts - halo + lax.broadcasted_iota(jnp.int32, (1, nk), 1)
            key_ok = (key_pos >= 0) & (key_pos < seq_len)
            for g in range(2):
                head = (pl.program_id(1) * n_sub + sub) * 2 + g if with_sink else g
                logits = lax.dot_general(main_ref[0, sub, rows, _lanes(g)], kt, (((1,), (1,)), ((), ())),
                                         preferred_element_type=F32)
                logits = jnp.where(key_ok, logits + bias_ref[head], MASK_VALUE)
                m = jnp.max(logits, axis=-1, keepdims=True)
                if with_sink:
                    sink = sink_ref[head]
                    m = jnp.maximum(m, sink)
                p = jnp.exp(logits - m)
                denom = jnp.sum(p, axis=-1, keepdims=True)
                if with_sink:
                    denom = denom + jnp.exp(sink - m)
                out = jnp.dot(p.astype(BF16), vt, preferred_element_type=F32) / denom
                if with_sink:
                    o_ref[0, rows, _lanes(2 * sub + g)] = out.astype(o_ref.dtype)
                else:
                    o_ref[0, sub, rows, _lanes(g)] = out
                    lse_ref[0, sub, rows, _lanes(g)] = jnp.broadcast_to(m + jnp.log(denom), (ts, HEAD_DIM))


def _band_specs(tq, halo, n_sub, n_halo_blocks):
    per = tq // halo
    return [
        pl.BlockSpec((1, n_sub, halo, PACKED), lambda b, c, i: (b, c, jnp.maximum(i * per - 1, 0), 0)),
        pl.BlockSpec((1, n_sub, tq, PACKED), lambda b, c, i: (b, c, i, 0)),
        pl.BlockSpec((1, n_sub, halo, PACKED),
                     lambda b, c, i: (b, c, jnp.minimum((i + 1) * per, n_halo_blocks - 1), 0)),
    ]


def _band_vmem(tq, ts, halo, n_sub, n_heads, out_bytes):
    nk = ts + 2 * halo
    return (n_heads * ts * nk * 4 + 2 * ts * nk * 4 + 2 * n_sub * (tq + 2 * halo) * PACKED * 2 + 2 * out_bytes
            + 10 * ts * nk * 4 + 4 * (tq + 2 * halo) * HEAD_DIM * 2 + 4 * MIB)


def _attn_a(za, rel_bias_flat, sink, *, tq=512, n_sub=2):
    b, n_kv, s, _ = za.shape
    halo = A_HALF_WINDOW
    ts = A_SUBTILE
    assert s % tq == 0 and tq % halo == 0 and n_kv % n_sub == 0 and tq % ts == 0
    nk = ts + 2 * halo
    bkt = _t5_bucket(_band_rel(ts, halo)).astype(jnp.int32)
    smem = pl.BlockSpec(memory_space=pltpu.SMEM)
    out_w = n_sub * 2 * HEAD_DIM
    return pl.pallas_call(
        functools.partial(_band_kernel, halo=halo, ts=ts, seq_len=s, n_heads=A_HEADS, head0=0,
                          n_table_cols=rel_bias_flat.shape[0] // NUM_BUCKETS, with_sink=True),
        grid=(b, n_kv // n_sub, s // tq),
        in_specs=[pl.BlockSpec((ts, nk), lambda b_, c, i: (0, 0)), smem, smem]
        + _band_specs(tq, halo, n_sub, s // halo),
        out_specs=pl.BlockSpec((1, tq, out_w), lambda b_, c, i: (b_, i, c)),
        out_shape=jax.ShapeDtypeStruct((b, s, A_HEADS * HEAD_DIM), BF16),
        scratch_shapes=[pltpu.VMEM((A_HEADS, ts, nk), F32)],
        compiler_params=_cparams(3, _band_vmem(tq, ts, halo, n_sub, A_HEADS, tq * out_w * 2)),
        name="attn_a",
    )(bkt, rel_bias_flat, sink, za, za, za)


def _attn_c_pair(zc, rel_bias_flat, pair):
    b, r, sub, _ = zc.shape
    halo = C_HALF_WINDOW
    n_sub = min(r, C_ROWS_PER_STEP // C_SUBTILE)
    tq = min(sub, C_ROWS_PER_STEP // n_sub)
    ts = min(C_SUBTILE, tq)
    assert sub % tq == 0 and tq % halo == 0 and r % n_sub == 0 and tq % ts == 0
    nk = ts + 2 * halo
    bkt = _t5_bucket(_band_rel(ts, halo) * r).astype(jnp.int32)
    smem = pl.BlockSpec(memory_space=pltpu.SMEM)
    out_spec = pl.BlockSpec((1, n_sub, tq, 2 * HEAD_DIM), lambda b_, c, i: (b_, c, i, 0))
    return pl.pallas_call(
        functools.partial(_band_kernel, halo=halo, ts=ts, seq_len=sub, n_heads=2, head0=A_HEADS + 2 * pair,
                          n_table_cols=rel_bias_flat.shape[0] // NUM_BUCKETS, with_sink=False),
        grid=(b, r // n_sub, sub // tq),
        in_specs=[pl.BlockSpec((ts, nk), lambda b_, c, i: (0, 0)), smem]
        + _band_specs(tq, halo, n_sub, sub // halo),
        out_specs=[out_spec, out_spec],
        out_shape=[jax.ShapeDtypeStruct((b, r, sub, 2 * HEAD_DIM), F32)] * 2,
        scratch_shapes=[pltpu.VMEM((2, ts, nk), F32)],
        compiler_params=_cparams(3, _band_vmem(tq, ts, halo, n_sub, 2, 2 * n_sub * tq * 256 * 4)),
        name=f"attn_c{pair}",
    )(bkt, rel_bias_flat, zc, zc, zc)


def _c_merge_kernel(*refs):
    n = len(C_DILATIONS)
    o_refs, l_refs, y_ref = refs[:n], refs[n:2 * n], refs[2 * n]
    buf_ref = refs[2 * n + 1]
    tm = y_ref.shape[0]
    slots = iter(range(buf_ref.shape[0]))

    def token_order(ref, r, lanes):
        if r == 1:
            return ref[0, 0, :, lanes]
        slot = next(slots)
        for c in range(r):
            buf_ref[slot, pl.ds(c, tm // r, stride=r), :] = ref[0, c, :, lanes]
        return buf_ref[slot]

    for g in range(y_ref.shape[1] // HEAD_DIM):
        lanes = _lanes(g)
        outs = [token_order(ref, r, lanes) for ref, r in zip(o_refs, C_DILATIONS)]
        lses = [token_order(ref, r, lanes) for ref, r in zip(l_refs, C_DILATIONS)]
        m = functools.reduce(jnp.maximum, lses)
        es = [jnp.exp(l - m) for l in lses]
        tot = functools.reduce(jnp.add, es)
        y = functools.reduce(jnp.add, [(e / tot) * o for e, o in zip(es, outs)])
        y_ref[:, lanes] = y.astype(y_ref.dtype)


def _attn_c(zcs, rel_bias_flat, *, tm=512):
    outs, lses = zip(*[_attn_c_pair(zc, rel_bias_flat, p) for p, zc in enumerate(zcs)])
    b, _, s, w = outs[0].shape
    m = b * s
    assert s % tm == 0
    pos_tiles = s // tm

    def spec(r):
        return pl.BlockSpec((1, r, tm // r, w), lambda i: (i // pos_tiles, 0, i % pos_tiles, 0))

    n_slots = 2 * (w // HEAD_DIM) * sum(r > 1 for r in C_DILATIONS)
    return pl.pallas_call(
        _c_merge_kernel,
        grid=(m // tm,),
        in_specs=[spec(r) for r in C_DILATIONS] * 2,
        out_specs=pl.BlockSpec((tm, w), lambda i: (i, 0)),
        out_shape=jax.ShapeDtypeStruct((m, w), BF16),
        scratch_shapes=[pltpu.VMEM((n_slots, tm, HEAD_DIM), F32)],
        compiler_params=_cparams(1, 2 * 6 * tm * w * 4 + (n_slots + 16) * tm * HEAD_DIM * 4 + 2 * MIB),
        name="attn_c_merge",
    )(*outs, *lses)


def _attn_b_kernel(q_ref, kv_ref, o_ref, *, tk):
    tq = q_ref.shape[2]
    s = kv_ref.shape[2]
    q = jnp.concatenate([q_ref[0, 0, :, _lanes(g)] for g in range(3)], axis=0)
    m = acc = None
    for c in range(s // tk):
        rows = slice(c * tk, (c + 1) * tk)
        logits = lax.dot_general(q, kv_ref[0, 0, rows, _lanes(0)], (((1,), (1,)), ((), ())),
                                 preferred_element_type=F32)
        cmax = jnp.max(logits, axis=-1, keepdims=True)
        m_new = cmax if c == 0 else jnp.maximum(m, cmax)
        p = jnp.exp2(logits - m_new)
        pv = jnp.dot(p.astype(BF16), kv_ref[0, 0, rows, HEAD_DIM:], preferred_element_type=F32)
        acc = pv if c == 0 else jnp.exp2(m - m_new) * acc + pv
        m = m_new
    out = acc[:, _lanes(0)] / acc[:, HEAD_DIM:HEAD_DIM + 1]
    for g in range(3):
        o_ref[0, :, _lanes(g)] = out[g * tq:(g + 1) * tq].astype(o_ref.dtype)


def _attn_b(zbq, zbkv, *, tq=256, tk=256):
    b, n_kv, s, _ = zbq.shape
    assert s % tq == 0 and s % tk == 0
    vmem = 2 * tq * 384 * 2 + 2 * s * 384 * 2 + 2 * tq * 384 * 2 \
        + 8 * 3 * tq * tk * 4 + 12 * 3 * tq * HEAD_DIM * 4 + 2 * MIB
    return pl.pallas_call(
        functools.partial(_attn_b_kernel, tk=tk),
        grid=(b, n_kv, s // tq),
        in_specs=[pl.BlockSpec((1, 1, tq, 3 * HEAD_DIM), lambda b_, c, i: (b_, c, i, 0)),
                  pl.BlockSpec((1, 1, s, 3 * HEAD_DIM), lambda b_, c, i: (b_, c, 0, 0))],
        out_specs=pl.BlockSpec((1, tq, 3 * HEAD_DIM), lambda b_, c, i: (b_, i, c)),
        out_shape=jax.ShapeDtypeStruct((b, s, n_kv * 3 * HEAD_DIM), BF16),
        compiler_params=_cparams(3, vmem),
        name="attn_b",
    )(zbq, zbkv)


def _merge_kernel(x_ref, h_ref, ya_ref, yb_ref, yc_ref, wga_ref, wgb_ref, wgc_ref,
                  bga_ref, bgb_ref, bgc_ref, wa_ref, wb_ref, wc_ref, wo_ref, o_ref, *, n_chunk):
    h = h_ref[...]

    def branch(y_ref, w_ref, wg_ref, bg_ref):
        gate = jax.nn.sigmoid(jnp.dot(h, wg_ref[...], preferred_element_type=F32) + bg_ref[...])
        return gate * jnp.dot(y_ref[...], w_ref[...], preferred_element_type=F32)

    @pl.when(pl.program_id(1) == 0)
    def _():
        o_ref[...] = x_ref[...]

    merged = (branch(ya_ref, wa_ref, wga_ref, bga_ref) + branch(yb_ref, wb_ref, wgb_ref, bgb_ref)
              + branch(yc_ref, wc_ref, wgc_ref, bgc_ref)).astype(BF16)
    for n in range(o_ref.shape[1] // n_chunk):
        cols = slice(n * n_chunk, (n + 1) * n_chunk)
        o_ref[:, cols] += jnp.dot(merged, wo_ref[:, cols], preferred_element_type=F32)


def _merge(x, h, ya, yb, yc, w_gate, b_gate, w_br_a, w_br_b, w_br_c, w_o, li, *, tm=512, tn=512, n_chunk=512):
    m, d = x.shape
    tn = min(tn, d)
    n_chunk = min(n_chunk, d)
    assert m % tm == 0 and d % tn == 0 and d % n_chunk == 0
    n_steps = d // tn
    ka, kb, kc = ya.shape[1], yb.shape[1], yc.shape[1]

    def row(width):
        return pl.BlockSpec((tm, width), lambda i, j: (i, 0))

    def gate_w(br):
        return pl.BlockSpec((None, d, tn), lambda i, j: (li, 0, br * n_steps + j))

    def gate_b(br):
        return pl.BlockSpec((None, 1, tn), lambda i, j: (li, 0, br * n_steps + j))

    def br_w(k):
        return pl.BlockSpec((None, k, tn), lambda i, j: (li, 0, j))

    bg = b_gate.reshape(b_gate.shape[0], 1, 3 * d)
    vmem = 2 * (tm * d * 4 + tm * d * 2 + tm * (ka + kb + kc) * 2) + 2 * tm * d * 4 \
        + 2 * (3 * d + ka + kb + kc + d) * tn * 2 + 7 * tm * tn * 4 + 3 * tm * n_chunk * 4 + 2 * MIB
    return pl.pallas_call(
        functools.partial(_merge_kernel, n_chunk=n_chunk),
        grid=(m // tm, n_steps),
        in_specs=[row(d), row(d), row(ka), row(kb), row(kc),
                  gate_w(0), gate_w(1), gate_w(2), gate_b(0), gate_b(1), gate_b(2),
                  br_w(ka), br_w(kb), br_w(kc),
                  pl.BlockSpec((None, tn, d), lambda i, j: (li, j, 0))],
        out_specs=pl.BlockSpec((tm, d), lambda i, j: (i, 0)),
        out_shape=jax.ShapeDtypeStruct((m, d), F32),
        compiler_params=_cparams(2, vmem),
        name="merge",
    )(x, h, ya, yb, yc, w_gate, w_gate, w_gate, bg, bg, bg, w_br_a, w_br_b, w_br_c, w_o)


def _rope_tables(s):
    t = jnp.arange(s)
    row_ids = (t // GRID_W).astype(F32)
    col_ids = (t % GRID_W).astype(F32)
    axis_dim = HEAD_DIM // 2
    inv_freq = ROPE_THETA ** (-jnp.arange(0, axis_dim, 2, dtype=F32) / axis_dim)
    ang_r = row_ids[:, None] * inv_freq
    ang_c = col_ids[:, None] * inv_freq
    cos = jnp.concatenate([jnp.cos(ang_r), jnp.cos(ang_r), jnp.cos(ang_c), jnp.cos(ang_c)], axis=1)
    sin = jnp.concatenate([-jnp.sin(ang_r), jnp.sin(ang_r), -jnp.sin(ang_c), jnp.sin(ang_c)], axis=1)
    return cos, sin


def _trunk(x, p):
    b, s, d = x.shape
    depth = p["w_in"].shape[0]
    cos_tab, sin_tab = _rope_tables(s)
    rel_flat = p["rel_bias"].reshape(-1)
    xf = x.reshape(b * s, d)
    for li in range(depth):
        xf = _ffn(xf, p["ffn1_norm"][li], p["ffn1_w13"], p["ffn1_w2"], li)
        h, za, zbq, zbkv, *zcs = _mix_in(xf, p["mix_norm"][li], p["w_in"], li, p["q_gain_b"][li],
                                        p["k_gain_b"][li], cos_tab, sin_tab)
        ya = _attn_a(za, rel_flat, p["sink_a"][li]).reshape(b * s, -1)
        yb = _attn_b(zbq, zbkv).reshape(b * s, -1)
        yc = _attn_c(zcs, rel_flat)
        xf = _merge(xf, h, ya, yb, yc, p["w_gate"], p["b_gate"], p["w_br_a"], p["w_br_b"], p["w_br_c"],
                    p["w_o"], li)
        fin = p["final_norm"] if li == depth - 1 else None
        xf = _ffn(xf, p["ffn2_norm"][li], p["ffn2_w13"], p["ffn2_w2"], li, fin)
    return xf.reshape(b, s, d)


_MATMUL_WEIGHTS = ("ffn1_w13", "ffn1_w2", "w_in", "w_gate", "w_br_a", "w_br_b", "w_br_c", "w_o",
                   "ffn2_w13", "ffn2_w2")


def kernel(x_prompt, x_sample, ffn1_norm, ffn1_w13, ffn1_w2, mix_norm, w_in, q_gain_b, k_gain_b, sink_a,
           w_gate, b_gate, w_br_a, w_br_b, w_br_c, w_o, ffn2_norm, ffn2_w13, ffn2_w2, rel_bias, final_norm):
    p = dict(ffn1_norm=ffn1_norm, ffn1_w13=ffn1_w13, ffn1_w2=ffn1_w2, mix_norm=mix_norm, w_in=w_in,
             q_gain_b=q_gain_b, k_gain_b=k_gain_b, sink_a=sink_a, w_gate=w_gate, b_gate=b_gate,
             w_br_a=w_br_a, w_br_b=w_br_b, w_br_c=w_br_c, w_o=w_o, ffn2_norm=ffn2_norm,
             ffn2_w13=ffn2_w13, ffn2_w2=ffn2_w2, rel_bias=rel_bias, final_norm=final_norm)
    for name in _MATMUL_WEIGHTS:
        p[name] = p[name].astype(BF16)
    return _trunk(x_prompt, p), _trunk(x_sample, p)
```

```python
import functools
import math

import jax
import jax.numpy as jnp
from jax import lax
from jax.experimental import pallas as pl
from jax.experimental.pallas import tpu as pltpu

F32 = jnp.float32
BF16 = jnp.bfloat16

HEAD_DIM = 128
NORM_EPS = 1e-6
MASK_VALUE = -1e30
QK_SCALE = 1.0 / math.sqrt(HEAD_DIM)
LOG2E = math.log2(math.e)
FFN_RESIDUAL = 0.5
ROPE_THETA = 10000.0
GRID_W = 64
NUM_BUCKETS = 32
REL_MAX_DISTANCE = 2048

A_Q, A_K, A_V = 0, 4, 6
B_Q, B_K, B_V = 8, 14, 16
C_Q, C_K, C_V = 18, 24, 27
N_HEAD_BLOCKS = 30
A_HEADS = 4
A_HALF_WINDOW = 128
C_HALF_WINDOW = 64
C_DILATIONS = (1, 4, 16)
PACKED = 4 * HEAD_DIM
A_SUBTILE = 256
C_SUBTILE = 128
C_ROWS_PER_STEP = 2048
ROW_CHUNK = 256
BF16_SUBLANES = 16

V7X_VMEM_BYTES = 64 * 1024 * 1024
MIB = 1024 * 1024


def _cparams(n_grid, vmem_bytes):
    assert vmem_bytes < V7X_VMEM_BYTES, vmem_bytes
    return pltpu.CompilerParams(dimension_semantics=("arbitrary",) * n_grid,
                                vmem_limit_bytes=V7X_VMEM_BYTES)


def _rms(x, gain):
    ms = jnp.mean(x * x, axis=-1, keepdims=True)
    return x * lax.rsqrt(ms + NORM_EPS) * gain


def _lanes(slot):
    return slice(slot * HEAD_DIM, (slot + 1) * HEAD_DIM)


def _ffn_kernel(*refs, n_steps, n_chunk, final):
    if final:
        x_ref, g_ref, wg_ref, wu_ref, w2_ref, fin_ref, o_ref, xn_ref = refs
    else:
        x_ref, g_ref, wg_ref, wu_ref, w2_ref, o_ref, xn_ref = refs
    j = pl.program_id(1)
    row_chunks = [slice(r, r + ROW_CHUNK) for r in range(0, x_ref.shape[0], ROW_CHUNK)]

    @pl.when(j == 0)
    def _():
        for rows in row_chunks:
            x = x_ref[rows, :]
            xn_ref[rows, :] = _rms(x, g_ref[...]).astype(BF16)
            o_ref[rows, :] = x

    xn = xn_ref[...]
    gate = jnp.dot(xn, wg_ref[...], preferred_element_type=F32)
    up = jnp.dot(xn, wu_ref[...], preferred_element_type=F32)
    act = (gate * jax.nn.sigmoid(gate) * up * FFN_RESIDUAL).astype(BF16)
    for n in range(o_ref.shape[1] // n_chunk):
        cols = slice(n * n_chunk, (n + 1) * n_chunk)
        o_ref[:, cols] += jnp.dot(act, w2_ref[:, cols], preferred_element_type=F32)

    if final:
        @pl.when(j == n_steps - 1)
        def _():
            for rows in row_chunks:
                o_ref[rows, :] = _rms(o_ref[rows, :], fin_ref[...])


def _ffn(x, gain, w13, w2, final_gain=None, *, tm=512, tf=512, n_chunk=512):
    m, d = x.shape
    d_ff = w2.shape[0]
    n_chunk = min(n_chunk, d)
    assert m % tm == 0 and d_ff % tf == 0 and d % n_chunk == 0 and tm % ROW_CHUNK == 0
    n_steps = d_ff // tf
    final = final_gain is not None
    in_specs = [
        pl.BlockSpec((tm, d), lambda i, j: (i, 0)),
        pl.BlockSpec((1, d), lambda i, j: (0, 0)),
        pl.BlockSpec((d, tf), lambda i, j: (0, j)),
        pl.BlockSpec((d, tf), lambda i, j: (0, j + n_steps)),
        pl.BlockSpec((tf, d), lambda i, j: (j, 0)),
    ]
    args = [x, gain.reshape(1, d), w13, w13, w2]
    if final:
        in_specs.append(pl.BlockSpec((1, d), lambda i, j: (0, 0)))
        args.append(final_gain.reshape(1, d))
    vmem = 2 * tm * d * 4 + 2 * tm * d * 4 + tm * d * 2 + 2 * (3 * d * tf * 2) \
        + 3 * tm * tf * 4 + tm * tf * 2 + 3 * tm * n_chunk * 4 + 3 * ROW_CHUNK * d * 4 + 2 * MIB
    return pl.pallas_call(
        functools.partial(_ffn_kernel, n_steps=n_steps, n_chunk=n_chunk, final=final),
        grid=(m // tm, n_steps),
        in_specs=in_specs,
        out_specs=pl.BlockSpec((tm, d), lambda i, j: (i, 0)),
        out_shape=jax.ShapeDtypeStruct((m, d), F32),
        scratch_shapes=[pltpu.VMEM((tm, d), BF16)],
        compiler_params=_cparams(2, vmem),
        name="ffn_final" if final else "ffn",
    )(*args)


def _swap_quarter_pairs(x):
    q = HEAD_DIM // 4
    lane = lax.broadcasted_iota(jnp.int32, x.shape, 1)
    from_right = pltpu.roll(x, HEAD_DIM - q, axis=1)
    from_left = pltpu.roll(x, q, axis=1)
    return jnp.where((lane % (2 * q)) < q, from_right, from_left)


def _mixin_kernel(x_ref, g_ref, w_ref, qg_ref, kg_ref, cos_ref, sin_ref,
                  h_ref, za_ref, zbq_ref, zbkv_ref, zc0_ref, zc1_ref, zc2_ref, cs_ref):
    tm = x_ref.shape[0]
    h = _rms(x_ref[...], g_ref[...]).astype(BF16)
    h_ref[...] = h
    cos = cos_ref[...]
    sin = sin_ref[...]

    def norm_rope(v, gain):
        ms = jnp.mean(v * v, axis=-1, keepdims=True)
        y = v * lax.rsqrt(ms + NORM_EPS) * gain
        return y * cos + _swap_quarter_pairs(y) * sin

    for c in range(N_HEAD_BLOCKS // 2):
        zc = jnp.dot(h, w_ref[:, c * 256:(c + 1) * 256], preferred_element_type=F32)
        for hh in range(2):
            blk = 2 * c + hh
            v = zc[:, _lanes(hh)]
            if blk < A_K:
                za_ref[0, blk // 2, :, _lanes(blk % 2)] = (v * QK_SCALE).astype(BF16)
            elif blk < A_V:
                za_ref[0, blk - A_K, :, _lanes(2)] = v.astype(BF16)
            elif blk < B_Q:
                za_ref[0, blk - A_V, :, _lanes(3)] = v.astype(BF16)
            elif blk < B_K:
                q = norm_rope(v, qg_ref[...]) * (QK_SCALE * LOG2E)
                zbq_ref[0, (blk - B_Q) // 3, :, _lanes((blk - B_Q) % 3)] = q.astype(BF16)
            elif blk < B_V:
                zbkv_ref[0, blk - B_K, :, _lanes(0)] = norm_rope(v, kg_ref[...]).astype(BF16)
            elif blk < C_Q:
                zbkv_ref[0, blk - B_V, :, _lanes(1)] = v.astype(BF16)
            elif blk < C_K:
                cs_ref[blk - C_Q] = v * QK_SCALE
            else:
                cs_ref[blk - C_Q] = v
    for kv in range(zbkv_ref.shape[1]):
        zbkv_ref[0, kv, :, _lanes(2)] = jnp.ones((tm, HEAD_DIM), BF16)

    n_pairs = len(C_DILATIONS)
    for pair, (r, zc_ref) in enumerate(zip(C_DILATIONS, (zc0_ref, zc1_ref, zc2_ref))):
        staged = (2 * pair, 2 * pair + 1, 2 * n_pairs + pair, 3 * n_pairs + pair)
        for slot, src in enumerate(staged):
            for c in range(r):
                rows = pl.ds(c, tm // r, stride=r) if r > 1 else slice(None)
                zc_ref[0, c, :, _lanes(slot)] = cs_ref[src, rows, :].astype(BF16)


def _mix_in(x, gain, w_in, q_gain, k_gain, cos_tab, sin_tab, *, tm=512):
    m, d = x.shape
    s = cos_tab.shape[0]
    b = m // s
    n_cols = w_in.shape[1]
    assert m % tm == 0 and s % tm == 0 and n_cols == N_HEAD_BLOCKS * HEAD_DIM
    assert all(tm % (16 * r) == 0 for r in C_DILATIONS)
    pos_tiles = s // tm

    def packed(n, width, rows):
        return pl.BlockSpec((1, n, rows, width), lambda i: (i // pos_tiles, 0, i % pos_tiles, 0))

    out_specs = [pl.BlockSpec((tm, d), lambda i: (i, 0)),
                 packed(2, PACKED, tm), packed(2, 3 * HEAD_DIM, tm), packed(2, 3 * HEAD_DIM, tm)]
    out_shape = [jax.ShapeDtypeStruct((m, d), BF16),
                 jax.ShapeDtypeStruct((b, 2, s, PACKED), BF16),
                 jax.ShapeDtypeStruct((b, 2, s, 3 * HEAD_DIM), BF16),
                 jax.ShapeDtypeStruct((b, 2, s, 3 * HEAD_DIM), BF16)]
    for r in C_DILATIONS:
        out_specs.append(packed(r, PACKED, tm // r))
        out_shape.append(jax.ShapeDtypeStruct((b, r, s // r, PACKED), BF16))
    c_cols = (N_HEAD_BLOCKS - C_Q) * HEAD_DIM
    vmem = 2 * tm * d * 4 + 2 * tm * d * 2 + 2 * tm * (n_cols + 2 * HEAD_DIM) * 2 + d * n_cols * 2 \
        + 4 * tm * HEAD_DIM * 4 + tm * c_cols * 4 + tm * d * 4 + 4 * tm * 256 * 4 \
        + 8 * tm * HEAD_DIM * 4 + 2 * MIB
    return pl.pallas_call(
        _mixin_kernel,
        grid=(m // tm,),
        in_specs=[
            pl.BlockSpec((tm, d), lambda i: (i, 0)),
            pl.BlockSpec((1, d), lambda i: (0, 0)),
            pl.BlockSpec((d, n_cols), lambda i: (0, 0), pipeline_mode=pl.Buffered(1)),
            pl.BlockSpec((1, HEAD_DIM), lambda i: (0, 0)),
            pl.BlockSpec((1, HEAD_DIM), lambda i: (0, 0)),
            pl.BlockSpec((tm, HEAD_DIM), lambda i: (i % pos_tiles, 0)),
            pl.BlockSpec((tm, HEAD_DIM), lambda i: (i % pos_tiles, 0)),
        ],
        out_specs=out_specs,
        out_shape=out_shape,
        scratch_shapes=[pltpu.VMEM((N_HEAD_BLOCKS - C_Q, tm, HEAD_DIM), F32)],
        compiler_params=_cparams(1, vmem),
        name="mix_in",
    )(x, gain.reshape(1, d), w_in, q_gain.reshape(1, HEAD_DIM), k_gain.reshape(1, HEAD_DIM),
      cos_tab, sin_tab)


def _t5_bucket(rel):
    half = NUM_BUCKETS // 2
    max_exact = half // 2
    ret = jnp.where(rel > 0, half, 0)
    n = jnp.abs(rel)
    nf = jnp.maximum(n, 1).astype(jnp.float32)
    large = max_exact + (jnp.log(nf / max_exact) / math.log(REL_MAX_DISTANCE / max_exact)
                         * (half - max_exact)).astype(jnp.int32)
    large = jnp.minimum(large, half - 1)
    return ret + jnp.where(n < max_exact, n, large)


def _band_rel(ts, halo):
    return jnp.arange(ts + 2 * halo)[None, :] - halo - jnp.arange(ts)[:, None]


def _band_kernel(*refs, halo, ts, seq_len, n_heads, head0, n_table_cols, with_sink):
    if with_sink:
        bkt_ref, tab_ref, sink_ref, left_ref, main_ref, right_ref, o_ref, bias_ref = refs
        lse_ref = None
    else:
        bkt_ref, tab_ref, left_ref, main_ref, right_ref, o_ref, lse_ref, bias_ref = refs
        sink_ref = None
    n_sub, tq = main_ref.shape[1], main_ref.shape[2]
    nk = ts + 2 * halo
    first = (pl.program_id(0) == 0) & (pl.program_id(1) == 0) & (pl.program_id(2) == 0)

    @pl.when(first)
    def _():
        bkt = bkt_ref[...]
        rel = (lax.broadcasted_iota(jnp.int32, (ts, nk), 1) - halo
               - lax.broadcasted_iota(jnp.int32, (ts, nk), 0))
        in_band = jnp.abs(rel) <= halo
        tiles = [jnp.zeros((ts, nk), F32) for _ in range(n_heads)]
        for t in range(NUM_BUCKETS):
            hit = bkt == t
            for h in range(n_heads):
                tiles[h] = jnp.where(hit, tab_ref[t * n_table_cols + head0 + h], tiles[h])
        for h in range(n_heads):
            bias_ref[h] = jnp.where(in_band, tiles[h], MASK_VALUE)

    qi = pl.program_id(2)
    for sub in range(n_sub):
        def keys(slot):
            return jnp.concatenate([left_ref[0, sub, :, _lanes(slot)], main_ref[0, sub, :, _lanes(slot)],
                                    right_ref[0, sub, :, _lanes(slot)]], axis=0)

        k = keys(2)
        v = keys(3)
        for t in range(tq // ts):
            rows = slice(t * ts, (t + 1) * ts)
            kt = k[t * ts:t * ts + nk]
            vt = v[t * ts:t * ts + nk]
            key_pos = qi * tq + t * ts - halo + lax.broadcasted_iota(jnp.int32, (1, nk), 1)
            key_ok = (key_pos >= 0) & (key_pos < seq_len)
            for g in range(2):
                head = (pl.program_id(1) * n_sub + sub) * 2 + g if with_sink else g
                logits = lax.dot_general(main_ref[0, sub, rows, _lanes(g)], kt, (((1,), (1,)), ((), ())),
                                         preferred_element_type=F32)
                logits = jnp.where(key_ok, logits + bias_ref[head], MASK_VALUE)
                m = jnp.max(logits, axis=-1, keepdims=True)
                if with_sink:
                    sink = sink_ref[head]
                    m = jnp.maximum(m, sink)
                p = jnp.exp(logits - m)
                denom = jnp.sum(p, axis=-1, keepdims=True)
                if with_sink:
                    denom = denom + jnp.exp(sink - m)
                out = jnp.dot(p.astype(BF16), vt, preferred_element_type=F32) / denom
                if with_sink:
                    o_ref[0, rows, _lanes(2 * sub + g)] = out.astype(o_ref.dtype)
                else:
                    o_ref[0, sub, rows, _lanes(g)] = out
                    lse_ref[0, sub, rows, _lanes(g)] = jnp.broadcast_to(m + jnp.log(denom), (ts, HEAD_DIM))


def _band_specs(tq, halo, n_sub, n_halo_blocks):
    per = tq // halo
    return [
        pl.BlockSpec((1, n_sub, halo, PACKED), lambda b, c, i: (b, c, jnp.maximum(i * per - 1, 0), 0)),
        pl.BlockSpec((1, n_sub, tq, PACKED), lambda b, c, i: (b, c, i, 0)),
        pl.BlockSpec((1, n_sub, halo, PACKED),
                     lambda b, c, i: (b, c, jnp.minimum((i + 1) * per, n_halo_blocks - 1), 0)),
    ]


def _band_vmem(tq, ts, halo, n_sub, n_heads, out_bytes):
    nk = ts + 2 * halo
    return (n_heads * ts * nk * 4 + 2 * ts * nk * 4 + 2 * n_sub * (tq + 2 * halo) * PACKED * 2 + 2 * out_bytes
            + 10 * ts * nk * 4 + 4 * (tq + 2 * halo) * HEAD_DIM * 2 + 4 * MIB)


def _attn_a(za, rel_bias_flat, sink, *, tq=512, n_sub=2):
    b, n_kv, s, _ = za.shape
    halo = A_HALF_WINDOW
    ts = A_SUBTILE
    assert s % tq == 0 and tq % halo == 0 and n_kv % n_sub == 0 and tq % ts == 0
    nk = ts + 2 * halo
    bkt = _t5_bucket(_band_rel(ts, halo)).astype(jnp.int32)
    smem = pl.BlockSpec(memory_space=pltpu.SMEM)
    out_w = n_sub * 2 * HEAD_DIM
    return pl.pallas_call(
        functools.partial(_band_kernel, halo=halo, ts=ts, seq_len=s, n_heads=A_HEADS, head0=0,
                          n_table_cols=rel_bias_flat.shape[0] // NUM_BUCKETS, with_sink=True),
        grid=(b, n_kv // n_sub, s // tq),
        in_specs=[pl.BlockSpec((ts, nk), lambda b_, c, i: (0, 0)), smem, smem]
        + _band_specs(tq, halo, n_sub, s // halo),
        out_specs=pl.BlockSpec((1, tq, out_w), lambda b_, c, i: (b_, i, c)),
        out_shape=jax.ShapeDtypeStruct((b, s, A_HEADS * HEAD_DIM), BF16),
        scratch_shapes=[pltpu.VMEM((A_HEADS, ts, nk), F32)],
        compiler_params=_cparams(3, _band_vmem(tq, ts, halo, n_sub, A_HEADS, tq * out_w * 2)),
        name="attn_a",
    )(bkt, rel_bias_flat, sink, za, za, za)


def _attn_c_pair(zc, rel_bias_flat, pair):
    b, r, sub, _ = zc.shape
    halo = C_HALF_WINDOW
    n_sub = min(r, C_ROWS_PER_STEP // C_SUBTILE)
    tq = min(sub, C_ROWS_PER_STEP // n_sub)
    ts = min(C_SUBTILE, tq)
    assert sub % tq == 0 and tq % halo == 0 and r % n_sub == 0 and tq % ts == 0
    nk = ts + 2 * halo
    bkt = _t5_bucket(_band_rel(ts, halo) * r).astype(jnp.int32)
    smem = pl.BlockSpec(memory_space=pltpu.SMEM)
    out_spec = pl.BlockSpec((1, n_sub, tq, 2 * HEAD_DIM), lambda b_, c, i: (b_, c, i, 0))
    return pl.pallas_call(
        functools.partial(_band_kernel, halo=halo, ts=ts, seq_len=sub, n_heads=2, head0=A_HEADS + 2 * pair,
                          n_table_cols=rel_bias_flat.shape[0] // NUM_BUCKETS, with_sink=False),
        grid=(b, r // n_sub, sub // tq),
        in_specs=[pl.BlockSpec((ts, nk), lambda b_, c, i: (0, 0)), smem]
        + _band_specs(tq, halo, n_sub, sub // halo),
        out_specs=[out_spec, out_spec],
        out_shape=[jax.ShapeDtypeStruct((b, r, sub, 2 * HEAD_DIM), F32)] * 2,
        scratch_shapes=[pltpu.VMEM((2, ts, nk), F32)],
        compiler_params=_cparams(3, _band_vmem(tq, ts, halo, n_sub, 2, 2 * n_sub * tq * 256 * 4)),
        name=f"attn_c{pair}",
    )(bkt, rel_bias_flat, zc, zc, zc)


def _c_merge_kernel(*refs):
    n = len(C_DILATIONS)
    o_refs, l_refs, y_ref = refs[:n], refs[n:2 * n], refs[2 * n]
    buf_ref = refs[2 * n + 1]
    tm = y_ref.shape[0]
    slots = iter(range(buf_ref.shape[0]))

    def token_order(ref, r, lanes):
        if r == 1:
            return ref[0, 0, :, lanes]
        slot = next(slots)
        for c in range(r):
            buf_ref[slot, pl.ds(c, tm // r, stride=r), :] = ref[0, c, :, lanes]
        return buf_ref[slot]

    for g in range(y_ref.shape[1] // HEAD_DIM):
        lanes = _lanes(g)
        outs = [token_order(ref, r, lanes) for ref, r in zip(o_refs, C_DILATIONS)]
        lses = [token_order(ref, r, lanes) for ref, r in zip(l_refs, C_DILATIONS)]
        m = functools.reduce(jnp.maximum, lses)
        es = [jnp.exp(l - m) for l in lses]
        tot = functools.reduce(jnp.add, es)
        y = functools.reduce(jnp.add, [(e / tot) * o for e, o in zip(es, outs)])
        y_ref[:, lanes] = y.astype(y_ref.dtype)


def _attn_c(zcs, rel_bias_flat, *, tm=512):
    outs, lses = zip(*[_attn_c_pair(zc, rel_bias_flat, p) for p, zc in enumerate(zcs)])
    b, _, s, w = outs[0].shape
    m = b * s
    assert s % tm == 0
    pos_tiles = s // tm

    def spec(r):
        return pl.BlockSpec((1, r, tm // r, w), lambda i: (i // pos_tiles, 0, i % pos_tiles, 0))

    n_slots = 2 * (w // HEAD_DIM) * sum(r > 1 for r in C_DILATIONS)
    return pl.pallas_call(
        _c_merge_kernel,
        grid=(m // tm,),
        in_specs=[spec(r) for r in C_DILATIONS] * 2,
        out_specs=pl.BlockSpec((tm, w), lambda i: (i, 0)),
        out_shape=jax.ShapeDtypeStruct((m, w), BF16),
        scratch_shapes=[pltpu.VMEM((n_slots, tm, HEAD_DIM), F32)],
        compiler_params=_cparams(1, 2 * 6 * tm * w * 4 + (n_slots + 16) * tm * HEAD_DIM * 4 + 2 * MIB),
        name="attn_c_merge",
    )(*outs, *lses)


def _attn_b_kernel(*refs, tk, n_casts):
    q_ref, kv_ref = refs[:2]
    cast_in = refs[2:2 + n_casts]
    o_ref = refs[2 + n_casts]
    cast_out = refs[3 + n_casts:]
    tq = q_ref.shape[2]
    s = kv_ref.shape[2]
    q = jnp.concatenate([q_ref[0, 0, :, _lanes(g)] for g in range(3)], axis=0)
    m = acc = None
    for c in range(s // tk):
        rows = slice(c * tk, (c + 1) * tk)
        logits = lax.dot_general(q, kv_ref[0, 0, rows, _lanes(0)], (((1,), (1,)), ((), ())),
                                 preferred_element_type=F32)
        cmax = jnp.max(logits, axis=-1, keepdims=True)
        m_new = cmax if c == 0 else jnp.maximum(m, cmax)
        p = jnp.exp2(logits - m_new)
        pv = jnp.dot(p.astype(BF16), kv_ref[0, 0, rows, HEAD_DIM:], preferred_element_type=F32)
        acc = pv if c == 0 else jnp.exp2(m - m_new) * acc + pv
        m = m_new
    out = acc[:, _lanes(0)] / acc[:, HEAD_DIM:HEAD_DIM + 1]
    for g in range(3):
        o_ref[0, :, _lanes(g)] = out[g * tq:(g + 1) * tq].astype(o_ref.dtype)
    for src, dst in zip(cast_in, cast_out):
        dst[...] = src[...].astype(BF16)


def _cast_blocks(n_rows, n_steps):
    n_blocks = n_steps
    while n_blocks > 1 and (n_rows % n_blocks or (n_rows // n_blocks) % BF16_SUBLANES):
        n_blocks //= 2
    assert n_steps % n_blocks == 0 and n_rows % n_blocks == 0
    return n_blocks


def _attn_b(zbq, zbkv, casts=(), *, tq=256, tk=256):
    b, n_kv, s, _ = zbq.shape
    assert s % tq == 0 and s % tk == 0
    n_qt = s // tq
    n_steps = b * n_kv * n_qt
    in_specs = [pl.BlockSpec((1, 1, tq, 3 * HEAD_DIM), lambda b_, c, i: (b_, c, i, 0)),
                pl.BlockSpec((1, 1, s, 3 * HEAD_DIM), lambda b_, c, i: (b_, c, 0, 0))]
    out_specs = [pl.BlockSpec((1, tq, 3 * HEAD_DIM), lambda b_, c, i: (b_, i, c))]
    out_shape = [jax.ShapeDtypeStruct((b, s, n_kv * 3 * HEAD_DIM), BF16)]
    vmem = 2 * tq * 384 * 2 + 2 * s * 384 * 2 + 2 * tq * 384 * 2 \
        + 8 * 3 * tq * tk * 4 + 12 * 3 * tq * HEAD_DIM * 4 + 2 * MIB
    for w, layer in casts:
        _, n_rows, n_cols = w.shape
        n_blocks = _cast_blocks(n_rows, n_steps)
        rep = n_steps // n_blocks
        rows = n_rows // n_blocks

        def block(b_, c, i, rep=rep):
            return ((b_ * n_kv + c) * n_qt + i) // rep

        in_specs.append(pl.BlockSpec((None, rows, n_cols),
                                     lambda b_, c, i, layer=layer, block=block: (layer, block(b_, c, i), 0)))
        out_specs.append(pl.BlockSpec((rows, n_cols), lambda b_, c, i, block=block: (block(b_, c, i), 0)))
        out_shape.append(jax.ShapeDtypeStruct((n_rows, n_cols), BF16))
        vmem += 2 * rows * n_cols * (4 + 2) + rows * n_cols * 4
    yb, *cast_out = pl.pallas_call(
        functools.partial(_attn_b_kernel, tk=tk, n_casts=len(casts)),
        grid=(b, n_kv, n_qt),
        in_specs=in_specs,
        out_specs=out_specs,
        out_shape=out_shape,
        compiler_params=_cparams(3, vmem),
        name="attn_b_cast" if casts else "attn_b",
    )(zbq, zbkv, *[w for w, _ in casts])
    return yb, cast_out


def _merge_kernel(x_ref, h_ref, ya_ref, yb_ref, yc_ref, wga_ref, wgb_ref, wgc_ref,
                  bga_ref, bgb_ref, bgc_ref, wa_ref, wb_ref, wc_ref, wo_ref, o_ref, *, n_chunk):
    h = h_ref[...]

    def branch(y_ref, w_ref, wg_ref, bg_ref):
        gate = jax.nn.sigmoid(jnp.dot(h, wg_ref[...], preferred_element_type=F32) + bg_ref[...])
        return gate * jnp.dot(y_ref[...], w_ref[...], preferred_element_type=F32)

    @pl.when(pl.program_id(1) == 0)
    def _():
        o_ref[...] = x_ref[...]

    merged = (branch(ya_ref, wa_ref, wga_ref, bga_ref) + branch(yb_ref, wb_ref, wgb_ref, bgb_ref)
              + branch(yc_ref, wc_ref, wgc_ref, bgc_ref)).astype(BF16)
    for n in range(o_ref.shape[1] // n_chunk):
        cols = slice(n * n_chunk, (n + 1) * n_chunk)
        o_ref[:, cols] += jnp.dot(merged, wo_ref[:, cols], preferred_element_type=F32)


def _merge(x, h, ya, yb, yc, w_gate, b_gate, w_br_a, w_br_b, w_br_c, w_o, *, tm=512, tn=512, n_chunk=512):
    m, d = x.shape
    tn = min(tn, d)
    n_chunk = min(n_chunk, d)
    assert m % tm == 0 and d % tn == 0 and d % n_chunk == 0
    n_steps = d // tn
    ka, kb, kc = ya.shape[1], yb.shape[1], yc.shape[1]

    def row(width):
        return pl.BlockSpec((tm, width), lambda i, j: (i, 0))

    def gate_w(br):
        return pl.BlockSpec((d, tn), lambda i, j: (0, br * n_steps + j))

    def gate_b(br):
        return pl.BlockSpec((1, tn), lambda i, j: (0, br * n_steps + j))

    def br_w(k):
        return pl.BlockSpec((k, tn), lambda i, j: (0, j))

    bg = b_gate.reshape(1, 3 * d)
    vmem = 2 * (tm * d * 4 + tm * d * 2 + tm * (ka + kb + kc) * 2) + 2 * tm * d * 4 \
        + 2 * (3 * d + ka + kb + kc + d) * tn * 2 + 7 * tm * tn * 4 + 3 * tm * n_chunk * 4 + 2 * MIB
    return pl.pallas_call(
        functools.partial(_merge_kernel, n_chunk=n_chunk),
        grid=(m // tm, n_steps),
        in_specs=[row(d), row(d), row(ka), row(kb), row(kc),
                  gate_w(0), gate_w(1), gate_w(2), gate_b(0), gate_b(1), gate_b(2),
                  br_w(ka), br_w(kb), br_w(kc),
                  pl.BlockSpec((tn, d), lambda i, j: (j, 0))],
        out_specs=pl.BlockSpec((tm, d), lambda i, j: (i, 0)),
        out_shape=jax.ShapeDtypeStruct((m, d), F32),
        compiler_params=_cparams(2, vmem),
        name="merge",
    )(x, h, ya, yb, yc, w_gate, w_gate, w_gate, bg, bg, bg, w_br_a, w_br_b, w_br_c, w_o)


def _rope_tables(s):
    t = jnp.arange(s)
    row_ids = (t // GRID_W).astype(F32)
    col_ids = (t % GRID_W).astype(F32)
    axis_dim = HEAD_DIM // 2
    inv_freq = ROPE_THETA ** (-jnp.arange(0, axis_dim, 2, dtype=F32) / axis_dim)
    ang_r = row_ids[:, None] * inv_freq
    ang_c = col_ids[:, None] * inv_freq
    cos = jnp.concatenate([jnp.cos(ang_r), jnp.cos(ang_r), jnp.cos(ang_c), jnp.cos(ang_c)], axis=1)
    sin = jnp.concatenate([-jnp.sin(ang_r), jnp.sin(ang_r), -jnp.sin(ang_c), jnp.sin(ang_c)], axis=1)
    return cos, sin


_EARLY_WEIGHTS = ("ffn1_w13", "ffn1_w2", "w_in")
_LATE_WEIGHTS = ("w_gate", "w_br_a", "w_br_b", "w_br_c", "w_o", "ffn2_w13", "ffn2_w2")


def _trunk(x, p, wb, host_casts):
    b, s, d = x.shape
    depth = p["w_in"].shape[0]
    cos_tab, sin_tab = _rope_tables(s)
    rel_flat = p["rel_bias"].reshape(-1)
    xf = x.reshape(b * s, d)
    for li in range(depth):
        xf = _ffn(xf, p["ffn1_norm"][li], wb["ffn1_w13", li], wb["ffn1_w2", li])
        h, za, zbq, zbkv, *zcs = _mix_in(xf, p["mix_norm"][li], wb["w_in", li], p["q_gain_b"][li],
                                        p["k_gain_b"][li], cos_tab, sin_tab)
        ya = _attn_a(za, rel_flat, p["sink_a"][li]).reshape(b * s, -1)
        keys = []
        if host_casts:
            keys = [(name, li) for name in _LATE_WEIGHTS]
            if li + 1 < depth:
                keys += [(name, li + 1) for name in _EARLY_WEIGHTS]
        yb, copies = _attn_b(zbq, zbkv, [(p[name], layer) for name, layer in keys])
        wb.update(zip(keys, copies))
        yc = _attn_c(zcs, rel_flat)
        xf = _merge(xf, h, ya, yb.reshape(b * s, -1), yc, wb["w_gate", li], p["b_gate"][li], wb["w_br_a", li],
                    wb["w_br_b", li], wb["w_br_c", li], wb["w_o", li])
        fin = p["final_norm"] if li == depth - 1 else None
        xf = _ffn(xf, p["ffn2_norm"][li], wb["ffn2_w13", li], wb["ffn2_w2", li], fin)
    return xf.reshape(b, s, d)


def kernel(x_prompt, x_sample, ffn1_norm, ffn1_w13, ffn1_w2, mix_norm, w_in, q_gain_b, k_gain_b, sink_a,
           w_gate, b_gate, w_br_a, w_br_b, w_br_c, w_o, ffn2_norm, ffn2_w13, ffn2_w2, rel_bias, final_norm):
    p = dict(ffn1_norm=ffn1_norm, ffn1_w13=ffn1_w13, ffn1_w2=ffn1_w2, mix_norm=mix_norm, w_in=w_in,
             q_gain_b=q_gain_b, k_gain_b=k_gain_b, sink_a=sink_a, w_gate=w_gate, b_gate=b_gate,
             w_br_a=w_br_a, w_br_b=w_br_b, w_br_c=w_br_c, w_o=w_o, ffn2_norm=ffn2_norm,
             ffn2_w13=ffn2_w13, ffn2_w2=ffn2_w2, rel_bias=rel_bias, final_norm=final_norm)
    wb = {(name, 0): p[name][0].astype(BF16) for name in _EARLY_WEIGHTS}
    y_prompt = _trunk(x_prompt, p, wb, host_casts=True)
    y_sample = _trunk(x_sample, p, wb, host_casts=False)
    return y_prompt, y_sample
```

```python
import functools
import math

import jax
import jax.numpy as jnp
from jax import lax
from jax.experimental import pallas as pl
from jax.experimental.pallas import tpu as pltpu

F32 = jnp.float32
BF16 = jnp.bfloat16

HEAD_DIM = 128
NORM_EPS = 1e-6
MASK_VALUE = -1e30
QK_SCALE = 1.0 / math.sqrt(HEAD_DIM)
LOG2E = math.log2(math.e)
FFN_RESIDUAL = 0.5
ROPE_THETA = 10000.0
GRID_W = 64
NUM_BUCKETS = 32
REL_MAX_DISTANCE = 2048

A_Q, A_K, A_V = 0, 4, 6
B_Q, B_K, B_V = 8, 14, 16
C_Q, C_K, C_V = 18, 24, 27
N_HEAD_BLOCKS = 30
A_HEADS = 4
A_HALF_WINDOW = 128
C_HALF_WINDOW = 64
C_DILATIONS = (1, 4, 16)
PACKED = 4 * HEAD_DIM
A_SUBTILE = 256
C_SUBTILE = 128
C_ROWS_PER_STEP = 2048
ROW_CHUNK = 256
BF16_SUBLANES = 16

V7X_VMEM_BYTES = 64 * 1024 * 1024
MIB = 1024 * 1024


def _cparams(n_grid, vmem_bytes):
    assert vmem_bytes < V7X_VMEM_BYTES, vmem_bytes
    return pltpu.CompilerParams(dimension_semantics=("arbitrary",) * n_grid,
                                vmem_limit_bytes=V7X_VMEM_BYTES)


def _rms(x, gain):
    ms = jnp.mean(x * x, axis=-1, keepdims=True)
    return x * lax.rsqrt(ms + NORM_EPS) * gain


def _lanes(slot):
    return slice(slot * HEAD_DIM, (slot + 1) * HEAD_DIM)


def _ffn_kernel(*refs, n_steps, n_chunk, final):
    if final:
        x_ref, g_ref, wg_ref, wu_ref, w2_ref, fin_ref, o_ref, xn_ref = refs
    else:
        x_ref, g_ref, wg_ref, wu_ref, w2_ref, o_ref, xn_ref = refs
    j = pl.program_id(1)
    row_chunks = [slice(r, r + ROW_CHUNK) for r in range(0, x_ref.shape[0], ROW_CHUNK)]

    @pl.when(j == 0)
    def _():
        for rows in row_chunks:
            x = x_ref[rows, :]
            xn_ref[rows, :] = _rms(x, g_ref[...]).astype(BF16)
            o_ref[rows, :] = x

    xn = xn_ref[...]
    gate = jnp.dot(xn, wg_ref[...], preferred_element_type=F32)
    up = jnp.dot(xn, wu_ref[...], preferred_element_type=F32)
    act = (gate * jax.nn.sigmoid(gate) * up * FFN_RESIDUAL).astype(BF16)
    for n in range(o_ref.shape[1] // n_chunk):
        cols = slice(n * n_chunk, (n + 1) * n_chunk)
        o_ref[:, cols] += jnp.dot(act, w2_ref[:, cols], preferred_element_type=F32)

    if final:
        @pl.when(j == n_steps - 1)
        def _():
            for rows in row_chunks:
                o_ref[rows, :] = _rms(o_ref[rows, :], fin_ref[...])


def _ffn(x, gain, w13, w2, final_gain=None, *, tm=512, tf=512, n_chunk=512, w_bufs=2):
    m, d = x.shape
    d_ff = w2.shape[0]
    n_chunk = min(n_chunk, d)
    assert m % tm == 0 and d_ff % tf == 0 and d % n_chunk == 0 and tm % ROW_CHUNK == 0
    n_steps = d_ff // tf
    final = final_gain is not None
    x_bufs = 1 if tm * d * 4 >= 8 * MIB else 2
    mode = pl.Buffered(w_bufs)
    if w13.ndim == 3:
        assert w13.shape[1:] == (d, tf)
        gate_spec = pl.BlockSpec((None, d, tf), lambda i, j: (j, 0, 0), pipeline_mode=mode)
        up_spec = pl.BlockSpec((None, d, tf), lambda i, j: (j + n_steps, 0, 0), pipeline_mode=mode)
    else:
        gate_spec = pl.BlockSpec((d, tf), lambda i, j: (0, j), pipeline_mode=mode)
        up_spec = pl.BlockSpec((d, tf), lambda i, j: (0, j + n_steps), pipeline_mode=mode)
    in_specs = [
        pl.BlockSpec((tm, d), lambda i, j: (i, 0), pipeline_mode=pl.Buffered(x_bufs)),
        pl.BlockSpec((1, d), lambda i, j: (0, 0)),
        gate_spec,
        up_spec,
        pl.BlockSpec((tf, d), lambda i, j: (j, 0), pipeline_mode=mode),
    ]
    args = [x, gain.reshape(1, d), w13, w13, w2]
    if final:
        in_specs.append(pl.BlockSpec((1, d), lambda i, j: (0, 0)))
        args.append(final_gain.reshape(1, d))
    vmem = x_bufs * tm * d * 4 + 2 * tm * d * 4 + tm * d * 2 + w_bufs * (3 * d * tf * 2) \
        + 3 * tm * tf * 4 + tm * tf * 2 + 3 * tm * n_chunk * 4 + 3 * ROW_CHUNK * d * 4 + 2 * MIB
    return pl.pallas_call(
        functools.partial(_ffn_kernel, n_steps=n_steps, n_chunk=n_chunk, final=final),
        grid=(m // tm, n_steps),
        in_specs=in_specs,
        out_specs=pl.BlockSpec((tm, d), lambda i, j: (i, 0)),
        out_shape=jax.ShapeDtypeStruct((m, d), F32),
        scratch_shapes=[pltpu.VMEM((tm, d), BF16)],
        compiler_params=_cparams(2, vmem),
        name=("ffnfin" if final else "ffn") + f"_{tm}_{w_bufs}_{w13.ndim}",
    )(*args)


def _swap_quarter_pairs(x):
    q = HEAD_DIM // 4
    lane = lax.broadcasted_iota(jnp.int32, x.shape, 1)
    from_right = pltpu.roll(x, HEAD_DIM - q, axis=1)
    from_left = pltpu.roll(x, q, axis=1)
    return jnp.where((lane % (2 * q)) < q, from_right, from_left)


def _mixin_kernel(x_ref, g_ref, w_ref, qg_ref, kg_ref, cos_ref, sin_ref,
                  h_ref, za_ref, zbq_ref, zbkv_ref, zc0_ref, zc1_ref, zc2_ref, cs_ref):
    tm = x_ref.shape[0]
    h = _rms(x_ref[...], g_ref[...]).astype(BF16)
    h_ref[...] = h
    cos = cos_ref[...]
    sin = sin_ref[...]

    def norm_rope(v, gain):
        ms = jnp.mean(v * v, axis=-1, keepdims=True)
        y = v * lax.rsqrt(ms + NORM_EPS) * gain
        return y * cos + _swap_quarter_pairs(y) * sin

    for c in range(N_HEAD_BLOCKS // 2):
        zc = jnp.dot(h, w_ref[:, c * 256:(c + 1) * 256], preferred_element_type=F32)
        for hh in range(2):
            blk = 2 * c + hh
            v = zc[:, _lanes(hh)]
            if blk < A_K:
                za_ref[0, blk // 2, :, _lanes(blk % 2)] = (v * QK_SCALE).astype(BF16)
            elif blk < A_V:
                za_ref[0, blk - A_K, :, _lanes(2)] = v.astype(BF16)
            elif blk < B_Q:
                za_ref[0, blk - A_V, :, _lanes(3)] = v.astype(BF16)
            elif blk < B_K:
                q = norm_rope(v, qg_ref[...]) * (QK_SCALE * LOG2E)
                zbq_ref[0, (blk - B_Q) // 3, :, _lanes((blk - B_Q) % 3)] = q.astype(BF16)
            elif blk < B_V:
                zbkv_ref[0, blk - B_K, :, _lanes(0)] = norm_rope(v, kg_ref[...]).astype(BF16)
            elif blk < C_Q:
                zbkv_ref[0, blk - B_V, :, _lanes(1)] = v.astype(BF16)
            elif blk < C_K:
                cs_ref[blk - C_Q] = v * QK_SCALE
            else:
                cs_ref[blk - C_Q] = v
    for kv in range(zbkv_ref.shape[1]):
        zbkv_ref[0, kv, :, _lanes(2)] = jnp.ones((tm, HEAD_DIM), BF16)

    n_pairs = len(C_DILATIONS)
    for pair, (r, zc_ref) in enumerate(zip(C_DILATIONS, (zc0_ref, zc1_ref, zc2_ref))):
        staged = (2 * pair, 2 * pair + 1, 2 * n_pairs + pair, 3 * n_pairs + pair)
        for slot, src in enumerate(staged):
            for c in range(r):
                rows = pl.ds(c, tm // r, stride=r) if r > 1 else slice(None)
                zc_ref[0, c, :, _lanes(slot)] = cs_ref[src, rows, :].astype(BF16)


def _mix_in(x, gain, w_in, q_gain, k_gain, cos_tab, sin_tab, *, tm=512):
    m, d = x.shape
    s = cos_tab.shape[0]
    b = m // s
    n_cols = w_in.shape[1]
    assert m % tm == 0 and s % tm == 0 and n_cols == N_HEAD_BLOCKS * HEAD_DIM
    assert all(tm % (16 * r) == 0 for r in C_DILATIONS)
    pos_tiles = s // tm

    def packed(n, width, rows):
        return pl.BlockSpec((1, n, rows, width), lambda i: (i // pos_tiles, 0, i % pos_tiles, 0))

    out_specs = [pl.BlockSpec((tm, d), lambda i: (i, 0)),
                 packed(2, PACKED, tm), packed(2, 3 * HEAD_DIM, tm), packed(2, 3 * HEAD_DIM, tm)]
    out_shape = [jax.ShapeDtypeStruct((m, d), BF16),
                 jax.ShapeDtypeStruct((b, 2, s, PACKED), BF16),
                 jax.ShapeDtypeStruct((b, 2, s, 3 * HEAD_DIM), BF16),
                 jax.ShapeDtypeStruct((b, 2, s, 3 * HEAD_DIM), BF16)]
    for r in C_DILATIONS:
        out_specs.append(packed(r, PACKED, tm // r))
        out_shape.append(jax.ShapeDtypeStruct((b, r, s // r, PACKED), BF16))
    c_cols = (N_HEAD_BLOCKS - C_Q) * HEAD_DIM
    vmem = 2 * tm * d * 4 + 2 * tm * d * 2 + 2 * tm * (n_cols + 2 * HEAD_DIM) * 2 + d * n_cols * 2 \
        + 4 * tm * HEAD_DIM * 4 + tm * c_cols * 4 + tm * d * 4 + 4 * tm * 256 * 4 \
        + 8 * tm * HEAD_DIM * 4 + 2 * MIB
    return pl.pallas_call(
        _mixin_kernel,
        grid=(m // tm,),
        in_specs=[
            pl.BlockSpec((tm, d), lambda i: (i, 0)),
            pl.BlockSpec((1, d), lambda i: (0, 0)),
            pl.BlockSpec((d, n_cols), lambda i: (0, 0), pipeline_mode=pl.Buffered(1)),
            pl.BlockSpec((1, HEAD_DIM), lambda i: (0, 0)),
            pl.BlockSpec((1, HEAD_DIM), lambda i: (0, 0)),
            pl.BlockSpec((tm, HEAD_DIM), lambda i: (i % pos_tiles, 0)),
            pl.BlockSpec((tm, HEAD_DIM), lambda i: (i % pos_tiles, 0)),
        ],
        out_specs=out_specs,
        out_shape=out_shape,
        scratch_shapes=[pltpu.VMEM((N_HEAD_BLOCKS - C_Q, tm, HEAD_DIM), F32)],
        compiler_params=_cparams(1, vmem),
        name="mix_in",
    )(x, gain.reshape(1, d), w_in, q_gain.reshape(1, HEAD_DIM), k_gain.reshape(1, HEAD_DIM),
      cos_tab, sin_tab)


def _t5_bucket(rel):
    half = NUM_BUCKETS // 2
    max_exact = half // 2
    ret = jnp.where(rel > 0, half, 0)
    n = jnp.abs(rel)
    nf = jnp.maximum(n, 1).astype(jnp.float32)
    large = max_exact + (jnp.log(nf / max_exact) / math.log(REL_MAX_DISTANCE / max_exact)
                         * (half - max_exact)).astype(jnp.int32)
    large = jnp.minimum(large, half - 1)
    return ret + jnp.where(n < max_exact, n, large)


def _band_rel(ts, halo):
    return jnp.arange(ts + 2 * halo)[None, :] - halo - jnp.arange(ts)[:, None]


def _band_kernel(*refs, halo, ts, seq_len, n_heads, head0, n_table_cols, with_sink):
    if with_sink:
        bkt_ref, tab_ref, sink_ref, left_ref, main_ref, right_ref, o_ref, bias_ref = refs
        lse_ref = None
    else:
        bkt_ref, tab_ref, left_ref, main_ref, right_ref, o_ref, lse_ref, bias_ref = refs
        sink_ref = None
    n_sub, tq = main_ref.shape[1], main_ref.shape[2]
    nk = ts + 2 * halo
    first = (pl.program_id(0) == 0) & (pl.program_id(1) == 0) & (pl.program_id(2) == 0)

    @pl.when(first)
    def _():
        bkt = bkt_ref[...]
        rel = (lax.broadcasted_iota(jnp.int32, (ts, nk), 1) - halo
               - lax.broadcasted_iota(jnp.int32, (ts, nk), 0))
        in_band = jnp.abs(rel) <= halo
        tiles = [jnp.zeros((ts, nk), F32) for _ in range(n_heads)]
        for t in range(NUM_BUCKETS):
            hit = bkt == t
            for h in range(n_heads):
                tiles[h] = jnp.where(hit, tab_ref[t * n_table_cols + head0 + h], tiles[h])
        for h in range(n_heads):
            bias_ref[h] = jnp.where(in_band, tiles[h], MASK_VALUE)

    qi = pl.program_id(2)
    for sub in range(n_sub):
        def keys(slot):
            return jnp.concatenate([left_ref[0, sub, :, _lanes(slot)], main_ref[0, sub, :, _lanes(slot)],
                                    right_ref[0, sub, :, _lanes(slot)]], axis=0)

        k = keys(2)
        v = keys(3)
        for t in range(tq // ts):
            rows = slice(t * ts, (t + 1) * ts)
            kt = k[t * ts:t * ts + nk]
            vt = v[t * ts:t * ts + nk]
            key_pos = qi * tq + t * ts - halo + lax.broadcasted_iota(jnp.int32, (1, nk), 1)
            key_ok = (key_pos >= 0) & (key_pos < seq_len)
            for g in range(2):
                head = (pl.program_id(1) * n_sub + sub) * 2 + g if with_sink else g
                logits = lax.dot_general(main_ref[0, sub, rows, _lanes(g)], kt, (((1,), (1,)), ((), ())),
                                         preferred_element_type=F32)
                logits = jnp.where(key_ok, logits + bias_ref[head], MASK_VALUE)
                m = jnp.max(logits, axis=-1, keepdims=True)
                if with_sink:
                    sink = sink_ref[head]
                    m = jnp.maximum(m, sink)
                p = jnp.exp(logits - m)
                denom = jnp.sum(p, axis=-1, keepdims=True)
                if with_sink:
                    denom = denom + jnp.exp(sink - m)
                out = jnp.dot(p.astype(BF16), vt, preferred_element_type=F32) / denom
                if with_sink:
                    o_ref[0, rows, _lanes(2 * sub + g)] = out.astype(o_ref.dtype)
                else:
                    o_ref[0, sub, rows, _lanes(g)] = out
                    lse_ref[0, sub, rows, _lanes(g)] = jnp.broadcast_to(m + jnp.log(denom), (ts, HEAD_DIM))


def _band_specs(tq, halo, n_sub, n_halo_blocks):
    per = tq // halo
    return [
        pl.BlockSpec((1, n_sub, halo, PACKED), lambda b, c, i: (b, c, jnp.maximum(i * per - 1, 0), 0)),
        pl.BlockSpec((1, n_sub, tq, PACKED), lambda b, c, i: (b, c, i, 0)),
        pl.BlockSpec((1, n_sub, halo, PACKED),
                     lambda b, c, i: (b, c, jnp.minimum((i + 1) * per, n_halo_blocks - 1), 0)),
    ]


def _band_vmem(tq, ts, halo, n_sub, n_heads, out_bytes):
    nk = ts + 2 * halo
    return (n_heads * ts * nk * 4 + 2 * ts * nk * 4 + 2 * n_sub * (tq + 2 * halo) * PACKED * 2 + 2 * out_bytes
            + 10 * ts * nk * 4 + 4 * (tq + 2 * halo) * HEAD_DIM * 2 + 4 * MIB)


def _attn_a(za, rel_bias_flat, sink, *, tq=512, n_sub=2):
    b, n_kv, s, _ = za.shape
    halo = A_HALF_WINDOW
    ts = A_SUBTILE
    assert s % tq == 0 and tq % halo == 0 and n_kv % n_sub == 0 and tq % ts == 0
    nk = ts + 2 * halo
    bkt = _t5_bucket(_band_rel(ts, halo)).astype(jnp.int32)
    smem = pl.BlockSpec(memory_space=pltpu.SMEM)
    out_w = n_sub * 2 * HEAD_DIM
    return pl.pallas_call(
        functools.partial(_band_kernel, halo=halo, ts=ts, seq_len=s, n_heads=A_HEADS, head0=0,
                          n_table_cols=rel_bias_flat.shape[0] // NUM_BUCKETS, with_sink=True),
        grid=(b, n_kv // n_sub, s // tq),
        in_specs=[pl.BlockSpec((ts, nk), lambda b_, c, i: (0, 0)), smem, smem]
        + _band_specs(tq, halo, n_sub, s // halo),
        out_specs=pl.BlockSpec((1, tq, out_w), lambda b_, c, i: (b_, i, c)),
        out_shape=jax.ShapeDtypeStruct((b, s, A_HEADS * HEAD_DIM), BF16),
        scratch_shapes=[pltpu.VMEM((A_HEADS, ts, nk), F32)],
        compiler_params=_cparams(3, _band_vmem(tq, ts, halo, n_sub, A_HEADS, tq * out_w * 2)),
        name="attn_a",
    )(bkt, rel_bias_flat, sink, za, za, za)


def _attn_c_pair(zc, rel_bias_flat, pair):
    b, r, sub, _ = zc.shape
    halo = C_HALF_WINDOW
    n_sub = min(r, C_ROWS_PER_STEP // C_SUBTILE)
    tq = min(sub, C_ROWS_PER_STEP // n_sub)
    ts = min(C_SUBTILE, tq)
    assert sub % tq == 0 and tq % halo == 0 and r % n_sub == 0 and tq % ts == 0
    nk = ts + 2 * halo
    bkt = _t5_bucket(_band_rel(ts, halo) * r).astype(jnp.int32)
    smem = pl.BlockSpec(memory_space=pltpu.SMEM)
    out_spec = pl.BlockSpec((1, n_sub, tq, 2 * HEAD_DIM), lambda b_, c, i: (b_, c, i, 0))
    return pl.pallas_call(
        functools.partial(_band_kernel, halo=halo, ts=ts, seq_len=sub, n_heads=2, head0=A_HEADS + 2 * pair,
                          n_table_cols=rel_bias_flat.shape[0] // NUM_BUCKETS, with_sink=False),
        grid=(b, r // n_sub, sub // tq),
        in_specs=[pl.BlockSpec((ts, nk), lambda b_, c, i: (0, 0)), smem]
        + _band_specs(tq, halo, n_sub, sub // halo),
        out_specs=[out_spec, out_spec],
        out_shape=[jax.ShapeDtypeStruct((b, r, sub, 2 * HEAD_DIM), F32)] * 2,
        scratch_shapes=[pltpu.VMEM((2, ts, nk), F32)],
        compiler_params=_cparams(3, _band_vmem(tq, ts, halo, n_sub, 2, 2 * n_sub * tq * 256 * 4)),
        name=f"attn_c{pair}",
    )(bkt, rel_bias_flat, zc, zc, zc)


def _c_merge_kernel(*refs):
    n = len(C_DILATIONS)
    o_refs, l_refs, y_ref = refs[:n], refs[n:2 * n], refs[2 * n]
    buf_ref = refs[2 * n + 1]
    tm = y_ref.shape[0]
    slots = iter(range(buf_ref.shape[0]))

    def token_order(ref, r, lanes):
        if r == 1:
            return ref[0, 0, :, lanes]
        slot = next(slots)
        for c in range(r):
            buf_ref[slot, pl.ds(c, tm // r, stride=r), :] = ref[0, c, :, lanes]
        return buf_ref[slot]

    for g in range(y_ref.shape[1] // HEAD_DIM):
        lanes = _lanes(g)
        outs = [token_order(ref, r, lanes) for ref, r in zip(o_refs, C_DILATIONS)]
        lses = [token_order(ref, r, lanes) for ref, r in zip(l_refs, C_DILATIONS)]
        m = functools.reduce(jnp.maximum, lses)
        es = [jnp.exp(l - m) for l in lses]
        tot = functools.reduce(jnp.add, es)
        y = functools.reduce(jnp.add, [(e / tot) * o for e, o in zip(es, outs)])
        y_ref[:, lanes] = y.astype(y_ref.dtype)


def _attn_c(zcs, rel_bias_flat, *, tm=512):
    outs, lses = zip(*[_attn_c_pair(zc, rel_bias_flat, p) for p, zc in enumerate(zcs)])
    b, _, s, w = outs[0].shape
    m = b * s
    assert s % tm == 0
    pos_tiles = s // tm

    def spec(r):
        return pl.BlockSpec((1, r, tm // r, w), lambda i: (i // pos_tiles, 0, i % pos_tiles, 0))

    n_slots = 2 * (w // HEAD_DIM) * sum(r > 1 for r in C_DILATIONS)
    return pl.pallas_call(
        _c_merge_kernel,
        grid=(m // tm,),
        in_specs=[spec(r) for r in C_DILATIONS] * 2,
        out_specs=pl.BlockSpec((tm, w), lambda i: (i, 0)),
        out_shape=jax.ShapeDtypeStruct((m, w), BF16),
        scratch_shapes=[pltpu.VMEM((n_slots, tm, HEAD_DIM), F32)],
        compiler_params=_cparams(1, 2 * 6 * tm * w * 4 + (n_slots + 16) * tm * HEAD_DIM * 4 + 2 * MIB),
        name="attn_c_merge",
    )(*outs, *lses)


def _attn_b_kernel(*refs, tk, n_casts):
    q_ref, kv_ref = refs[:2]
    cast_in = refs[2:2 + n_casts]
    o_ref = refs[2 + n_casts]
    cast_out = refs[3 + n_casts:]
    tq = q_ref.shape[2]
    s = kv_ref.shape[2]
    q = jnp.concatenate([q_ref[0, 0, :, _lanes(g)] for g in range(3)], axis=0)
    m = acc = None
    for c in range(s // tk):
        rows = slice(c * tk, (c + 1) * tk)
        logits = lax.dot_general(q, kv_ref[0, 0, rows, _lanes(0)], (((1,), (1,)), ((), ())),
                                 preferred_element_type=F32)
        cmax = jnp.max(logits, axis=-1, keepdims=True)
        m_new = cmax if c == 0 else jnp.maximum(m, cmax)
        p = jnp.exp2(logits - m_new)
        pv = jnp.dot(p.astype(BF16), kv_ref[0, 0, rows, HEAD_DIM:], preferred_element_type=F32)
        acc = pv if c == 0 else jnp.exp2(m - m_new) * acc + pv
        m = m_new
    out = acc[:, _lanes(0)] / acc[:, HEAD_DIM:HEAD_DIM + 1]
    for g in range(3):
        o_ref[0, :, _lanes(g)] = out[g * tq:(g + 1) * tq].astype(o_ref.dtype)
    for src, dst in zip(cast_in, cast_out):
        if len(dst.shape) == 3:
            width = dst.shape[2]
            for cb in range(dst.shape[0]):
                dst[cb] = src[:, cb * width:(cb + 1) * width].astype(BF16)
        else:
            dst[...] = src[...].astype(BF16)


def _cast_blocks(n_rows, n_steps):
    n_blocks = n_steps
    while n_blocks > 1 and (n_rows % n_blocks or (n_rows // n_blocks) % BF16_SUBLANES):
        n_blocks //= 2
    assert n_steps % n_blocks == 0 and n_rows % n_blocks == 0
    return n_blocks


def _attn_b(zbq, zbkv, casts=(), *, tq=256, tk=256):
    b, n_kv, s, _ = zbq.shape
    assert s % tq == 0 and s % tk == 0
    n_qt = s // tq
    n_steps = b * n_kv * n_qt
    in_specs = [pl.BlockSpec((1, 1, tq, 3 * HEAD_DIM), lambda b_, c, i: (b_, c, i, 0)),
                pl.BlockSpec((1, 1, s, 3 * HEAD_DIM), lambda b_, c, i: (b_, c, 0, 0))]
    out_specs = [pl.BlockSpec((1, tq, 3 * HEAD_DIM), lambda b_, c, i: (b_, i, c))]
    out_shape = [jax.ShapeDtypeStruct((b, s, n_kv * 3 * HEAD_DIM), BF16)]
    vmem = 2 * tq * 384 * 2 + 2 * s * 384 * 2 + 2 * tq * 384 * 2 \
        + 8 * 3 * tq * tk * 4 + 12 * 3 * tq * HEAD_DIM * 4 + 2 * MIB
    for w, layer, col_block in casts:
        _, n_rows, n_cols = w.shape
        n_blocks = _cast_blocks(n_rows, n_steps)
        rep = n_steps // n_blocks
        rows = n_rows // n_blocks

        def block(b_, c, i, rep=rep):
            return ((b_ * n_kv + c) * n_qt + i) // rep

        in_specs.append(pl.BlockSpec((None, rows, n_cols),
                                     lambda b_, c, i, layer=layer, block=block: (layer, block(b_, c, i), 0)))
        if col_block is None:
            out_specs.append(pl.BlockSpec((rows, n_cols), lambda b_, c, i, block=block: (block(b_, c, i), 0)))
            out_shape.append(jax.ShapeDtypeStruct((n_rows, n_cols), BF16))
        else:
            n_cb = n_cols // col_block
            out_specs.append(pl.BlockSpec((n_cb, rows, col_block),
                                          lambda b_, c, i, block=block: (0, block(b_, c, i), 0)))
            out_shape.append(jax.ShapeDtypeStruct((n_cb, n_rows, col_block), BF16))
        vmem += 2 * rows * n_cols * (4 + 2) + rows * n_cols * 4
    yb, *cast_out = pl.pallas_call(
        functools.partial(_attn_b_kernel, tk=tk, n_casts=len(casts)),
        grid=(b, n_kv, n_qt),
        in_specs=in_specs,
        out_specs=out_specs,
        out_shape=out_shape,
        compiler_params=_cparams(3, vmem),
        name="attn_b_cast" if casts else "attn_b",
    )(zbq, zbkv, *[w for w, _, _ in casts])
    return yb, cast_out


def _merge_kernel(x_ref, h_ref, ya_ref, yb_ref, yc_ref, wga_ref, wgb_ref, wgc_ref,
                  bga_ref, bgb_ref, bgc_ref, wa_ref, wb_ref, wc_ref, wo_ref, o_ref, *, n_chunk):
    h = h_ref[...]

    def branch(y_ref, w_ref, wg_ref, bg_ref):
        gate = jax.nn.sigmoid(jnp.dot(h, wg_ref[...], preferred_element_type=F32) + bg_ref[...])
        return gate * jnp.dot(y_ref[...], w_ref[...], preferred_element_type=F32)

    @pl.when(pl.program_id(1) == 0)
    def _():
        o_ref[...] = x_ref[...]

    merged = (branch(ya_ref, wa_ref, wga_ref, bga_ref) + branch(yb_ref, wb_ref, wgb_ref, bgb_ref)
              + branch(yc_ref, wc_ref, wgc_ref, bgc_ref)).astype(BF16)
    for n in range(o_ref.shape[1] // n_chunk):
        cols = slice(n * n_chunk, (n + 1) * n_chunk)
        o_ref[:, cols] += jnp.dot(merged, wo_ref[:, cols], preferred_element_type=F32)


def _merge(x, h, ya, yb, yc, w_gate, b_gate, w_br_a, w_br_b, w_br_c, w_o, *, tm=512, tn=512, n_chunk=512):
    m, d = x.shape
    tn = min(tn, d)
    n_chunk = min(n_chunk, d)
    assert m % tm == 0 and d % tn == 0 and d % n_chunk == 0
    n_steps = d // tn
    ka, kb, kc = ya.shape[1], yb.shape[1], yc.shape[1]
    assert w_gate.ndim == 2 or w_gate.shape[1:] == (d, tn)

    def row(width):
        return pl.BlockSpec((tm, width), lambda i, j: (i, 0))

    def gate_w(br):
        if w_gate.ndim == 3:
            return pl.BlockSpec((None, d, tn), lambda i, j: (br * n_steps + j, 0, 0))
        return pl.BlockSpec((d, tn), lambda i, j: (0, br * n_steps + j))

    def gate_b(br):
        return pl.BlockSpec((1, tn), lambda i, j: (0, br * n_steps + j))

    def br_w(k):
        return pl.BlockSpec((k, tn), lambda i, j: (0, j))

    bg = b_gate.reshape(1, 3 * d)
    vmem = 2 * (tm * d * 4 + tm * d * 2 + tm * (ka + kb + kc) * 2) + 2 * tm * d * 4 \
        + 2 * (3 * d + ka + kb + kc + d) * tn * 2 + 7 * tm * tn * 4 + 3 * tm * n_chunk * 4 + 2 * MIB
    return pl.pallas_call(
        functools.partial(_merge_kernel, n_chunk=n_chunk),
        grid=(m // tm, n_steps),
        in_specs=[row(d), row(d), row(ka), row(kb), row(kc),
                  gate_w(0), gate_w(1), gate_w(2), gate_b(0), gate_b(1), gate_b(2),
                  br_w(ka), br_w(kb), br_w(kc),
                  pl.BlockSpec((tn, d), lambda i, j: (j, 0))],
        out_specs=pl.BlockSpec((tm, d), lambda i, j: (i, 0)),
        out_shape=jax.ShapeDtypeStruct((m, d), F32),
        compiler_params=_cparams(2, vmem),
        name="merge",
    )(x, h, ya, yb, yc, w_gate, w_gate, w_gate, bg, bg, bg, w_br_a, w_br_b, w_br_c, w_o)


def _rope_tables(s):
    t = jnp.arange(s)
    row_ids = (t // GRID_W).astype(F32)
    col_ids = (t % GRID_W).astype(F32)
    axis_dim = HEAD_DIM // 2
    inv_freq = ROPE_THETA ** (-jnp.arange(0, axis_dim, 2, dtype=F32) / axis_dim)
    ang_r = row_ids[:, None] * inv_freq
    ang_c = col_ids[:, None] * inv_freq
    cos = jnp.concatenate([jnp.cos(ang_r), jnp.cos(ang_r), jnp.cos(ang_c), jnp.cos(ang_c)], axis=1)
    sin = jnp.concatenate([-jnp.sin(ang_r), jnp.sin(ang_r), -jnp.sin(ang_c), jnp.sin(ang_c)], axis=1)
    return cos, sin


_EARLY_WEIGHTS = ("ffn1_w13", "ffn1_w2", "w_in")
_LATE_WEIGHTS = ("w_gate", "w_br_a", "w_br_b", "w_br_c", "w_o", "ffn2_w13", "ffn2_w2")
_COLUMN_BLOCKED = {"w_gate": 512, "ffn1_w13": 512, "ffn2_w13": 512}


_FFN1_ARMS = {(True, 0): dict(), (True, 1): dict(), (False, 0): dict(), (False, 1): dict(tm=1024)}
_FFN2_ARMS = {(True, 0): dict(), (True, 1): dict(), (False, 0): dict(), (False, 1): dict()}


def _trunk(x, p, wb, host_casts):
    b, s, d = x.shape
    depth = p["w_in"].shape[0]
    cos_tab, sin_tab = _rope_tables(s)
    rel_flat = p["rel_bias"].reshape(-1)
    xf = x.reshape(b * s, d)
    for li in range(depth):
        xf = _ffn(xf, p["ffn1_norm"][li], wb["ffn1_w13", li], wb["ffn1_w2", li], **_FFN1_ARMS[host_casts, li])
        h, za, zbq, zbkv, *zcs = _mix_in(xf, p["mix_norm"][li], wb["w_in", li], p["q_gain_b"][li],
                                        p["k_gain_b"][li], cos_tab, sin_tab)
        ya = _attn_a(za, rel_flat, p["sink_a"][li]).reshape(b * s, -1)
        keys = []
        if host_casts:
            keys = [(name, li) for name in _LATE_WEIGHTS]
            if li + 1 < depth:
                keys += [(name, li + 1) for name in _EARLY_WEIGHTS]
        yb, copies = _attn_b(zbq, zbkv, [(p[name], layer, _COLUMN_BLOCKED.get(name)) for name, layer in keys])
        wb.update(zip(keys, copies))
        yc = _attn_c(zcs, rel_flat)
        xf = _merge(xf, h, ya, yb.reshape(b * s, -1), yc, wb["w_gate", li], p["b_gate"][li], wb["w_br_a", li],
                    wb["w_br_b", li], wb["w_br_c", li], wb["w_o", li])
        fin = p["final_norm"] if li == depth - 1 else None
        xf = _ffn(xf, p["ffn2_norm"][li], wb["ffn2_w13", li], wb["ffn2_w2", li], fin, **_FFN2_ARMS[host_casts, li])
    return xf.reshape(b, s, d)


def kernel(x_prompt, x_sample, ffn1_norm, ffn1_w13, ffn1_w2, mix_norm, w_in, q_gain_b, k_gain_b, sink_a,
           w_gate, b_gate, w_br_a, w_br_b, w_br_c, w_o, ffn2_norm, ffn2_w13, ffn2_w2, rel_bias, final_norm):
    p = dict(ffn1_norm=ffn1_norm, ffn1_w13=ffn1_w13, ffn1_w2=ffn1_w2, mix_norm=mix_norm, w_in=w_in,
             q_gain_b=q_gain_b, k_gain_b=k_gain_b, sink_a=sink_a, w_gate=w_gate, b_gate=b_gate,
             w_br_a=w_br_a, w_br_b=w_br_b, w_br_c=w_br_c, w_o=w_o, ffn2_norm=ffn2_norm,
             ffn2_w13=ffn2_w13, ffn2_w2=ffn2_w2, rel_bias=rel_bias, final_norm=final_norm)
    wb = {(name, 0): p[name][0].astype(BF16) for name in _EARLY_WEIGHTS}
    y_prompt = _trunk(x_prompt, p, wb, host_casts=True)
    y_sample = _trunk(x_sample, p, wb, host_casts=False)
    return y_prompt, y_sample
```

```python
import functools
import math

import jax
import jax.numpy as jnp
from jax import lax
from jax.experimental import pallas as pl
from jax.experimental.pallas import tpu as pltpu

F32 = jnp.float32
BF16 = jnp.bfloat16

HEAD_DIM = 128
NORM_EPS = 1e-6
MASK_VALUE = -1e30
QK_SCALE = 1.0 / math.sqrt(HEAD_DIM)
LOG2E = math.log2(math.e)
FFN_RESIDUAL = 0.5
ROPE_THETA = 10000.0
GRID_W = 64
NUM_BUCKETS = 32
REL_MAX_DISTANCE = 2048

A_Q, A_K, A_V = 0, 4, 6
B_Q, B_K, B_V = 8, 14, 16
C_Q, C_K, C_V = 18, 24, 27
N_HEAD_BLOCKS = 30
A_HEADS = 4
A_HALF_WINDOW = 128
C_HALF_WINDOW = 64
C_DILATIONS = (1, 4, 16)
PACKED = 4 * HEAD_DIM
A_SUBTILE = 256
C_SUBTILE = 128
C_ROWS_PER_STEP = 2048
ROW_CHUNK = 256
BF16_SUBLANES = 16

V7X_VMEM_BYTES = 64 * 1024 * 1024
MIB = 1024 * 1024


def _cparams(n_grid, vmem_bytes):
    assert vmem_bytes < V7X_VMEM_BYTES, vmem_bytes
    return pltpu.CompilerParams(dimension_semantics=("arbitrary",) * n_grid,
                                vmem_limit_bytes=V7X_VMEM_BYTES)


def _rms(x, gain):
    ms = jnp.mean(x * x, axis=-1, keepdims=True)
    return x * lax.rsqrt(ms + NORM_EPS) * gain


def _lanes(slot):
    return slice(slot * HEAD_DIM, (slot + 1) * HEAD_DIM)


def _ffn_kernel(*refs, n_steps, n_chunk, final):
    if final:
        x_ref, g_ref, wg_ref, wu_ref, w2_ref, fin_ref, o_ref, xn_ref = refs
    else:
        x_ref, g_ref, wg_ref, wu_ref, w2_ref, o_ref, xn_ref = refs
    j = pl.program_id(1)
    row_chunks = [slice(r, r + ROW_CHUNK) for r in range(0, x_ref.shape[0], ROW_CHUNK)]

    @pl.when(j == 0)
    def _():
        for rows in row_chunks:
            x = x_ref[rows, :]
            xn_ref[rows, :] = _rms(x, g_ref[...]).astype(BF16)
            o_ref[rows, :] = x

    xn = xn_ref[...]
    gate = jnp.dot(xn, wg_ref[...], preferred_element_type=F32)
    up = jnp.dot(xn, wu_ref[...], preferred_element_type=F32)
    act = (gate * jax.nn.sigmoid(gate) * up * FFN_RESIDUAL).astype(BF16)
    for n in range(o_ref.shape[1] // n_chunk):
        cols = slice(n * n_chunk, (n + 1) * n_chunk)
        o_ref[:, cols] += jnp.dot(act, w2_ref[:, cols], preferred_element_type=F32)

    if final:
        @pl.when(j == n_steps - 1)
        def _():
            for rows in row_chunks:
                o_ref[rows, :] = _rms(o_ref[rows, :], fin_ref[...])


def _ffn(x, gain, w13, w2, final_gain=None, *, tm=512, tf=512, n_chunk=512):
    m, d = x.shape
    d_ff = w2.shape[0]
    n_chunk = min(n_chunk, d)
    assert m % tm == 0 and d_ff % tf == 0 and d % n_chunk == 0 and tm % ROW_CHUNK == 0
    n_steps = d_ff // tf
    final = final_gain is not None
    in_specs = [
        pl.BlockSpec((tm, d), lambda i, j: (i, 0)),
        pl.BlockSpec((1, d), lambda i, j: (0, 0)),
        pl.BlockSpec((d, tf), lambda i, j: (0, j)),
        pl.BlockSpec((d, tf), lambda i, j: (0, j + n_steps)),
        pl.BlockSpec((tf, d), lambda i, j: (j, 0)),
    ]
    args = [x, gain.reshape(1, d), w13, w13, w2]
    if final:
        in_specs.append(pl.BlockSpec((1, d), lambda i, j: (0, 0)))
        args.append(final_gain.reshape(1, d))
    vmem = 2 * tm * d * 4 + 2 * tm * d * 4 + tm * d * 2 + 2 * (3 * d * tf * 2) \
        + 3 * tm * tf * 4 + tm * tf * 2 + 3 * tm * n_chunk * 4 + 3 * ROW_CHUNK * d * 4 + 2 * MIB
    return pl.pallas_call(
        functools.partial(_ffn_kernel, n_steps=n_steps, n_chunk=n_chunk, final=final),
        grid=(m // tm, n_steps),
        in_specs=in_specs,
        out_specs=pl.BlockSpec((tm, d), lambda i, j: (i, 0)),
        out_shape=jax.ShapeDtypeStruct((m, d), F32),
        scratch_shapes=[pltpu.VMEM((tm, d), BF16)],
        compiler_params=_cparams(2, vmem),
        name="ffn_final" if final else "ffn",
    )(*args)


def _swap_quarter_pairs(x):
    q = HEAD_DIM // 4
    lane = lax.broadcasted_iota(jnp.int32, x.shape, 1)
    from_right = pltpu.roll(x, HEAD_DIM - q, axis=1)
    from_left = pltpu.roll(x, q, axis=1)
    return jnp.where((lane % (2 * q)) < q, from_right, from_left)


def _mixin_kernel(x_ref, g_ref, w_ref, qg_ref, kg_ref, cos_ref, sin_ref,
                  h_ref, za_ref, zbq_ref, zbkv_ref, zc0_ref, zc1_ref, zc2_ref, cs_ref):
    tm = x_ref.shape[0]
    h = _rms(x_ref[...], g_ref[...]).astype(BF16)
    h_ref[...] = h
    cos = cos_ref[...]
    sin = sin_ref[...]

    def norm_rope(v, gain):
        ms = jnp.mean(v * v, axis=-1, keepdims=True)
        y = v * lax.rsqrt(ms + NORM_EPS) * gain
        return y * cos + _swap_quarter_pairs(y) * sin

    for c in range(N_HEAD_BLOCKS // 2):
        zc = jnp.dot(h, w_ref[:, c * 256:(c + 1) * 256], preferred_element_type=F32)
        for hh in range(2):
            blk = 2 * c + hh
            v = zc[:, _lanes(hh)]
            if blk < A_K:
                za_ref[0, blk // 2, :, _lanes(blk % 2)] = (v * QK_SCALE).astype(BF16)
            elif blk < A_V:
                za_ref[0, blk - A_K, :, _lanes(2)] = v.astype(BF16)
            elif blk < B_Q:
                za_ref[0, blk - A_V, :, _lanes(3)] = v.astype(BF16)
            elif blk < B_K:
                q = norm_rope(v, qg_ref[...]) * (QK_SCALE * LOG2E)
                zbq_ref[0, (blk - B_Q) // 3, :, _lanes((blk - B_Q) % 3)] = q.astype(BF16)
            elif blk < B_V:
                zbkv_ref[0, blk - B_K, :, _lanes(0)] = norm_rope(v, kg_ref[...]).astype(BF16)
            elif blk < C_Q:
                zbkv_ref[0, blk - B_V, :, _lanes(1)] = v.astype(BF16)
            elif blk < C_K:
                cs_ref[blk - C_Q] = v * QK_SCALE
            else:
                cs_ref[blk - C_Q] = v
    for kv in range(zbkv_ref.shape[1]):
        zbkv_ref[0, kv, :, _lanes(2)] = jnp.ones((tm, HEAD_DIM), BF16)

    n_pairs = len(C_DILATIONS)
    for pair, (r, zc_ref) in enumerate(zip(C_DILATIONS, (zc0_ref, zc1_ref, zc2_ref))):
        staged = (2 * pair, 2 * pair + 1, 2 * n_pairs + pair, 3 * n_pairs + pair)
        for slot, src in enumerate(staged):
            for c in range(r):
                rows = pl.ds(c, tm // r, stride=r) if r > 1 else slice(None)
                zc_ref[0, c, :, _lanes(slot)] = cs_ref[src, rows, :].astype(BF16)


def _mix_in(x, gain, w_in, q_gain, k_gain, cos_tab, sin_tab, *, tm=512):
    m, d = x.shape
    s = cos_tab.shape[0]
    b = m // s
    n_cols = w_in.shape[1]
    assert m % tm == 0 and s % tm == 0 and n_cols == N_HEAD_BLOCKS * HEAD_DIM
    assert all(tm % (16 * r) == 0 for r in C_DILATIONS)
    pos_tiles = s // tm

    def packed(n, width, rows):
        return pl.BlockSpec((1, n, rows, width), lambda i: (i // pos_tiles, 0, i % pos_tiles, 0))

    out_specs = [pl.BlockSpec((tm, d), lambda i: (i, 0)),
                 packed(2, PACKED, tm), packed(2, 3 * HEAD_DIM, tm), packed(2, 3 * HEAD_DIM, tm)]
    out_shape = [jax.ShapeDtypeStruct((m, d), BF16),
                 jax.ShapeDtypeStruct((b, 2, s, PACKED), BF16),
                 jax.ShapeDtypeStruct((b, 2, s, 3 * HEAD_DIM), BF16),
                 jax.ShapeDtypeStruct((b, 2, s, 3 * HEAD_DIM), BF16)]
    for r in C_DILATIONS:
        out_specs.append(packed(r, PACKED, tm // r))
        out_shape.append(jax.ShapeDtypeStruct((b, r, s // r, PACKED), BF16))
    c_cols = (N_HEAD_BLOCKS - C_Q) * HEAD_DIM
    vmem = 2 * tm * d * 4 + 2 * tm * d * 2 + 2 * tm * (n_cols + 2 * HEAD_DIM) * 2 + d * n_cols * 2 \
        + 4 * tm * HEAD_DIM * 4 + tm * c_cols * 4 + tm * d * 4 + 4 * tm * 256 * 4 \
        + 8 * tm * HEAD_DIM * 4 + 2 * MIB
    return pl.pallas_call(
        _mixin_kernel,
        grid=(m // tm,),
        in_specs=[
            pl.BlockSpec((tm, d), lambda i: (i, 0)),
            pl.BlockSpec((1, d), lambda i: (0, 0)),
            pl.BlockSpec((d, n_cols), lambda i: (0, 0), pipeline_mode=pl.Buffered(1)),
            pl.BlockSpec((1, HEAD_DIM), lambda i: (0, 0)),
            pl.BlockSpec((1, HEAD_DIM), lambda i: (0, 0)),
            pl.BlockSpec((tm, HEAD_DIM), lambda i: (i % pos_tiles, 0)),
            pl.BlockSpec((tm, HEAD_DIM), lambda i: (i % pos_tiles, 0)),
        ],
        out_specs=out_specs,
        out_shape=out_shape,
        scratch_shapes=[pltpu.VMEM((N_HEAD_BLOCKS - C_Q, tm, HEAD_DIM), F32)],
        compiler_params=_cparams(1, vmem),
        name="mix_in",
    )(x, gain.reshape(1, d), w_in, q_gain.reshape(1, HEAD_DIM), k_gain.reshape(1, HEAD_DIM),
      cos_tab, sin_tab)


def _t5_bucket(rel):
    half = NUM_BUCKETS // 2
    max_exact = half // 2
    ret = jnp.where(rel > 0, half, 0)
    n = jnp.abs(rel)
    nf = jnp.maximum(n, 1).astype(jnp.float32)
    large = max_exact + (jnp.log(nf / max_exact) / math.log(REL_MAX_DISTANCE / max_exact)
                         * (half - max_exact)).astype(jnp.int32)
    large = jnp.minimum(large, half - 1)
    return ret + jnp.where(n < max_exact, n, large)


def _band_rel(ts, halo):
    return jnp.arange(ts + 2 * halo)[None, :] - halo - jnp.arange(ts)[:, None]


def _band_kernel(*refs, halo, ts, seq_len, n_heads, head0, n_table_cols, with_sink):
    if with_sink:
        bkt_ref, tab_ref, sink_ref, left_ref, main_ref, right_ref, o_ref, bias_ref = refs
        lse_ref = None
    else:
        bkt_ref, tab_ref, left_ref, main_ref, right_ref, o_ref, lse_ref, bias_ref = refs
        sink_ref = None
    n_sub, tq = main_ref.shape[1], main_ref.shape[2]
    nk = ts + 2 * halo
    first = (pl.program_id(0) == 0) & (pl.program_id(1) == 0) & (pl.program_id(2) == 0)

    @pl.when(first)
    def _():
        bkt = bkt_ref[...]
        rel = (lax.broadcasted_iota(jnp.int32, (ts, nk), 1) - halo
               - lax.broadcasted_iota(jnp.int32, (ts, nk), 0))
        in_band = jnp.abs(rel) <= halo
        tiles = [jnp.zeros((ts, nk), F32) for _ in range(n_heads)]
        for t in range(NUM_BUCKETS):
            hit = bkt == t
            for h in range(n_heads):
                tiles[h] = jnp.where(hit, tab_ref[t * n_table_cols + head0 + h], tiles[h])
        for h in range(n_heads):
            bias_ref[h] = jnp.where(in_band, tiles[h], MASK_VALUE)

    qi = pl.program_id(2)
    for sub in range(n_sub):
        def keys(slot):
            return jnp.concatenate([left_ref[0, sub, :, _lanes(slot)], main_ref[0, sub, :, _lanes(slot)],
                                    right_ref[0, sub, :, _lanes(slot)]], axis=0)

        k = keys(2)
        v = keys(3)
        for t in range(tq // ts):
            rows = slice(t * ts, (t + 1) * ts)
            kt = k[t * ts:t * ts + nk]
            vt = v[t * ts:t * ts + nk]
            key_pos = qi * tq + t * ts - halo + lax.broadcasted_iota(jnp.int32, (1, nk), 1)
            key_ok = (key_pos >= 0) & (key_pos < seq_len)
            for g in range(2):
                head = (pl.program_id(1) * n_sub + sub) * 2 + g if with_sink else g
                logits = lax.dot_general(main_ref[0, sub, rows, _lanes(g)], kt, (((1,), (1,)), ((), ())),
                                         preferred_element_type=F32)
                logits = jnp.where(key_ok, logits + bias_ref[head], MASK_VALUE)
                m = jnp.max(logits, axis=-1, keepdims=True)
                if with_sink:
                    sink = sink_ref[head]
                    m = jnp.maximum(m, sink)
                p = jnp.exp(logits - m)
                denom = jnp.sum(p, axis=-1, keepdims=True)
                if with_sink:
                    denom = denom + jnp.exp(sink - m)
                out = jnp.dot(p.astype(BF16), vt, preferred_element_type=F32) / denom
                if with_sink:
                    o_ref[0, rows, _lanes(2 * sub + g)] = out.astype(o_ref.dtype)
                else:
                    o_ref[0, sub, rows, _lanes(g)] = out
                    lse_ref[0, sub, rows, _lanes(g)] = jnp.broadcast_to(m + jnp.log(denom), (ts, HEAD_DIM))


def _band_specs(tq, halo, n_sub, n_halo_blocks):
    per = tq // halo
    return [
        pl.BlockSpec((1, n_sub, halo, PACKED), lambda b, c, i: (b, c, jnp.maximum(i * per - 1, 0), 0)),
        pl.BlockSpec((1, n_sub, tq, PACKED), lambda b, c, i: (b, c, i, 0)),
        pl.BlockSpec((1, n_sub, halo, PACKED),
                     lambda b, c, i: (b, c, jnp.minimum((i + 1) * per, n_halo_blocks - 1), 0)),
    ]


def _band_vmem(tq, ts, halo, n_sub, n_heads, out_bytes):
    nk = ts + 2 * halo
    return (n_heads * ts * nk * 4 + 2 * ts * nk * 4 + 2 * n_sub * (tq + 2 * halo) * PACKED * 2 + 2 * out_bytes
            + 10 * ts * nk * 4 + 4 * (tq + 2 * halo) * HEAD_DIM * 2 + 4 * MIB)


def _attn_a(za, rel_bias_flat, sink, *, tq=512, n_sub=2):
    b, n_kv, s, _ = za.shape
    halo = A_HALF_WINDOW
    ts = A_SUBTILE
    assert s % tq == 0 and tq % halo == 0 and n_kv % n_sub == 0 and tq % ts == 0
    nk = ts + 2 * halo
    bkt = _t5_bucket(_band_rel(ts, halo)).astype(jnp.int32)
    smem = pl.BlockSpec(memory_space=pltpu.SMEM)
    out_w = n_sub * 2 * HEAD_DIM
    return pl.pallas_call(
        functools.partial(_band_kernel, halo=halo, ts=ts, seq_len=s, n_heads=A_HEADS, head0=0,
                          n_table_cols=rel_bias_flat.shape[0] // NUM_BUCKETS, with_sink=True),
        grid=(b, n_kv // n_sub, s // tq),
        in_specs=[pl.BlockSpec((ts, nk), lambda b_, c, i: (0, 0)), smem, smem]
        + _band_specs(tq, halo, n_sub, s // halo),
        out_specs=pl.BlockSpec((1, tq, out_w), lambda b_, c, i: (b_, i, c)),
        out_shape=jax.ShapeDtypeStruct((b, s, A_HEADS * HEAD_DIM), BF16),
        scratch_shapes=[pltpu.VMEM((A_HEADS, ts, nk), F32)],
        compiler_params=_cparams(3, _band_vmem(tq, ts, halo, n_sub, A_HEADS, tq * out_w * 2)),
        name="attn_a",
    )(bkt, rel_bias_flat, sink, za, za, za)


def _attn_c_pair(zc, rel_bias_flat, pair):
    b, r, sub, _ = zc.shape
    halo = C_HALF_WINDOW
    n_sub = min(r, C_ROWS_PER_STEP // C_SUBTILE)
    tq = min(sub, C_ROWS_PER_STEP // n_sub)
    ts = min(C_SUBTILE, tq)
    assert sub % tq == 0 and tq % halo == 0 and r % n_sub == 0 and tq % ts == 0
    nk = ts + 2 * halo
    bkt = _t5_bucket(_band_rel(ts, halo) * r).astype(jnp.int32)
    smem = pl.BlockSpec(memory_space=pltpu.SMEM)
    out_spec = pl.BlockSpec((1, n_sub, tq, 2 * HEAD_DIM), lambda b_, c, i: (b_, c, i, 0))
    return pl.pallas_call(
        functools.partial(_band_kernel, halo=halo, ts=ts, seq_len=sub, n_heads=2, head0=A_HEADS + 2 * pair,
                          n_table_cols=rel_bias_flat.shape[0] // NUM_BUCKETS, with_sink=False),
        grid=(b, r // n_sub, sub // tq),
        in_specs=[pl.BlockSpec((ts, nk), lambda b_, c, i: (0, 0)), smem]
        + _band_specs(tq, halo, n_sub, sub // halo),
        out_specs=[out_spec, out_spec],
        out_shape=[jax.ShapeDtypeStruct((b, r, sub, 2 * HEAD_DIM), F32)] * 2,
        scratch_shapes=[pltpu.VMEM((2, ts, nk), F32)],
        compiler_params=_cparams(3, _band_vmem(tq, ts, halo, n_sub, 2, 2 * n_sub * tq * 256 * 4)),
        name=f"attn_c{pair}",
    )(bkt, rel_bias_flat, zc, zc, zc)


def _c_merge_kernel(*refs):
    n = len(C_DILATIONS)
    o_refs, l_refs, y_ref = refs[:n], refs[n:2 * n], refs[2 * n]
    buf_ref = refs[2 * n + 1]
    tm = y_ref.shape[0]
    slots = iter(range(buf_ref.shape[0]))

    def token_order(ref, r, lanes):
        if r == 1:
            return ref[0, 0, :, lanes]
        slot = next(slots)
        for c in range(r):
            buf_ref[slot, pl.ds(c, tm // r, stride=r), :] = ref[0, c, :, lanes]
        return buf_ref[slot]

    for g in range(y_ref.shape[1] // HEAD_DIM):
        lanes = _lanes(g)
        outs = [token_order(ref, r, lanes) for ref, r in zip(o_refs, C_DILATIONS)]
        lses = [token_order(ref, r, lanes) for ref, r in zip(l_refs, C_DILATIONS)]
        m = functools.reduce(jnp.maximum, lses)
        es = [jnp.exp(l - m) for l in lses]
        tot = functools.reduce(jnp.add, es)
        y = functools.reduce(jnp.add, [(e / tot) * o for e, o in zip(es, outs)])
        y_ref[:, lanes] = y.astype(y_ref.dtype)


def _attn_c(zcs, rel_bias_flat, *, tm=512):
    outs, lses = zip(*[_attn_c_pair(zc, rel_bias_flat, p) for p, zc in enumerate(zcs)])
    b, _, s, w = outs[0].shape
    m = b * s
    assert s % tm == 0
    pos_tiles = s // tm

    def spec(r):
        return pl.BlockSpec((1, r, tm // r, w), lambda i: (i // pos_tiles, 0, i % pos_tiles, 0))

    n_slots = 2 * (w // HEAD_DIM) * sum(r > 1 for r in C_DILATIONS)
    return pl.pallas_call(
        _c_merge_kernel,
        grid=(m // tm,),
        in_specs=[spec(r) for r in C_DILATIONS] * 2,
        out_specs=pl.BlockSpec((tm, w), lambda i: (i, 0)),
        out_shape=jax.ShapeDtypeStruct((m, w), BF16),
        scratch_shapes=[pltpu.VMEM((n_slots, tm, HEAD_DIM), F32)],
        compiler_params=_cparams(1, 2 * 6 * tm * w * 4 + (n_slots + 16) * tm * HEAD_DIM * 4 + 2 * MIB),
        name="attn_c_merge",
    )(*outs, *lses)


def _attn_b_kernel(*refs, tk, n_casts):
    q_ref, kv_ref = refs[:2]
    cast_in = refs[2:2 + n_casts]
    o_ref = refs[2 + n_casts]
    cast_out = refs[3 + n_casts:]
    tq = q_ref.shape[2]
    s = kv_ref.shape[2]
    q = jnp.concatenate([q_ref[0, 0, :, _lanes(g)] for g in range(3)], axis=0)
    m = acc = None
    for c in range(s // tk):
        rows = slice(c * tk, (c + 1) * tk)
        logits = lax.dot_general(q, kv_ref[0, 0, rows, _lanes(0)], (((1,), (1,)), ((), ())),
                                 preferred_element_type=F32)
        cmax = jnp.max(logits, axis=-1, keepdims=True)
        m_new = cmax if c == 0 else jnp.maximum(m, cmax)
        p = jnp.exp2(logits - m_new)
        pv = jnp.dot(p.astype(BF16), kv_ref[0, 0, rows, HEAD_DIM:], preferred_element_type=F32)
        acc = pv if c == 0 else jnp.exp2(m - m_new) * acc + pv
        m = m_new
    out = acc[:, _lanes(0)] / acc[:, HEAD_DIM:HEAD_DIM + 1]
    for g in range(3):
        o_ref[0, :, _lanes(g)] = out[g * tq:(g + 1) * tq].astype(o_ref.dtype)
    for src, dst in zip(cast_in, cast_out):
        dst[...] = src[...].astype(BF16)


def _cast_blocks(n_rows, n_steps):
    n_blocks = n_steps
    while n_blocks > 1 and (n_rows % n_blocks or (n_rows // n_blocks) % BF16_SUBLANES):
        n_blocks //= 2
    assert n_steps % n_blocks == 0 and n_rows % n_blocks == 0
    return n_blocks


def _attn_b(zbq, zbkv, casts=(), *, tq=256, tk=256):
    b, n_kv, s, _ = zbq.shape
    assert s % tq == 0 and s % tk == 0
    n_qt = s // tq
    n_steps = b * n_kv * n_qt
    in_specs = [pl.BlockSpec((1, 1, tq, 3 * HEAD_DIM), lambda b_, c, i: (b_, c, i, 0)),
                pl.BlockSpec((1, 1, s, 3 * HEAD_DIM), lambda b_, c, i: (b_, c, 0, 0))]
    out_specs = [pl.BlockSpec((1, tq, 3 * HEAD_DIM), lambda b_, c, i: (b_, i, c))]
    out_shape = [jax.ShapeDtypeStruct((b, s, n_kv * 3 * HEAD_DIM), BF16)]
    vmem = 2 * tq * 384 * 2 + 2 * s * 384 * 2 + 2 * tq * 384 * 2 \
        + 8 * 3 * tq * tk * 4 + 12 * 3 * tq * HEAD_DIM * 4 + 2 * MIB
    for w, layer in casts:
        _, n_rows, n_cols = w.shape
        n_blocks = _cast_blocks(n_rows, n_steps)
        rep = n_steps // n_blocks
        rows = n_rows // n_blocks

        def block(b_, c, i, rep=rep):
            return ((b_ * n_kv + c) * n_qt + i) // rep

        in_specs.append(pl.BlockSpec((None, rows, n_cols),
                                     lambda b_, c, i, layer=layer, block=block: (layer, block(b_, c, i), 0)))
        out_specs.append(pl.BlockSpec((rows, n_cols), lambda b_, c, i, block=block: (block(b_, c, i), 0)))
        out_shape.append(jax.ShapeDtypeStruct((n_rows, n_cols), BF16))
        vmem += 2 * rows * n_cols * (4 + 2) + rows * n_cols * 4
    yb, *cast_out = pl.pallas_call(
        functools.partial(_attn_b_kernel, tk=tk, n_casts=len(casts)),
        grid=(b, n_kv, n_qt),
        in_specs=in_specs,
        out_specs=out_specs,
        out_shape=out_shape,
        compiler_params=_cparams(3, vmem),
        name="attn_b_cast" if casts else "attn_b",
    )(zbq, zbkv, *[w for w, _ in casts])
    return yb, cast_out


def _merge_kernel(x_ref, h_ref, ya_ref, yb_ref, yc_ref, wga_ref, wgb_ref, wgc_ref,
                  bga_ref, bgb_ref, bgc_ref, wa_ref, wb_ref, wc_ref, wo_ref, o_ref, *, n_chunk):
    h = h_ref[...]

    def branch(y_ref, w_ref, wg_ref, bg_ref):
        gate = jax.nn.sigmoid(jnp.dot(h, wg_ref[...], preferred_element_type=F32) + bg_ref[...])
        return gate * jnp.dot(y_ref[...], w_ref[...], preferred_element_type=F32)

    @pl.when(pl.program_id(1) == 0)
    def _():
        o_ref[...] = x_ref[...]

    merged = (branch(ya_ref, wa_ref, wga_ref, bga_ref) + branch(yb_ref, wb_ref, wgb_ref, bgb_ref)
              + branch(yc_ref, wc_ref, wgc_ref, bgc_ref)).astype(BF16)
    for n in range(o_ref.shape[1] // n_chunk):
        cols = slice(n * n_chunk, (n + 1) * n_chunk)
        o_ref[:, cols] += jnp.dot(merged, wo_ref[:, cols], preferred_element_type=F32)


def _merge(x, h, ya, yb, yc, w_gate, b_gate, w_br_a, w_br_b, w_br_c, w_o, *, tm=512, tn=512, n_chunk=512):
    m, d = x.shape
    tn = min(tn, d)
    n_chunk = min(n_chunk, d)
    assert m % tm == 0 and d % tn == 0 and d % n_chunk == 0
    n_steps = d // tn
    ka, kb, kc = ya.shape[1], yb.shape[1], yc.shape[1]

    def row(width):
        return pl.BlockSpec((tm, width), lambda i, j: (i, 0))

    def gate_w(br):
        return pl.BlockSpec((d, tn), lambda i, j: (0, br * n_steps + j))

    def gate_b(br):
        return pl.BlockSpec((1, tn), lambda i, j: (0, br * n_steps + j))

    def br_w(k):
        return pl.BlockSpec((k, tn), lambda i, j: (0, j))

    bg = b_gate.reshape(1, 3 * d)
    vmem = 2 * (tm * d * 4 + tm * d * 2 + tm * (ka + kb + kc) * 2) + 2 * tm * d * 4 \
        + 2 * (3 * d + ka + kb + kc + d) * tn * 2 + 7 * tm * tn * 4 + 3 * tm * n_chunk * 4 + 2 * MIB
    return pl.pallas_call(
        functools.partial(_merge_kernel, n_chunk=n_chunk),
        grid=(m // tm, n_steps),
        in_specs=[row(d), row(d), row(ka), row(kb), row(kc),
                  gate_w(0), gate_w(1), gate_w(2), gate_b(0), gate_b(1), gate_b(2),
                  br_w(ka), br_w(kb), br_w(kc),
                  pl.BlockSpec((tn, d), lambda i, j: (j, 0))],
        out_specs=pl.BlockSpec((tm, d), lambda i, j: (i, 0)),
        out_shape=jax.ShapeDtypeStruct((m, d), F32),
        compiler_params=_cparams(2, vmem),
        name="merge",
    )(x, h, ya, yb, yc, w_gate, w_gate, w_gate, bg, bg, bg, w_br_a, w_br_b, w_br_c, w_o)


def _rope_tables(s):
    t = jnp.arange(s)
    row_ids = (t // GRID_W).astype(F32)
    col_ids = (t % GRID_W).astype(F32)
    axis_dim = HEAD_DIM // 2
    inv_freq = ROPE_THETA ** (-jnp.arange(0, axis_dim, 2, dtype=F32) / axis_dim)
    ang_r = row_ids[:, None] * inv_freq
    ang_c = col_ids[:, None] * inv_freq
    cos = jnp.concatenate([jnp.cos(ang_r), jnp.cos(ang_r), jnp.cos(ang_c), jnp.cos(ang_c)], axis=1)
    sin = jnp.concatenate([-jnp.sin(ang_r), jnp.sin(ang_r), -jnp.sin(ang_c), jnp.sin(ang_c)], axis=1)
    return cos, sin


_EARLY_WEIGHTS = ("ffn1_w13", "ffn1_w2", "w_in")
_LATE_WEIGHTS = ("w_gate", "w_br_a", "w_br_b", "w_br_c", "w_o", "ffn2_w13", "ffn2_w2")


def _trunk(x, p, wb, host_casts):
    b, s, d = x.shape
    depth = p["w_in"].shape[0]
    cos_tab, sin_tab = _rope_tables(s)
    rel_flat = p["rel_bias"].reshape(-1)
    xf = x.reshape(b * s, d)
    for li in range(depth):
        xf = _ffn(xf, p["ffn1_norm"][li], wb["ffn1_w13", li], wb["ffn1_w2", li])
        h, za, zbq, zbkv, *zcs = _mix_in(xf, p["mix_norm"][li], wb["w_in", li], p["q_gain_b"][li],
                                        p["k_gain_b"][li], cos_tab, sin_tab)
        ya = _attn_a(za, rel_flat, p["sink_a"][li]).reshape(b * s, -1)
        keys = []
        if host_casts:
            keys = [(name, li) for name in _LATE_WEIGHTS]
            if li + 1 < depth:
                keys += [(name, li + 1) for name in _EARLY_WEIGHTS]
        yb, copies = _attn_b(zbq, zbkv, [(p[name], layer) for name, layer in keys])
        wb.update(zip(keys, copies))
        yc = _attn_c(zcs, rel_flat)
        xf = _merge(xf, h, ya, yb.reshape(b * s, -1), yc, wb["w_gate", li], p["b_gate"][li], wb["w_br_a", li],
                    wb["w_br_b", li], wb["w_br_c", li], wb["w_o", li])
        fin = p["final_norm"] if li == depth - 1 else None
        xf = _ffn(xf, p["ffn2_norm"][li], wb["ffn2_w13", li], wb["ffn2_w2", li], fin)
    return xf.reshape(b, s, d)


def kernel(x_prompt, x_sample, ffn1_norm, ffn1_w13, ffn1_w2, mix_norm, w_in, q_gain_b, k_gain_b, sink_a,
           w_gate, b_gate, w_br_a, w_br_b, w_br_c, w_o, ffn2_norm, ffn2_w13, ffn2_w2, rel_bias, final_norm):
    p = dict(ffn1_norm=ffn1_norm, ffn1_w13=ffn1_w13, ffn1_w2=ffn1_w2, mix_norm=mix_norm, w_in=w_in,
             q_gain_b=q_gain_b, k_gain_b=k_gain_b, sink_a=sink_a, w_gate=w_gate, b_gate=b_gate,
             w_br_a=w_br_a, w_br_b=w_br_b, w_br_c=w_br_c, w_o=w_o, ffn2_norm=ffn2_norm,
             ffn2_w13=ffn2_w13, ffn2_w2=ffn2_w2, rel_bias=rel_bias, final_norm=final_norm)
    wb = {(name, 0): p[name][0].astype(BF16) for name in _EARLY_WEIGHTS}
    y_sample = _trunk(x_sample, p, wb, host_casts=True)
    y_prompt = _trunk(x_prompt, p, wb, host_casts=False)
    return y_prompt, y_sample
```

```python
import functools
import math

import jax
import jax.numpy as jnp
from jax import lax
from jax.experimental import pallas as pl
from jax.experimental.pallas import tpu as pltpu

F32 = jnp.float32
BF16 = jnp.bfloat16

HEAD_DIM = 128
NORM_EPS = 1e-6
MASK_VALUE = -1e30
QK_SCALE = 1.0 / math.sqrt(HEAD_DIM)
LOG2E = math.log2(math.e)
FFN_RESIDUAL = 0.5
ROPE_THETA = 10000.0
GRID_W = 64
NUM_BUCKETS = 32
REL_MAX_DISTANCE = 2048

A_Q, A_K, A_V = 0, 4, 6
B_Q, B_K, B_V = 8, 14, 16
C_Q, C_K, C_V = 18, 24, 27
N_HEAD_BLOCKS = 30
A_HEADS = 4
A_HALF_WINDOW = 128
C_HALF_WINDOW = 64
C_DILATIONS = (1, 4, 16)
PACKED = 4 * HEAD_DIM
A_SUBTILE = 256
C_SUBTILE = 128
C_ROWS_PER_STEP = 2048
ROW_CHUNK = 256
BF16_SUBLANES = 16

V7X_VMEM_BYTES = 64 * 1024 * 1024
MIB = 1024 * 1024


def _cparams(n_grid, vmem_bytes):
    assert vmem_bytes < V7X_VMEM_BYTES, vmem_bytes
    return pltpu.CompilerParams(dimension_semantics=("arbitrary",) * n_grid,
                                vmem_limit_bytes=V7X_VMEM_BYTES)


def _rms(x, gain):
    ms = jnp.mean(x * x, axis=-1, keepdims=True)
    return x * lax.rsqrt(ms + NORM_EPS) * gain


def _lanes(slot):
    return slice(slot * HEAD_DIM, (slot + 1) * HEAD_DIM)


def _ffn_kernel(*refs, n_steps, n_chunk, final):
    if final:
        x_ref, g_ref, wg_ref, wu_ref, w2_ref, fin_ref, o_ref, xn_ref = refs
    else:
        x_ref, g_ref, wg_ref, wu_ref, w2_ref, o_ref, xn_ref = refs
    j = pl.program_id(1)
    row_chunks = [slice(r, r + ROW_CHUNK) for r in range(0, x_ref.shape[0], ROW_CHUNK)]

    @pl.when(j == 0)
    def _():
        for rows in row_chunks:
            x = x_ref[rows, :]
            xn_ref[rows, :] = _rms(x, g_ref[...]).astype(BF16)
            o_ref[rows, :] = x

    xn = xn_ref[...]
    gate = jnp.dot(xn, wg_ref[...], preferred_element_type=F32)
    up = jnp.dot(xn, wu_ref[...], preferred_element_type=F32)
    act = (gate * jax.nn.sigmoid(gate) * up * FFN_RESIDUAL).astype(BF16)
    for n in range(o_ref.shape[1] // n_chunk):
        cols = slice(n * n_chunk, (n + 1) * n_chunk)
        o_ref[:, cols] += jnp.dot(act, w2_ref[:, cols], preferred_element_type=F32)

    if final:
        @pl.when(j == n_steps - 1)
        def _():
            for rows in row_chunks:
                o_ref[rows, :] = _rms(o_ref[rows, :], fin_ref[...])


def _ffn(x, gain, w13, w2, final_gain=None, *, tm=512, tf=512, n_chunk=512):
    m, d = x.shape
    d_ff = w2.shape[0]
    n_chunk = min(n_chunk, d)
    assert m % tm == 0 and d_ff % tf == 0 and d % n_chunk == 0 and tm % ROW_CHUNK == 0
    n_steps = d_ff // tf
    final = final_gain is not None
    in_specs = [
        pl.BlockSpec((tm, d), lambda i, j: (i, 0)),
        pl.BlockSpec((1, d), lambda i, j: (0, 0)),
        pl.BlockSpec((d, tf), lambda i, j: (0, j)),
        pl.BlockSpec((d, tf), lambda i, j: (0, j + n_steps)),
        pl.BlockSpec((tf, d), lambda i, j: (j, 0)),
    ]
    args = [x, gain.reshape(1, d), w13, w13, w2]
    if final:
        in_specs.append(pl.BlockSpec((1, d), lambda i, j: (0, 0)))
        args.append(final_gain.reshape(1, d))
    vmem = 2 * tm * d * 4 + 2 * tm * d * 4 + tm * d * 2 + 2 * (3 * d * tf * 2) \
        + 3 * tm * tf * 4 + tm * tf * 2 + 3 * tm * n_chunk * 4 + 3 * ROW_CHUNK * d * 4 + 2 * MIB
    return pl.pallas_call(
        functools.partial(_ffn_kernel, n_steps=n_steps, n_chunk=n_chunk, final=final),
        grid=(m // tm, n_steps),
        in_specs=in_specs,
        out_specs=pl.BlockSpec((tm, d), lambda i, j: (i, 0)),
        out_shape=jax.ShapeDtypeStruct((m, d), F32),
        scratch_shapes=[pltpu.VMEM((tm, d), BF16)],
        compiler_params=_cparams(2, vmem),
        name="ffn_final" if final else "ffn",
    )(*args)


def _swap_quarter_pairs(x):
    q = HEAD_DIM // 4
    lane = lax.broadcasted_iota(jnp.int32, x.shape, 1)
    from_right = pltpu.roll(x, HEAD_DIM - q, axis=1)
    from_left = pltpu.roll(x, q, axis=1)
    return jnp.where((lane % (2 * q)) < q, from_right, from_left)


def _mixin_kernel(x_ref, g_ref, w_ref, qg_ref, kg_ref, cos_ref, sin_ref,
                  h_ref, za_ref, zbq_ref, zbkv_ref, zc0_ref, zc1_ref, zc2_ref, cs_ref):
    tm = x_ref.shape[0]
    h = _rms(x_ref[...], g_ref[...]).astype(BF16)
    h_ref[...] = h
    cos = cos_ref[...]
    sin = sin_ref[...]

    def norm_rope(v, gain):
        ms = jnp.mean(v * v, axis=-1, keepdims=True)
        y = v * lax.rsqrt(ms + NORM_EPS) * gain
        return y * cos + _swap_quarter_pairs(y) * sin

    for c in range(N_HEAD_BLOCKS // 2):
        zc = jnp.dot(h, w_ref[:, c * 256:(c + 1) * 256], preferred_element_type=F32)
        for hh in range(2):
            blk = 2 * c + hh
            v = zc[:, _lanes(hh)]
            if blk < A_K:
                za_ref[0, blk // 2, :, _lanes(blk % 2)] = (v * QK_SCALE).astype(BF16)
            elif blk < A_V:
                za_ref[0, blk - A_K, :, _lanes(2)] = v.astype(BF16)
            elif blk < B_Q:
                za_ref[0, blk - A_V, :, _lanes(3)] = v.astype(BF16)
            elif blk < B_K:
                q = norm_rope(v, qg_ref[...]) * (QK_SCALE * LOG2E)
                zbq_ref[0, (blk - B_Q) // 3, :, _lanes((blk - B_Q) % 3)] = q.astype(BF16)
            elif blk < B_V:
                zbkv_ref[0, blk - B_K, :, _lanes(0)] = norm_rope(v, kg_ref[...]).astype(BF16)
            elif blk < C_Q:
                zbkv_ref[0, blk - B_V, :, _lanes(1)] = v.astype(BF16)
            elif blk < C_K:
                cs_ref[blk - C_Q] = v * QK_SCALE
            else:
                cs_ref[blk - C_Q] = v
    for kv in range(zbkv_ref.shape[1]):
        zbkv_ref[0, kv, :, _lanes(2)] = jnp.ones((tm, HEAD_DIM), BF16)

    n_pairs = len(C_DILATIONS)
    for pair, (r, zc_ref) in enumerate(zip(C_DILATIONS, (zc0_ref, zc1_ref, zc2_ref))):
        staged = (2 * pair, 2 * pair + 1, 2 * n_pairs + pair, 3 * n_pairs + pair)
        for slot, src in enumerate(staged):
            for c in range(r):
                rows = pl.ds(c, tm // r, stride=r) if r > 1 else slice(None)
                zc_ref[0, c, :, _lanes(slot)] = cs_ref[src, rows, :].astype(BF16)


def _mix_in(x, gain, w_in, q_gain, k_gain, cos_tab, sin_tab, *, tm=512):
    m, d = x.shape
    s = cos_tab.shape[0]
    b = m // s
    n_cols = w_in.shape[1]
    assert m % tm == 0 and s % tm == 0 and n_cols == N_HEAD_BLOCKS * HEAD_DIM
    assert all(tm % (16 * r) == 0 for r in C_DILATIONS)
    pos_tiles = s // tm

    def packed(n, width, rows):
        return pl.BlockSpec((1, n, rows, width), lambda i: (i // pos_tiles, 0, i % pos_tiles, 0))

    out_specs = [pl.BlockSpec((tm, d), lambda i: (i, 0)),
                 packed(2, PACKED, tm), packed(2, 3 * HEAD_DIM, tm), packed(2, 3 * HEAD_DIM, tm)]
    out_shape = [jax.ShapeDtypeStruct((m, d), BF16),
                 jax.ShapeDtypeStruct((b, 2, s, PACKED), BF16),
                 jax.ShapeDtypeStruct((b, 2, s, 3 * HEAD_DIM), BF16),
                 jax.ShapeDtypeStruct((b, 2, s, 3 * HEAD_DIM), BF16)]
    for r in C_DILATIONS:
        out_specs.append(packed(r, PACKED, tm // r))
        out_shape.append(jax.ShapeDtypeStruct((b, r, s // r, PACKED), BF16))
    c_cols = (N_HEAD_BLOCKS - C_Q) * HEAD_DIM
    vmem = 2 * tm * d * 4 + 2 * tm * d * 2 + 2 * tm * (n_cols + 2 * HEAD_DIM) * 2 + d * n_cols * 2 \
        + 4 * tm * HEAD_DIM * 4 + tm * c_cols * 4 + tm * d * 4 + 4 * tm * 256 * 4 \
        + 8 * tm * HEAD_DIM * 4 + 2 * MIB
    return pl.pallas_call(
        _mixin_kernel,
        grid=(m // tm,),
        in_specs=[
            pl.BlockSpec((tm, d), lambda i: (i, 0)),
            pl.BlockSpec((1, d), lambda i: (0, 0)),
            pl.BlockSpec((d, n_cols), lambda i: (0, 0), pipeline_mode=pl.Buffered(1)),
            pl.BlockSpec((1, HEAD_DIM), lambda i: (0, 0)),
            pl.BlockSpec((1, HEAD_DIM), lambda i: (0, 0)),
            pl.BlockSpec((tm, HEAD_DIM), lambda i: (i % pos_tiles, 0)),
            pl.BlockSpec((tm, HEAD_DIM), lambda i: (i % pos_tiles, 0)),
        ],
        out_specs=out_specs,
        out_shape=out_shape,
        scratch_shapes=[pltpu.VMEM((N_HEAD_BLOCKS - C_Q, tm, HEAD_DIM), F32)],
        compiler_params=_cparams(1, vmem),
        name="mix_in",
    )(x, gain.reshape(1, d), w_in, q_gain.reshape(1, HEAD_DIM), k_gain.reshape(1, HEAD_DIM),
      cos_tab, sin_tab)


def _t5_bucket(rel):
    half = NUM_BUCKETS // 2
    max_exact = half // 2
    ret = jnp.where(rel > 0, half, 0)
    n = jnp.abs(rel)
    nf = jnp.maximum(n, 1).astype(jnp.float32)
    large = max_exact + (jnp.log(nf / max_exact) / math.log(REL_MAX_DISTANCE / max_exact)
                         * (half - max_exact)).astype(jnp.int32)
    large = jnp.minimum(large, half - 1)
    return ret + jnp.where(n < max_exact, n, large)


def _band_rel(ts, halo):
    return jnp.arange(ts + 2 * halo)[None, :] - halo - jnp.arange(ts)[:, None]


def _band_kernel(*refs, halo, ts, seq_len, n_heads, head0, n_table_cols, with_sink):
    if with_sink:
        bkt_ref, tab_ref, sink_ref, left_ref, main_ref, right_ref, o_ref, bias_ref = refs
        lse_ref = None
    else:
        bkt_ref, tab_ref, left_ref, main_ref, right_ref, o_ref, lse_ref, bias_ref = refs
        sink_ref = None
    n_sub, tq = main_ref.shape[1], main_ref.shape[2]
    nk = ts + 2 * halo
    first = (pl.program_id(0) == 0) & (pl.program_id(1) == 0) & (pl.program_id(2) == 0)

    @pl.when(first)
    def _():
        bkt = bkt_ref[...]
        rel = (lax.broadcasted_iota(jnp.int32, (ts, nk), 1) - halo
               - lax.broadcasted_iota(jnp.int32, (ts, nk), 0))
        in_band = jnp.abs(rel) <= halo
        tiles = [jnp.zeros((ts, nk), F32) for _ in range(n_heads)]
        for t in range(NUM_BUCKETS):
            hit = bkt == t
            for h in range(n_heads):
                tiles[h] = jnp.where(hit, tab_ref[t * n_table_cols + head0 + h], tiles[h])
        for h in range(n_heads):
            bias_ref[h] = jnp.where(in_band, tiles[h], MASK_VALUE)

    qi = pl.program_id(2)
    for sub in range(n_sub):
        def keys(slot):
            return jnp.concatenate([left_ref[0, sub, :, _lanes(slot)], main_ref[0, sub, :, _lanes(slot)],
                                    right_ref[0, sub, :, _lanes(slot)]], axis=0)

        k = keys(2)
        v = keys(3)
        for t in range(tq // ts):
            rows = slice(t * ts, (t + 1) * ts)
            kt = k[t * ts:t * ts + nk]
            vt = v[t * ts:t * ts + nk]
            key_pos = qi * tq + t * ts - halo + lax.broadcasted_iota(jnp.int32, (1, nk), 1)
            key_ok = (key_pos >= 0) & (key_pos < seq_len)
            lses = []
            for g in range(2):
                head = (pl.program_id(1) * n_sub + sub) * 2 + g if with_sink else g
                logits = lax.dot_general(main_ref[0, sub, rows, _lanes(g)], kt, (((1,), (1,)), ((), ())),
                                         preferred_element_type=F32)
                logits = jnp.where(key_ok, logits + bias_ref[head], MASK_VALUE)
                m = jnp.max(logits, axis=-1, keepdims=True)
                if with_sink:
                    sink = sink_ref[head]
                    m = jnp.maximum(m, sink)
                p = jnp.exp(logits - m)
                denom = jnp.sum(p, axis=-1, keepdims=True)
                if with_sink:
                    denom = denom + jnp.exp(sink - m)
                out = jnp.dot(p.astype(BF16), vt, preferred_element_type=F32) / denom
                if with_sink:
                    o_ref[0, rows, _lanes(2 * sub + g)] = out.astype(o_ref.dtype)
                else:
                    o_ref[0, sub, rows, _lanes(g)] = out.astype(o_ref.dtype)
                    lses.append(m + jnp.log(denom))
            if not with_sink:
                lane = lax.broadcasted_iota(jnp.int32, (ts, HEAD_DIM), 1)
                lse_ref[0, sub, rows, :] = jnp.where(lane < HEAD_DIM // 2, lses[0], lses[1])


def _band_specs(tq, halo, n_sub, n_halo_blocks):
    per = tq // halo
    return [
        pl.BlockSpec((1, n_sub, halo, PACKED), lambda b, c, i: (b, c, jnp.maximum(i * per - 1, 0), 0)),
        pl.BlockSpec((1, n_sub, tq, PACKED), lambda b, c, i: (b, c, i, 0)),
        pl.BlockSpec((1, n_sub, halo, PACKED),
                     lambda b, c, i: (b, c, jnp.minimum((i + 1) * per, n_halo_blocks - 1), 0)),
    ]


def _band_vmem(tq, ts, halo, n_sub, n_heads, out_bytes):
    nk = ts + 2 * halo
    return (n_heads * ts * nk * 4 + 2 * ts * nk * 4 + 2 * n_sub * (tq + 2 * halo) * PACKED * 2 + 2 * out_bytes
            + 10 * ts * nk * 4 + 4 * (tq + 2 * halo) * HEAD_DIM * 2 + 4 * MIB)


def _attn_a(za, rel_bias_flat, sink, *, tq=512, n_sub=2):
    b, n_kv, s, _ = za.shape
    halo = A_HALF_WINDOW
    ts = A_SUBTILE
    assert s % tq == 0 and tq % halo == 0 and n_kv % n_sub == 0 and tq % ts == 0
    nk = ts + 2 * halo
    bkt = _t5_bucket(_band_rel(ts, halo)).astype(jnp.int32)
    smem = pl.BlockSpec(memory_space=pltpu.SMEM)
    out_w = n_sub * 2 * HEAD_DIM
    return pl.pallas_call(
        functools.partial(_band_kernel, halo=halo, ts=ts, seq_len=s, n_heads=A_HEADS, head0=0,
                          n_table_cols=rel_bias_flat.shape[0] // NUM_BUCKETS, with_sink=True),
        grid=(b, n_kv // n_sub, s // tq),
        in_specs=[pl.BlockSpec((ts, nk), lambda b_, c, i: (0, 0)), smem, smem]
        + _band_specs(tq, halo, n_sub, s // halo),
        out_specs=pl.BlockSpec((1, tq, out_w), lambda b_, c, i: (b_, i, c)),
        out_shape=jax.ShapeDtypeStruct((b, s, A_HEADS * HEAD_DIM), BF16),
        scratch_shapes=[pltpu.VMEM((A_HEADS, ts, nk), F32)],
        compiler_params=_cparams(3, _band_vmem(tq, ts, halo, n_sub, A_HEADS, tq * out_w * 2)),
        name="attn_a",
    )(bkt, rel_bias_flat, sink, za, za, za)


def _attn_c_pair(zc, rel_bias_flat, pair):
    b, r, sub, _ = zc.shape
    halo = C_HALF_WINDOW
    n_sub = min(r, C_ROWS_PER_STEP // C_SUBTILE)
    tq = min(sub, C_ROWS_PER_STEP // n_sub)
    ts = min(C_SUBTILE, tq)
    assert sub % tq == 0 and tq % halo == 0 and r % n_sub == 0 and tq % ts == 0
    nk = ts + 2 * halo
    bkt = _t5_bucket(_band_rel(ts, halo) * r).astype(jnp.int32)
    smem = pl.BlockSpec(memory_space=pltpu.SMEM)
    def out_spec(width):
        return pl.BlockSpec((1, n_sub, tq, width), lambda b_, c, i: (b_, c, i, 0))

    return pl.pallas_call(
        functools.partial(_band_kernel, halo=halo, ts=ts, seq_len=sub, n_heads=2, head0=A_HEADS + 2 * pair,
                          n_table_cols=rel_bias_flat.shape[0] // NUM_BUCKETS, with_sink=False),
        grid=(b, r // n_sub, sub // tq),
        in_specs=[pl.BlockSpec((ts, nk), lambda b_, c, i: (0, 0)), smem]
        + _band_specs(tq, halo, n_sub, sub // halo),
        out_specs=[out_spec(2 * HEAD_DIM), out_spec(HEAD_DIM)],
        out_shape=[jax.ShapeDtypeStruct((b, r, sub, 2 * HEAD_DIM), BF16),
                   jax.ShapeDtypeStruct((b, r, sub, HEAD_DIM), F32)],
        scratch_shapes=[pltpu.VMEM((2, ts, nk), F32)],
        compiler_params=_cparams(3, _band_vmem(tq, ts, halo, n_sub, 2, 2 * n_sub * tq * HEAD_DIM * 4)),
        name=f"attn_c{pair}",
    )(bkt, rel_bias_flat, zc, zc, zc)


def _c_merge_kernel(*refs):
    n = len(C_DILATIONS)
    o_refs, l_refs, y_ref = refs[:n], refs[n:2 * n], refs[2 * n]
    buf_ref = refs[2 * n + 1]
    tm = y_ref.shape[0]
    slots = iter(range(buf_ref.shape[0]))

    def token_order(ref, r, lanes):
        if r == 1:
            return ref[0, 0, :, lanes].astype(F32)
        slot = next(slots)
        for c in range(r):
            buf_ref[slot, pl.ds(c, tm // r, stride=r), :] = ref[0, c, :, lanes].astype(F32)
        return buf_ref[slot]

    packed_lses = [token_order(ref, r, _lanes(0)) for ref, r in zip(l_refs, C_DILATIONS)]
    half = HEAD_DIM // 2
    for g in range(y_ref.shape[1] // HEAD_DIM):
        lanes = _lanes(g)
        outs = [token_order(ref, r, lanes) for ref, r in zip(o_refs, C_DILATIONS)]
        lses = [jnp.broadcast_to(l[:, g * half:g * half + 1], (tm, HEAD_DIM)) for l in packed_lses]
        m = functools.reduce(jnp.maximum, lses)
        es = [jnp.exp(l - m) for l in lses]
        tot = functools.reduce(jnp.add, es)
        y = functools.reduce(jnp.add, [(e / tot) * o for e, o in zip(es, outs)])
        y_ref[:, lanes] = y.astype(y_ref.dtype)


def _attn_c(zcs, rel_bias_flat, *, tm=512):
    outs, lses = zip(*[_attn_c_pair(zc, rel_bias_flat, p) for p, zc in enumerate(zcs)])
    b, _, s, w = outs[0].shape
    m = b * s
    assert s % tm == 0
    pos_tiles = s // tm

    def spec(r, width):
        return pl.BlockSpec((1, r, tm // r, width), lambda i: (i // pos_tiles, 0, i % pos_tiles, 0))

    n_slots = (w // HEAD_DIM + 1) * sum(r > 1 for r in C_DILATIONS)
    return pl.pallas_call(
        _c_merge_kernel,
        grid=(m // tm,),
        in_specs=[spec(r, w) for r in C_DILATIONS] + [spec(r, HEAD_DIM) for r in C_DILATIONS],
        out_specs=pl.BlockSpec((tm, w), lambda i: (i, 0)),
        out_shape=jax.ShapeDtypeStruct((m, w), BF16),
        scratch_shapes=[pltpu.VMEM((n_slots, tm, HEAD_DIM), F32)],
        compiler_params=_cparams(1, 2 * 6 * tm * w * 4 + (n_slots + 16) * tm * HEAD_DIM * 4 + 2 * MIB),
        name="attn_c_merge",
    )(*outs, *lses)


def _attn_b_kernel(*refs, tk, n_casts):
    q_ref, kv_ref = refs[:2]
    cast_in = refs[2:2 + n_casts]
    o_ref = refs[2 + n_casts]
    cast_out = refs[3 + n_casts:]
    tq = q_ref.shape[2]
    s = kv_ref.shape[2]
    q = jnp.concatenate([q_ref[0, 0, :, _lanes(g)] for g in range(3)], axis=0)
    m = acc = None
    for c in range(s // tk):
        rows = slice(c * tk, (c + 1) * tk)
        logits = lax.dot_general(q, kv_ref[0, 0, rows, _lanes(0)], (((1,), (1,)), ((), ())),
                                 preferred_element_type=F32)
        cmax = jnp.max(logits, axis=-1, keepdims=True)
        m_new = cmax if c == 0 else jnp.maximum(m, cmax)
        p = jnp.exp2(logits - m_new)
        pv = jnp.dot(p.astype(BF16), kv_ref[0, 0, rows, HEAD_DIM:], preferred_element_type=F32)
        acc = pv if c == 0 else jnp.exp2(m - m_new) * acc + pv
        m = m_new
    out = acc[:, _lanes(0)] / acc[:, HEAD_DIM:HEAD_DIM + 1]
    for g in range(3):
        o_ref[0, :, _lanes(g)] = out[g * tq:(g + 1) * tq].astype(o_ref.dtype)
    for src, dst in zip(cast_in, cast_out):
        dst[...] = src[...].astype(BF16)


def _cast_blocks(n_rows, n_steps):
    n_blocks = n_steps
    while n_blocks > 1 and (n_rows % n_blocks or (n_rows // n_blocks) % BF16_SUBLANES):
        n_blocks //= 2
    assert n_steps % n_blocks == 0 and n_rows % n_blocks == 0
    return n_blocks


def _attn_b(zbq, zbkv, casts=(), *, tq=256, tk=256):
    b, n_kv, s, _ = zbq.shape
    assert s % tq == 0 and s % tk == 0
    n_qt = s // tq
    n_steps = b * n_kv * n_qt
    in_specs = [pl.BlockSpec((1, 1, tq, 3 * HEAD_DIM), lambda b_, c, i: (b_, c, i, 0)),
                pl.BlockSpec((1, 1, s, 3 * HEAD_DIM), lambda b_, c, i: (b_, c, 0, 0))]
    out_specs = [pl.BlockSpec((1, tq, 3 * HEAD_DIM), lambda b_, c, i: (b_, i, c))]
    out_shape = [jax.ShapeDtypeStruct((b, s, n_kv * 3 * HEAD_DIM), BF16)]
    vmem = 2 * tq * 384 * 2 + 2 * s * 384 * 2 + 2 * tq * 384 * 2 \
        + 8 * 3 * tq * tk * 4 + 12 * 3 * tq * HEAD_DIM * 4 + 2 * MIB
    for w, layer in casts:
        _, n_rows, n_cols = w.shape
        n_blocks = _cast_blocks(n_rows, n_steps)
        rep = n_steps // n_blocks
        rows = n_rows // n_blocks

        def block(b_, c, i, rep=rep):
            return ((b_ * n_kv + c) * n_qt + i) // rep

        in_specs.append(pl.BlockSpec((None, rows, n_cols),
                                     lambda b_, c, i, layer=layer, block=block: (layer, block(b_, c, i), 0)))
        out_specs.append(pl.BlockSpec((rows, n_cols), lambda b_, c, i, block=block: (block(b_, c, i), 0)))
        out_shape.append(jax.ShapeDtypeStruct((n_rows, n_cols), BF16))
        vmem += 2 * rows * n_cols * (4 + 2) + rows * n_cols * 4
    yb, *cast_out = pl.pallas_call(
        functools.partial(_attn_b_kernel, tk=tk, n_casts=len(casts)),
        grid=(b, n_kv, n_qt),
        in_specs=in_specs,
        out_specs=out_specs,
        out_shape=out_shape,
        compiler_params=_cparams(3, vmem),
        name="attn_b_cast" if casts else "attn_b",
    )(zbq, zbkv, *[w for w, _ in casts])
    return yb, cast_out


def _merge_kernel(x_ref, h_ref, ya_ref, yb_ref, yc_ref, wga_ref, wgb_ref, wgc_ref,
                  bga_ref, bgb_ref, bgc_ref, wa_ref, wb_ref, wc_ref, wo_ref, o_ref, *, n_chunk):
    h = h_ref[...]

    def branch(y_ref, w_ref, wg_ref, bg_ref):
        gate = jax.nn.sigmoid(jnp.dot(h, wg_ref[...], preferred_element_type=F32) + bg_ref[...])
        return gate * jnp.dot(y_ref[...], w_ref[...], preferred_element_type=F32)

    @pl.when(pl.program_id(1) == 0)
    def _():
        o_ref[...] = x_ref[...]

    merged = (branch(ya_ref, wa_ref, wga_ref, bga_ref) + branch(yb_ref, wb_ref, wgb_ref, bgb_ref)
              + branch(yc_ref, wc_ref, wgc_ref, bgc_ref)).astype(BF16)
    for n in range(o_ref.shape[1] // n_chunk):
        cols = slice(n * n_chunk, (n + 1) * n_chunk)
        o_ref[:, cols] += jnp.dot(merged, wo_ref[:, cols], preferred_element_type=F32)


def _merge(x, h, ya, yb, yc, w_gate, b_gate, w_br_a, w_br_b, w_br_c, w_o, *, tm=512, tn=512, n_chunk=512):
    m, d = x.shape
    tn = min(tn, d)
    n_chunk = min(n_chunk, d)
    assert m % tm == 0 and d % tn == 0 and d % n_chunk == 0
    n_steps = d // tn
    ka, kb, kc = ya.shape[1], yb.shape[1], yc.shape[1]

    def row(width):
        return pl.BlockSpec((tm, width), lambda i, j: (i, 0))

    def gate_w(br):
        return pl.BlockSpec((d, tn), lambda i, j: (0, br * n_steps + j))

    def gate_b(br):
        return pl.BlockSpec((1, tn), lambda i, j: (0, br * n_steps + j))

    def br_w(k):
        return pl.BlockSpec((k, tn), lambda i, j: (0, j))

    bg = b_gate.reshape(1, 3 * d)
    vmem = 2 * (tm * d * 4 + tm * d * 2 + tm * (ka + kb + kc) * 2) + 2 * tm * d * 4 \
        + 2 * (3 * d + ka + kb + kc + d) * tn * 2 + 7 * tm * tn * 4 + 3 * tm * n_chunk * 4 + 2 * MIB
    return pl.pallas_call(
        functools.partial(_merge_kernel, n_chunk=n_chunk),
        grid=(m // tm, n_steps),
        in_specs=[row(d), row(d), row(ka), row(kb), row(kc),
                  gate_w(0), gate_w(1), gate_w(2), gate_b(0), gate_b(1), gate_b(2),
                  br_w(ka), br_w(kb), br_w(kc),
                  pl.BlockSpec((tn, d), lambda i, j: (j, 0))],
        out_specs=pl.BlockSpec((tm, d), lambda i, j: (i, 0)),
        out_shape=jax.ShapeDtypeStruct((m, d), F32),
        compiler_params=_cparams(2, vmem),
        name="merge",
    )(x, h, ya, yb, yc, w_gate, w_gate, w_gate, bg, bg, bg, w_br_a, w_br_b, w_br_c, w_o)


def _rope_tables(s):
    t = jnp.arange(s)
    row_ids = (t // GRID_W).astype(F32)
    col_ids = (t % GRID_W).astype(F32)
    axis_dim = HEAD_DIM // 2
    inv_freq = ROPE_THETA ** (-jnp.arange(0, axis_dim, 2, dtype=F32) / axis_dim)
    ang_r = row_ids[:, None] * inv_freq
    ang_c = col_ids[:, None] * inv_freq
    cos = jnp.concatenate([jnp.cos(ang_r), jnp.cos(ang_r), jnp.cos(ang_c), jnp.cos(ang_c)], axis=1)
    sin = jnp.concatenate([-jnp.sin(ang_r), jnp.sin(ang_r), -jnp.sin(ang_c), jnp.sin(ang_c)], axis=1)
    return cos, sin


_EARLY_WEIGHTS = ("ffn1_w13", "ffn1_w2", "w_in")
_LATE_WEIGHTS = ("w_gate", "w_br_a", "w_br_b", "w_br_c", "w_o", "ffn2_w13", "ffn2_w2")


def _trunk(x, p, wb, host_casts):
    b, s, d = x.shape
    depth = p["w_in"].shape[0]
    cos_tab, sin_tab = _rope_tables(s)
    rel_flat = p["rel_bias"].reshape(-1)
    xf = x.reshape(b * s, d)
    for li in range(depth):
        xf = _ffn(xf, p["ffn1_norm"][li], wb["ffn1_w13", li], wb["ffn1_w2", li])
        h, za, zbq, zbkv, *zcs = _mix_in(xf, p["mix_norm"][li], wb["w_in", li], p["q_gain_b"][li],
                                        p["k_gain_b"][li], cos_tab, sin_tab)
        ya = _attn_a(za, rel_flat, p["sink_a"][li]).reshape(b * s, -1)
        keys = []
        if host_casts:
            keys = [(name, li) for name in _LATE_WEIGHTS]
            if li + 1 < depth:
                keys += [(name, li + 1) for name in _EARLY_WEIGHTS]
        yb, copies = _attn_b(zbq, zbkv, [(p[name], layer) for name, layer in keys])
        wb.update(zip(keys, copies))
        yc = _attn_c(zcs, rel_flat)
        xf = _merge(xf, h, ya, yb.reshape(b * s, -1), yc, wb["w_gate", li], p["b_gate"][li], wb["w_br_a", li],
                    wb["w_br_b", li], wb["w_br_c", li], wb["w_o", li])
        fin = p["final_norm"] if li == depth - 1 else None
        xf = _ffn(xf, p["ffn2_norm"][li], wb["ffn2_w13", li], wb["ffn2_w2", li], fin)
    return xf.reshape(b, s, d)


def kernel(x_prompt, x_sample, ffn1_norm, ffn1_w13, ffn1_w2, mix_norm, w_in, q_gain_b, k_gain_b, sink_a,
           w_gate, b_gate, w_br_a, w_br_b, w_br_c, w_o, ffn2_norm, ffn2_w13, ffn2_w2, rel_bias, final_norm):
    p = dict(ffn1_norm=ffn1_norm, ffn1_w13=ffn1_w13, ffn1_w2=ffn1_w2, mix_norm=mix_norm, w_in=w_in,
             q_gain_b=q_gain_b, k_gain_b=k_gain_b, sink_a=sink_a, w_gate=w_gate, b_gate=b_gate,
             w_br_a=w_br_a, w_br_b=w_br_b, w_br_c=w_br_c, w_o=w_o, ffn2_norm=ffn2_norm,
             ffn2_w13=ffn2_w13, ffn2_w2=ffn2_w2, rel_bias=rel_bias, final_norm=final_norm)
    wb = {(name, 0): p[name][0].astype(BF16) for name in _EARLY_WEIGHTS}
    y_sample = _trunk(x_sample, p, wb, host_casts=True)
    y_prompt = _trunk(x_prompt, p, wb, host_casts=False)
    return y_prompt, y_sample
```

```python
import functools
import math

import jax
import jax.numpy as jnp
from jax import lax
from jax.experimental import pallas as pl
from jax.experimental.pallas import tpu as pltpu

F32 = jnp.float32
BF16 = jnp.bfloat16

HEAD_DIM = 128
NORM_EPS = 1e-6
MASK_VALUE = -1e30
QK_SCALE = 1.0 / math.sqrt(HEAD_DIM)
LOG2E = math.log2(math.e)
FFN_RESIDUAL = 0.5
ROPE_THETA = 10000.0
GRID_W = 64
NUM_BUCKETS = 32
REL_MAX_DISTANCE = 2048

A_Q, A_K, A_V = 0, 4, 6
B_Q, B_K, B_V = 8, 14, 16
C_Q, C_K, C_V = 18, 24, 27
N_HEAD_BLOCKS = 30
A_HEADS = 4
A_HALF_WINDOW = 128
C_HALF_WINDOW = 64
C_DILATIONS = (1, 4, 16)
PACKED = 4 * HEAD_DIM
A_SUBTILE = 256
C_SUBTILE = 128
C_ROWS_PER_STEP = 2048
ROW_CHUNK = 256
BF16_SUBLANES = 16

V7X_VMEM_BYTES = 64 * 1024 * 1024
MIB = 1024 * 1024


def _cparams(n_grid, vmem_bytes):
    assert vmem_bytes < V7X_VMEM_BYTES, vmem_bytes
    return pltpu.CompilerParams(dimension_semantics=("arbitrary",) * n_grid,
                                vmem_limit_bytes=V7X_VMEM_BYTES)


def _rms(x, gain):
    ms = jnp.mean(x * x, axis=-1, keepdims=True)
    return x * lax.rsqrt(ms + NORM_EPS) * gain


def _lanes(slot):
    return slice(slot * HEAD_DIM, (slot + 1) * HEAD_DIM)


def _ffn_kernel(*refs, n_steps, n_chunk, final):
    if final:
        x_ref, g_ref, wg_ref, wu_ref, w2_ref, fin_ref, o_ref, xn_ref = refs
    else:
        x_ref, g_ref, wg_ref, wu_ref, w2_ref, o_ref, xn_ref = refs
    j = pl.program_id(1)
    row_chunks = [slice(r, r + ROW_CHUNK) for r in range(0, x_ref.shape[0], ROW_CHUNK)]

    @pl.when(j == 0)
    def _():
        for rows in row_chunks:
            x = x_ref[rows, :]
            xn_ref[rows, :] = _rms(x, g_ref[...]).astype(BF16)
            o_ref[rows, :] = x

    xn = xn_ref[...]
    gate = jnp.dot(xn, wg_ref[...], preferred_element_type=F32)
    up = jnp.dot(xn, wu_ref[...], preferred_element_type=F32)
    act = (gate * jax.nn.sigmoid(gate) * up * FFN_RESIDUAL).astype(BF16)
    for n in range(o_ref.shape[1] // n_chunk):
        cols = slice(n * n_chunk, (n + 1) * n_chunk)
        o_ref[:, cols] += jnp.dot(act, w2_ref[:, cols], preferred_element_type=F32)

    if final:
        @pl.when(j == n_steps - 1)
        def _():
            for rows in row_chunks:
                o_ref[rows, :] = _rms(o_ref[rows, :], fin_ref[...])


def _ffn(x, gain, w13, w2, final_gain=None, *, tm=512, tf=512, n_chunk=512):
    m, d = x.shape
    d_ff = w2.shape[0]
    n_chunk = min(n_chunk, d)
    assert m % tm == 0 and d_ff % tf == 0 and d % n_chunk == 0 and tm % ROW_CHUNK == 0
    n_steps = d_ff // tf
    final = final_gain is not None
    in_specs = [
        pl.BlockSpec((tm, d), lambda i, j: (i, 0)),
        pl.BlockSpec((1, d), lambda i, j: (0, 0)),
        pl.BlockSpec((d, tf), lambda i, j: (0, j)),
        pl.BlockSpec((d, tf), lambda i, j: (0, j + n_steps)),
        pl.BlockSpec((tf, d), lambda i, j: (j, 0)),
    ]
    args = [x, gain.reshape(1, d), w13, w13, w2]
    if final:
        in_specs.append(pl.BlockSpec((1, d), lambda i, j: (0, 0)))
        args.append(final_gain.reshape(1, d))
    vmem = 2 * tm * d * 4 + 2 * tm * d * 4 + tm * d * 2 + 2 * (3 * d * tf * 2) \
        + 3 * tm * tf * 4 + tm * tf * 2 + 3 * tm * n_chunk * 4 + 3 * ROW_CHUNK * d * 4 + 2 * MIB
    return pl.pallas_call(
        functools.partial(_ffn_kernel, n_steps=n_steps, n_chunk=n_chunk, final=final),
        grid=(m // tm, n_steps),
        in_specs=in_specs,
        out_specs=pl.BlockSpec((tm, d), lambda i, j: (i, 0)),
        out_shape=jax.ShapeDtypeStruct((m, d), F32),
        scratch_shapes=[pltpu.VMEM((tm, d), BF16)],
        compiler_params=_cparams(2, vmem),
        name="ffn_final" if final else "ffn",
    )(*args)


def _swap_quarter_pairs(x):
    q = HEAD_DIM // 4
    lane = lax.broadcasted_iota(jnp.int32, x.shape, 1)
    from_right = pltpu.roll(x, HEAD_DIM - q, axis=1)
    from_left = pltpu.roll(x, q, axis=1)
    return jnp.where((lane % (2 * q)) < q, from_right, from_left)


def _mixin_kernel(x_ref, g_ref, w_ref, qg_ref, kg_ref, cos_ref, sin_ref,
                  h_ref, za_ref, zbq_ref, zbkv_ref, zc0_ref, zc1_ref, zc2_ref, cs_ref):
    tm = x_ref.shape[0]
    h = _rms(x_ref[...], g_ref[...]).astype(BF16)
    h_ref[...] = h
    cos = cos_ref[...]
    sin = sin_ref[...]

    def norm_rope(v, gain):
        ms = jnp.mean(v * v, axis=-1, keepdims=True)
        y = v * lax.rsqrt(ms + NORM_EPS) * gain
        return y * cos + _swap_quarter_pairs(y) * sin

    for c in range(N_HEAD_BLOCKS // 2):
        zc = jnp.dot(h, w_ref[:, c * 256:(c + 1) * 256], preferred_element_type=F32)
        for hh in range(2):
            blk = 2 * c + hh
            v = zc[:, _lanes(hh)]
            if blk < A_K:
                za_ref[0, blk // 2, :, _lanes(blk % 2)] = (v * QK_SCALE).astype(BF16)
            elif blk < A_V:
                za_ref[0, blk - A_K, :, _lanes(2)] = v.astype(BF16)
            elif blk < B_Q:
                za_ref[0, blk - A_V, :, _lanes(3)] = v.astype(BF16)
            elif blk < B_K:
                q = norm_rope(v, qg_ref[...]) * (QK_SCALE * LOG2E)
                zbq_ref[0, (blk - B_Q) // 3, :, _lanes((blk - B_Q) % 3)] = q.astype(BF16)
            elif blk < B_V:
                zbkv_ref[0, blk - B_K, :, _lanes(0)] = norm_rope(v, kg_ref[...]).astype(BF16)
            elif blk < C_Q:
                zbkv_ref[0, blk - B_V, :, _lanes(1)] = v.astype(BF16)
            elif blk < C_K:
                cs_ref[blk - C_Q] = v * QK_SCALE
            else:
                cs_ref[blk - C_Q] = v
    for kv in range(zbkv_ref.shape[1]):
        zbkv_ref[0, kv, :, _lanes(2)] = jnp.ones((tm, HEAD_DIM), BF16)

    n_pairs = len(C_DILATIONS)
    for pair, (r, zc_ref) in enumerate(zip(C_DILATIONS, (zc0_ref, zc1_ref, zc2_ref))):
        staged = (2 * pair, 2 * pair + 1, 2 * n_pairs + pair, 3 * n_pairs + pair)
        for slot, src in enumerate(staged):
            for c in range(r):
                rows = pl.ds(c, tm // r, stride=r) if r > 1 else slice(None)
                zc_ref[0, c, :, _lanes(slot)] = cs_ref[src, rows, :].astype(BF16)


def _mix_in(x, gain, w_in, q_gain, k_gain, cos_tab, sin_tab, *, tm=512):
    m, d = x.shape
    s = cos_tab.shape[0]
    b = m // s
    n_cols = w_in.shape[1]
    assert m % tm == 0 and s % tm == 0 and n_cols == N_HEAD_BLOCKS * HEAD_DIM
    assert all(tm % (16 * r) == 0 for r in C_DILATIONS)
    pos_tiles = s // tm

    def packed(n, width, rows):
        return pl.BlockSpec((1, n, rows, width), lambda i: (i // pos_tiles, 0, i % pos_tiles, 0))

    out_specs = [pl.BlockSpec((tm, d), lambda i: (i, 0)),
                 packed(2, PACKED, tm), packed(2, 3 * HEAD_DIM, tm), packed(2, 3 * HEAD_DIM, tm)]
    out_shape = [jax.ShapeDtypeStruct((m, d), BF16),
                 jax.ShapeDtypeStruct((b, 2, s, PACKED), BF16),
                 jax.ShapeDtypeStruct((b, 2, s, 3 * HEAD_DIM), BF16),
                 jax.ShapeDtypeStruct((b, 2, s, 3 * HEAD_DIM), BF16)]
    for r in C_DILATIONS:
        out_specs.append(packed(r, PACKED, tm // r))
        out_shape.append(jax.ShapeDtypeStruct((b, r, s // r, PACKED), BF16))
    c_cols = (N_HEAD_BLOCKS - C_Q) * HEAD_DIM
    vmem = 2 * tm * d * 4 + 2 * tm * d * 2 + 2 * tm * (n_cols + 2 * HEAD_DIM) * 2 + d * n_cols * 2 \
        + 4 * tm * HEAD_DIM * 4 + tm * c_cols * 4 + tm * d * 4 + 4 * tm * 256 * 4 \
        + 8 * tm * HEAD_DIM * 4 + 2 * MIB
    return pl.pallas_call(
        _mixin_kernel,
        grid=(m // tm,),
        in_specs=[
            pl.BlockSpec((tm, d), lambda i: (i, 0)),
            pl.BlockSpec((1, d), lambda i: (0, 0)),
            pl.BlockSpec((d, n_cols), lambda i: (0, 0), pipeline_mode=pl.Buffered(1)),
            pl.BlockSpec((1, HEAD_DIM), lambda i: (0, 0)),
            pl.BlockSpec((1, HEAD_DIM), lambda i: (0, 0)),
            pl.BlockSpec((tm, HEAD_DIM), lambda i: (i % pos_tiles, 0)),
            pl.BlockSpec((tm, HEAD_DIM), lambda i: (i % pos_tiles, 0)),
        ],
        out_specs=out_specs,
        out_shape=out_shape,
        scratch_shapes=[pltpu.VMEM((N_HEAD_BLOCKS - C_Q, tm, HEAD_DIM), F32)],
        compiler_params=_cparams(1, vmem),
        name="mix_in",
    )(x, gain.reshape(1, d), w_in, q_gain.reshape(1, HEAD_DIM), k_gain.reshape(1, HEAD_DIM),
      cos_tab, sin_tab)


def _t5_bucket(rel):
    half = NUM_BUCKETS // 2
    max_exact = half // 2
    ret = jnp.where(rel > 0, half, 0)
    n = jnp.abs(rel)
    nf = jnp.maximum(n, 1).astype(jnp.float32)
    large = max_exact + (jnp.log(nf / max_exact) / math.log(REL_MAX_DISTANCE / max_exact)
                         * (half - max_exact)).astype(jnp.int32)
    large = jnp.minimum(large, half - 1)
    return ret + jnp.where(n < max_exact, n, large)


def _band_rel(ts, halo):
    return jnp.arange(ts + 2 * halo)[None, :] - halo - jnp.arange(ts)[:, None]


def _band_kernel(*refs, halo, ts, seq_len, n_heads, head0, n_table_cols, with_sink):
    if with_sink:
        bkt_ref, tab_ref, sink_ref, left_ref, main_ref, right_ref, o_ref, bias_ref = refs
        lse_ref = None
    else:
        bkt_ref, tab_ref, left_ref, main_ref, right_ref, o_ref, lse_ref, bias_ref = refs
        sink_ref = None
    n_sub, tq = main_ref.shape[1], main_ref.shape[2]
    nk = ts + 2 * halo
    first = (pl.program_id(0) == 0) & (pl.program_id(1) == 0) & (pl.program_id(2) == 0)

    @pl.when(first)
    def _():
        bkt = bkt_ref[...]
        rel = (lax.broadcasted_iota(jnp.int32, (ts, nk), 1) - halo
               - lax.broadcasted_iota(jnp.int32, (ts, nk), 0))
        in_band = jnp.abs(rel) <= halo
        tiles = [jnp.zeros((ts, nk), F32) for _ in range(n_heads)]
        for t in range(NUM_BUCKETS):
            hit = bkt == t
            for h in range(n_heads):
                tiles[h] = jnp.where(hit, tab_ref[t * n_table_cols + head0 + h], tiles[h])
        for h in range(n_heads):
            bias_ref[h] = jnp.where(in_band, tiles[h], MASK_VALUE)

    qi = pl.program_id(2)
    for sub in range(n_sub):
        def keys(slot):
            return jnp.concatenate([left_ref[0, sub, :, _lanes(slot)], main_ref[0, sub, :, _lanes(slot)],
                                    right_ref[0, sub, :, _lanes(slot)]], axis=0)

        k = keys(2)
        v = keys(3)
        for t in range(tq // ts):
            rows = slice(t * ts, (t + 1) * ts)
            kt = k[t * ts:t * ts + nk]
            vt = v[t * ts:t * ts + nk]
            key_pos = qi * tq + t * ts - halo + lax.broadcasted_iota(jnp.int32, (1, nk), 1)
            key_ok = (key_pos >= 0) & (key_pos < seq_len)
            lses = []
            for g in range(2):
                head = (pl.program_id(1) * n_sub + sub) * 2 + g if with_sink else g
                logits = lax.dot_general(main_ref[0, sub, rows, _lanes(g)], kt, (((1,), (1,)), ((), ())),
                                         preferred_element_type=F32)
                logits = jnp.where(key_ok, logits + bias_ref[head], MASK_VALUE)
                m = jnp.max(logits, axis=-1, keepdims=True)
                if with_sink:
                    sink = sink_ref[head]
                    m = jnp.maximum(m, sink)
                p = jnp.exp(logits - m)
                denom = jnp.sum(p, axis=-1, keepdims=True)
                if with_sink:
                    denom = denom + jnp.exp(sink - m)
                out = jnp.dot(p.astype(BF16), vt, preferred_element_type=F32) / denom
                if with_sink:
                    o_ref[0, rows, _lanes(2 * sub + g)] = out.astype(o_ref.dtype)
                else:
                    o_ref[0, sub, rows, _lanes(g)] = out.astype(o_ref.dtype)
                    lses.append(m + jnp.log(denom))
            if not with_sink:
                lane = lax.broadcasted_iota(jnp.int32, (ts, HEAD_DIM), 1)
                lse_ref[0, sub, rows, :] = jnp.where(lane < HEAD_DIM // 2, lses[0], lses[1])


def _band_specs(tq, halo, n_sub, n_halo_blocks):
    per = tq // halo
    return [
        pl.BlockSpec((1, n_sub, halo, PACKED), lambda b, c, i: (b, c, jnp.maximum(i * per - 1, 0), 0)),
        pl.BlockSpec((1, n_sub, tq, PACKED), lambda b, c, i: (b, c, i, 0)),
        pl.BlockSpec((1, n_sub, halo, PACKED),
                     lambda b, c, i: (b, c, jnp.minimum((i + 1) * per, n_halo_blocks - 1), 0)),
    ]


def _band_vmem(tq, ts, halo, n_sub, n_heads, out_bytes):
    nk = ts + 2 * halo
    return (n_heads * ts * nk * 4 + 2 * ts * nk * 4 + 2 * n_sub * (tq + 2 * halo) * PACKED * 2 + 2 * out_bytes
            + 10 * ts * nk * 4 + 4 * (tq + 2 * halo) * HEAD_DIM * 2 + 4 * MIB)


def _attn_a(za, rel_bias_flat, sink, *, tq=1024, n_sub=2):
    b, n_kv, s, _ = za.shape
    halo = A_HALF_WINDOW
    ts = A_SUBTILE
    assert s % tq == 0 and tq % halo == 0 and n_kv % n_sub == 0 and tq % ts == 0
    nk = ts + 2 * halo
    bkt = _t5_bucket(_band_rel(ts, halo)).astype(jnp.int32)
    smem = pl.BlockSpec(memory_space=pltpu.SMEM)
    out_w = n_sub * 2 * HEAD_DIM
    return pl.pallas_call(
        functools.partial(_band_kernel, halo=halo, ts=ts, seq_len=s, n_heads=A_HEADS, head0=0,
                          n_table_cols=rel_bias_flat.shape[0] // NUM_BUCKETS, with_sink=True),
        grid=(b, n_kv // n_sub, s // tq),
        in_specs=[pl.BlockSpec((ts, nk), lambda b_, c, i: (0, 0)), smem, smem]
        + _band_specs(tq, halo, n_sub, s // halo),
        out_specs=pl.BlockSpec((1, tq, out_w), lambda b_, c, i: (b_, i, c)),
        out_shape=jax.ShapeDtypeStruct((b, s, A_HEADS * HEAD_DIM), BF16),
        scratch_shapes=[pltpu.VMEM((A_HEADS, ts, nk), F32)],
        compiler_params=_cparams(3, _band_vmem(tq, ts, halo, n_sub, A_HEADS, tq * out_w * 2)),
        name="attn_a",
    )(bkt, rel_bias_flat, sink, za, za, za)


def _attn_c_pair(zc, rel_bias_flat, pair):
    b, r, sub, _ = zc.shape
    halo = C_HALF_WINDOW
    n_sub = min(r, C_ROWS_PER_STEP // C_SUBTILE)
    tq = min(sub, C_ROWS_PER_STEP // n_sub)
    ts = min(C_SUBTILE, tq)
    assert sub % tq == 0 and tq % halo == 0 and r % n_sub == 0 and tq % ts == 0
    nk = ts + 2 * halo
    bkt = _t5_bucket(_band_rel(ts, halo) * r).astype(jnp.int32)
    smem = pl.BlockSpec(memory_space=pltpu.SMEM)
    def out_spec(width):
        return pl.BlockSpec((1, n_sub, tq, width), lambda b_, c, i: (b_, c, i, 0))

    return pl.pallas_call(
        functools.partial(_band_kernel, halo=halo, ts=ts, seq_len=sub, n_heads=2, head0=A_HEADS + 2 * pair,
                          n_table_cols=rel_bias_flat.shape[0] // NUM_BUCKETS, with_sink=False),
        grid=(b, r // n_sub, sub // tq),
        in_specs=[pl.BlockSpec((ts, nk), lambda b_, c, i: (0, 0)), smem]
        + _band_specs(tq, halo, n_sub, sub // halo),
        out_specs=[out_spec(2 * HEAD_DIM), out_spec(HEAD_DIM)],
        out_shape=[jax.ShapeDtypeStruct((b, r, sub, 2 * HEAD_DIM), BF16),
                   jax.ShapeDtypeStruct((b, r, sub, HEAD_DIM), F32)],
        scratch_shapes=[pltpu.VMEM((2, ts, nk), F32)],
        compiler_params=_cparams(3, _band_vmem(tq, ts, halo, n_sub, 2, 2 * n_sub * tq * HEAD_DIM * 4)),
        name=f"attn_c{pair}",
    )(bkt, rel_bias_flat, zc, zc, zc)


def _c_merge_kernel(*refs):
    n = len(C_DILATIONS)
    o_refs, l_refs, y_ref = refs[:n], refs[n:2 * n], refs[2 * n]
    buf_ref = refs[2 * n + 1]
    tm = y_ref.shape[0]
    slots = iter(range(buf_ref.shape[0]))

    def token_order(ref, r, lanes):
        if r == 1:
            return ref[0, 0, :, lanes].astype(F32)
        slot = next(slots)
        for c in range(r):
            buf_ref[slot, pl.ds(c, tm // r, stride=r), :] = ref[0, c, :, lanes].astype(F32)
        return buf_ref[slot]

    lses = [token_order(ref, r, _lanes(0)) for ref, r in zip(l_refs, C_DILATIONS)]
    m = functools.reduce(jnp.maximum, lses)
    es = [jnp.exp(l - m) for l in lses]
    tot = functools.reduce(jnp.add, es)
    weights = [e / tot for e in es]
    half = HEAD_DIM // 2
    for g in range(y_ref.shape[1] // HEAD_DIM):
        lanes = _lanes(g)
        outs = [token_order(ref, r, lanes) for ref, r in zip(o_refs, C_DILATIONS)]
        y = functools.reduce(jnp.add, [jnp.broadcast_to(w[:, g * half:g * half + 1], (tm, HEAD_DIM)) * o
                                       for w, o in zip(weights, outs)])
        y_ref[:, lanes] = y.astype(y_ref.dtype)


def _attn_c(zcs, rel_bias_flat, *, tm=512):
    outs, lses = zip(*[_attn_c_pair(zc, rel_bias_flat, p) for p, zc in enumerate(zcs)])
    b, _, s, w = outs[0].shape
    m = b * s
    assert s % tm == 0
    pos_tiles = s // tm

    def spec(r, width):
        return pl.BlockSpec((1, r, tm // r, width), lambda i: (i // pos_tiles, 0, i % pos_tiles, 0))

    n_slots = (w // HEAD_DIM + 1) * sum(r > 1 for r in C_DILATIONS)
    return pl.pallas_call(
        _c_merge_kernel,
        grid=(m // tm,),
        in_specs=[spec(r, w) for r in C_DILATIONS] + [spec(r, HEAD_DIM) for r in C_DILATIONS],
        out_specs=pl.BlockSpec((tm, w), lambda i: (i, 0)),
        out_shape=jax.ShapeDtypeStruct((m, w), BF16),
        scratch_shapes=[pltpu.VMEM((n_slots, tm, HEAD_DIM), F32)],
        compiler_params=_cparams(1, 2 * 6 * tm * w * 4 + (n_slots + 16) * tm * HEAD_DIM * 4 + 2 * MIB),
        name="attn_c_merge",
    )(*outs, *lses)


def _attn_b_kernel(*refs, tk, n_casts):
    q_ref, kv_ref = refs[:2]
    cast_in = refs[2:2 + n_casts]
    o_ref = refs[2 + n_casts]
    cast_out = refs[3 + n_casts:]
    tq = q_ref.shape[2]
    s = kv_ref.shape[2]
    q = jnp.concatenate([q_ref[0, 0, :, _lanes(g)] for g in range(3)], axis=0)
    m = acc = None
    for c in range(s // tk):
        rows = slice(c * tk, (c + 1) * tk)
        logits = lax.dot_general(q, kv_ref[0, 0, rows, _lanes(0)], (((1,), (1,)), ((), ())),
                                 preferred_element_type=F32)
        cmax = jnp.max(logits, axis=-1, keepdims=True)
        m_new = cmax if c == 0 else jnp.maximum(m, cmax)
        p = jnp.exp2(logits - m_new)
        pv = jnp.dot(p.astype(BF16), kv_ref[0, 0, rows, HEAD_DIM:], preferred_element_type=F32)
        acc = pv if c == 0 else jnp.exp2(m - m_new) * acc + pv
        m = m_new
    out = acc[:, _lanes(0)] / acc[:, HEAD_DIM:HEAD_DIM + 1]
    for g in range(3):
        o_ref[0, :, _lanes(g)] = out[g * tq:(g + 1) * tq].astype(o_ref.dtype)
    for src, dst in zip(cast_in, cast_out):
        dst[...] = src[...].astype(BF16)


def _cast_blocks(n_rows, n_steps):
    n_blocks = n_steps
    while n_blocks > 1 and (n_rows % n_blocks or (n_rows // n_blocks) % BF16_SUBLANES):
        n_blocks //= 2
    assert n_steps % n_blocks == 0 and n_rows % n_blocks == 0
    return n_blocks


def _attn_b(zbq, zbkv, casts=(), *, tq=256, tk=256):
    b, n_kv, s, _ = zbq.shape
    assert s % tq == 0 and s % tk == 0
    n_qt = s // tq
    n_steps = b * n_kv * n_qt
    in_specs = [pl.BlockSpec((1, 1, tq, 3 * HEAD_DIM), lambda b_, c, i: (b_, c, i, 0)),
                pl.BlockSpec((1, 1, s, 3 * HEAD_DIM), lambda b_, c, i: (b_, c, 0, 0))]
    out_specs = [pl.BlockSpec((1, tq, 3 * HEAD_DIM), lambda b_, c, i: (b_, i, c))]
    out_shape = [jax.ShapeDtypeStruct((b, s, n_kv * 3 * HEAD_DIM), BF16)]
    vmem = 2 * tq * 384 * 2 + 2 * s * 384 * 2 + 2 * tq * 384 * 2 \
        + 8 * 3 * tq * tk * 4 + 12 * 3 * tq * HEAD_DIM * 4 + 2 * MIB
    for w, layer in casts:
        _, n_rows, n_cols = w.shape
        n_blocks = _cast_blocks(n_rows, n_steps)
        rep = n_steps // n_blocks
        rows = n_rows // n_blocks

        def block(b_, c, i, rep=rep):
            return ((b_ * n_kv + c) * n_qt + i) // rep

        in_specs.append(pl.BlockSpec((None, rows, n_cols),
                                     lambda b_, c, i, layer=layer, block=block: (layer, block(b_, c, i), 0)))
        out_specs.append(pl.BlockSpec((rows, n_cols), lambda b_, c, i, block=block: (block(b_, c, i), 0)))
        out_shape.append(jax.ShapeDtypeStruct((n_rows, n_cols), BF16))
        vmem += 2 * rows * n_cols * (4 + 2) + rows * n_cols * 4
    yb, *cast_out = pl.pallas_call(
        functools.partial(_attn_b_kernel, tk=tk, n_casts=len(casts)),
        grid=(b, n_kv, n_qt),
        in_specs=in_specs,
        out_specs=out_specs,
        out_shape=out_shape,
        compiler_params=_cparams(3, vmem),
        name="attn_b_cast" if casts else "attn_b",
    )(zbq, zbkv, *[w for w, _ in casts])
    return yb, cast_out


def _merge_kernel(x_ref, h_ref, ya_ref, yb_ref, yc_ref, wga_ref, wgb_ref, wgc_ref,
                  bga_ref, bgb_ref, bgc_ref, wa_ref, wb_ref, wc_ref, wo_ref, o_ref, *, n_chunk):
    h = h_ref[...]

    def branch(y_ref, w_ref, wg_ref, bg_ref):
        gate = jax.nn.sigmoid(jnp.dot(h, wg_ref[...], preferred_element_type=F32) + bg_ref[...])
        return gate * jnp.dot(y_ref[...], w_ref[...], preferred_element_type=F32)

    @pl.when(pl.program_id(1) == 0)
    def _():
        o_ref[...] = x_ref[...]

    merged = (branch(ya_ref, wa_ref, wga_ref, bga_ref) + branch(yb_ref, wb_ref, wgb_ref, bgb_ref)
              + branch(yc_ref, wc_ref, wgc_ref, bgc_ref)).astype(BF16)
    for n in range(o_ref.shape[1] // n_chunk):
        cols = slice(n * n_chunk, (n + 1) * n_chunk)
        o_ref[:, cols] += jnp.dot(merged, wo_ref[:, cols], preferred_element_type=F32)


def _merge(x, h, ya, yb, yc, w_gate, b_gate, w_br_a, w_br_b, w_br_c, w_o, *, tm=512, tn=512, n_chunk=512):
    m, d = x.shape
    tn = min(tn, d)
    n_chunk = min(n_chunk, d)
    assert m % tm == 0 and d % tn == 0 and d % n_chunk == 0
    n_steps = d // tn
    ka, kb, kc = ya.shape[1], yb.shape[1], yc.shape[1]

    def row(width):
        return pl.BlockSpec((tm, width), lambda i, j: (i, 0))

    def gate_w(br):
        return pl.BlockSpec((d, tn), lambda i, j: (0, br * n_steps + j))

    def gate_b(br):
        return pl.BlockSpec((1, tn), lambda i, j: (0, br * n_steps + j))

    def br_w(k):
        return pl.BlockSpec((k, tn), lambda i, j: (0, j))

    bg = b_gate.reshape(1, 3 * d)
    vmem = 2 * (tm * d * 4 + tm * d * 2 + tm * (ka + kb + kc) * 2) + 2 * tm * d * 4 \
        + 2 * (3 * d + ka + kb + kc + d) * tn * 2 + 7 * tm * tn * 4 + 3 * tm * n_chunk * 4 + 2 * MIB
    return pl.pallas_call(
        functools.partial(_merge_kernel, n_chunk=n_chunk),
        grid=(m // tm, n_steps),
        in_specs=[row(d), row(d), row(ka), row(kb), row(kc),
                  gate_w(0), gate_w(1), gate_w(2), gate_b(0), gate_b(1), gate_b(2),
                  br_w(ka), br_w(kb), br_w(kc),
                  pl.BlockSpec((tn, d), lambda i, j: (j, 0))],
        out_specs=pl.BlockSpec((tm, d), lambda i, j: (i, 0)),
        out_shape=jax.ShapeDtypeStruct((m, d), F32),
        compiler_params=_cparams(2, vmem),
        name="merge",
    )(x, h, ya, yb, yc, w_gate, w_gate, w_gate, bg, bg, bg, w_br_a, w_br_b, w_br_c, w_o)


def _rope_tables(s):
    t = jnp.arange(s)
    row_ids = (t // GRID_W).astype(F32)
    col_ids = (t % GRID_W).astype(F32)
    axis_dim = HEAD_DIM // 2
    inv_freq = ROPE_THETA ** (-jnp.arange(0, axis_dim, 2, dtype=F32) / axis_dim)
    ang_r = row_ids[:, None] * inv_freq
    ang_c = col_ids[:, None] * inv_freq
    cos = jnp.concatenate([jnp.cos(ang_r), jnp.cos(ang_r), jnp.cos(ang_c), jnp.cos(ang_c)], axis=1)
    sin = jnp.concatenate([-jnp.sin(ang_r), jnp.sin(ang_r), -jnp.sin(ang_c), jnp.sin(ang_c)], axis=1)
    return cos, sin


_EARLY_WEIGHTS = ("ffn1_w13", "ffn1_w2", "w_in")
_LATE_WEIGHTS = ("w_gate", "w_br_a", "w_br_b", "w_br_c", "w_o", "ffn2_w13", "ffn2_w2")


def _trunk(x, p, wb, host_casts, cos_tab, sin_tab):
    b, s, d = x.shape
    depth = p["w_in"].shape[0]
    rel_flat = p["rel_bias"].reshape(-1)
    xf = x.reshape(b * s, d)
    for li in range(depth):
        xf = _ffn(xf, p["ffn1_norm"][li], wb["ffn1_w13", li], wb["ffn1_w2", li])
        h, za, zbq, zbkv, *zcs = _mix_in(xf, p["mix_norm"][li], wb["w_in", li], p["q_gain_b"][li],
                                        p["k_gain_b"][li], cos_tab, sin_tab)
        ya = _attn_a(za, rel_flat, p["sink_a"][li]).reshape(b * s, -1)
        keys = []
        if host_casts:
            keys = [(name, li) for name in _LATE_WEIGHTS]
            if li + 1 < depth:
                keys += [(name, li + 1) for name in _EARLY_WEIGHTS]
        yb, copies = _attn_b(zbq, zbkv, [(p[name], layer) for name, layer in keys])
        wb.update(zip(keys, copies))
        yc = _attn_c(zcs, rel_flat)
        xf = _merge(xf, h, ya, yb.reshape(b * s, -1), yc, wb["w_gate", li], p["b_gate"][li], wb["w_br_a", li],
                    wb["w_br_b", li], wb["w_br_c", li], wb["w_o", li])
        fin = p["final_norm"] if li == depth - 1 else None
        xf = _ffn(xf, p["ffn2_norm"][li], wb["ffn2_w13", li], wb["ffn2_w2", li], fin)
    return xf.reshape(b, s, d)


def kernel(x_prompt, x_sample, ffn1_norm, ffn1_w13, ffn1_w2, mix_norm, w_in, q_gain_b, k_gain_b, sink_a,
           w_gate, b_gate, w_br_a, w_br_b, w_br_c, w_o, ffn2_norm, ffn2_w13, ffn2_w2, rel_bias, final_norm):
    p = dict(ffn1_norm=ffn1_norm, ffn1_w13=ffn1_w13, ffn1_w2=ffn1_w2, mix_norm=mix_norm, w_in=w_in,
             q_gain_b=q_gain_b, k_gain_b=k_gain_b, sink_a=sink_a, w_gate=w_gate, b_gate=b_gate,
             w_br_a=w_br_a, w_br_b=w_br_b, w_br_c=w_br_c, w_o=w_o, ffn2_norm=ffn2_norm,
             ffn2_w13=ffn2_w13, ffn2_w2=ffn2_w2, rel_bias=rel_bias, final_norm=final_norm)
    wb = {(name, 0): p[name][0].astype(BF16) for name in _EARLY_WEIGHTS}
    cos_tab, sin_tab = _rope_tables(max(x_prompt.shape[1], x_sample.shape[1]))
    s_p, s_s = x_prompt.shape[1], x_sample.shape[1]
    y_sample = _trunk(x_sample, p, wb, True, cos_tab[:s_s], sin_tab[:s_s])
    y_prompt = _trunk(x_prompt, p, wb, False, cos_tab[:s_p], sin_tab[:s_p])
    return y_prompt, y_sample
```

```python
import functools
import math

import jax
import jax.numpy as jnp
from jax import lax
from jax.experimental import pallas as pl
from jax.experimental.pallas import tpu as pltpu

F32 = jnp.float32
BF16 = jnp.bfloat16

HEAD_DIM = 128
NORM_EPS = 1e-6
MASK_VALUE = -1e30
QK_SCALE = 1.0 / math.sqrt(HEAD_DIM)
LOG2E = math.log2(math.e)
FFN_RESIDUAL = 0.5
ROPE_THETA = 10000.0
GRID_W = 64
NUM_BUCKETS = 32
REL_MAX_DISTANCE = 2048

A_Q, A_K, A_V = 0, 4, 6
B_Q, B_K, B_V = 8, 14, 16
C_Q, C_K, C_V = 18, 24, 27
N_HEAD_BLOCKS = 30
A_HEADS = 4
A_HALF_WINDOW = 128
C_HALF_WINDOW = 64
C_DILATIONS = (1, 4, 16)
PACKED = 4 * HEAD_DIM
A_SUBTILE = 256
C_SUBTILE = 128
C_ROWS_PER_STEP = 2048
ROW_CHUNK = 256
BF16_SUBLANES = 16

V7X_VMEM_BYTES = 64 * 1024 * 1024
MIB = 1024 * 1024


def _cparams(n_grid, vmem_bytes):
    assert vmem_bytes < V7X_VMEM_BYTES, vmem_bytes
    return pltpu.CompilerParams(dimension_semantics=("arbitrary",) * n_grid,
                                vmem_limit_bytes=V7X_VMEM_BYTES)


def _rms(x, gain):
    ms = jnp.mean(x * x, axis=-1, keepdims=True)
    return x * lax.rsqrt(ms + NORM_EPS) * gain


def _lanes(slot):
    return slice(slot * HEAD_DIM, (slot + 1) * HEAD_DIM)


def _ffn_kernel(*refs, n_steps, n_chunk, final):
    if final:
        x_ref, g_ref, wg_ref, wu_ref, w2_ref, fin_ref, o_ref, xn_ref = refs
    else:
        x_ref, g_ref, wg_ref, wu_ref, w2_ref, o_ref, xn_ref = refs
    j = pl.program_id(1)
    row_chunks = [slice(r, r + ROW_CHUNK) for r in range(0, x_ref.shape[0], ROW_CHUNK)]

    def step(first):
        if first:
            for rows in row_chunks:
                xn_ref[rows, :] = _rms(x_ref[rows, :], g_ref[...]).astype(BF16)
        xn = xn_ref[...]
        gate = jnp.dot(xn, wg_ref[...], preferred_element_type=F32)
        up = jnp.dot(xn, wu_ref[...], preferred_element_type=F32)
        act = (gate * jax.nn.sigmoid(gate) * up * FFN_RESIDUAL).astype(BF16)
        acc_ref = x_ref if first else o_ref
        for n in range(o_ref.shape[1] // n_chunk):
            cols = slice(n * n_chunk, (n + 1) * n_chunk)
            o_ref[:, cols] = acc_ref[:, cols] + jnp.dot(act, w2_ref[:, cols], preferred_element_type=F32)

    pl.when(j == 0)(functools.partial(step, True))
    pl.when(j > 0)(functools.partial(step, False))

    if final:
        @pl.when(j == n_steps - 1)
        def _():
            for rows in row_chunks:
                o_ref[rows, :] = _rms(o_ref[rows, :], fin_ref[...])


def _ffn(x, gain, w13, w2, final_gain=None, *, tm=512, tf=512, n_chunk=512):
    m, d = x.shape
    d_ff = w2.shape[0]
    n_chunk = min(n_chunk, d)
    assert m % tm == 0 and d_ff % tf == 0 and d % n_chunk == 0 and tm % ROW_CHUNK == 0
    n_steps = d_ff // tf
    final = final_gain is not None
    in_specs = [
        pl.BlockSpec((tm, d), lambda i, j: (i, 0)),
        pl.BlockSpec((1, d), lambda i, j: (0, 0)),
        pl.BlockSpec((d, tf), lambda i, j: (0, j)),
        pl.BlockSpec((d, tf), lambda i, j: (0, j + n_steps)),
        pl.BlockSpec((tf, d), lambda i, j: (j, 0)),
    ]
    args = [x, gain.reshape(1, d), w13, w13, w2]
    if final:
        in_specs.append(pl.BlockSpec((1, d), lambda i, j: (0, 0)))
        args.append(final_gain.reshape(1, d))
    vmem = 2 * tm * d * 4 + 2 * tm * d * 4 + tm * d * 2 + 2 * (3 * d * tf * 2) \
        + 3 * tm * tf * 4 + tm * tf * 2 + 3 * tm * n_chunk * 4 + 3 * ROW_CHUNK * d * 4 + 2 * MIB
    return pl.pallas_call(
        functools.partial(_ffn_kernel, n_steps=n_steps, n_chunk=n_chunk, final=final),
        grid=(m // tm, n_steps),
        in_specs=in_specs,
        out_specs=pl.BlockSpec((tm, d), lambda i, j: (i, 0)),
        out_shape=jax.ShapeDtypeStruct((m, d), F32),
        scratch_shapes=[pltpu.VMEM((tm, d), BF16)],
        compiler_params=_cparams(2, vmem),
        name="ffn_final" if final else "ffn",
    )(*args)


def _swap_quarter_pairs(x):
    q = HEAD_DIM // 4
    lane = lax.broadcasted_iota(jnp.int32, x.shape, 1)
    from_right = pltpu.roll(x, HEAD_DIM - q, axis=1)
    from_left = pltpu.roll(x, q, axis=1)
    return jnp.where((lane % (2 * q)) < q, from_right, from_left)


def _mixin_kernel(x_ref, g_ref, w_ref, qg_ref, kg_ref, cos_ref, sin_ref,
                  h_ref, za_ref, zbq_ref, zbkv_ref, zc0_ref, zc1_ref, zc2_ref, cs_ref):
    tm = x_ref.shape[0]
    h = _rms(x_ref[...], g_ref[...]).astype(BF16)
    h_ref[...] = h
    cos = cos_ref[...]
    sin = sin_ref[...]

    def norm_rope(v, gain):
        ms = jnp.mean(v * v, axis=-1, keepdims=True)
        y = v * lax.rsqrt(ms + NORM_EPS) * gain
        return y * cos + _swap_quarter_pairs(y) * sin

    for c in range(N_HEAD_BLOCKS // 2):
        zc = jnp.dot(h, w_ref[:, c * 256:(c + 1) * 256], preferred_element_type=F32)
        for hh in range(2):
            blk = 2 * c + hh
            v = zc[:, _lanes(hh)]
            if blk < A_K:
                za_ref[0, blk // 2, :, _lanes(blk % 2)] = (v * QK_SCALE).astype(BF16)
            elif blk < A_V:
                za_ref[0, blk - A_K, :, _lanes(2)] = v.astype(BF16)
            elif blk < B_Q:
                za_ref[0, blk - A_V, :, _lanes(3)] = v.astype(BF16)
            elif blk < B_K:
                q = norm_rope(v, qg_ref[...]) * (QK_SCALE * LOG2E)
                zbq_ref[0, (blk - B_Q) // 3, :, _lanes((blk - B_Q) % 3)] = q.astype(BF16)
            elif blk < B_V:
                zbkv_ref[0, blk - B_K, :, _lanes(0)] = norm_rope(v, kg_ref[...]).astype(BF16)
            elif blk < C_Q:
                zbkv_ref[0, blk - B_V, :, _lanes(1)] = v.astype(BF16)
            elif blk < C_K:
                cs_ref[blk - C_Q] = v * QK_SCALE
            else:
                cs_ref[blk - C_Q] = v
    for kv in range(zbkv_ref.shape[1]):
        zbkv_ref[0, kv, :, _lanes(2)] = jnp.ones((tm, HEAD_DIM), BF16)

    n_pairs = len(C_DILATIONS)
    for pair, (r, zc_ref) in enumerate(zip(C_DILATIONS, (zc0_ref, zc1_ref, zc2_ref))):
        staged = (2 * pair, 2 * pair + 1, 2 * n_pairs + pair, 3 * n_pairs + pair)
        for slot, src in enumerate(staged):
            for c in range(r):
                rows = pl.ds(c, tm // r, stride=r) if r > 1 else slice(None)
                zc_ref[0, c, :, _lanes(slot)] = cs_ref[src, rows, :].astype(BF16)


def _mix_in(x, gain, w_in, q_gain, k_gain, cos_tab, sin_tab, *, tm=512):
    m, d = x.shape
    s = cos_tab.shape[0]
    b = m // s
    n_cols = w_in.shape[1]
    assert m % tm == 0 and s % tm == 0 and n_cols == N_HEAD_BLOCKS * HEAD_DIM
    assert all(tm % (16 * r) == 0 for r in C_DILATIONS)
    pos_tiles = s // tm

    def packed(n, width, rows):
        return pl.BlockSpec((1, n, rows, width), lambda i: (i // pos_tiles, 0, i % pos_tiles, 0))

    out_specs = [pl.BlockSpec((tm, d), lambda i: (i, 0)),
                 packed(2, PACKED, tm), packed(2, 3 * HEAD_DIM, tm), packed(2, 3 * HEAD_DIM, tm)]
    out_shape = [jax.ShapeDtypeStruct((m, d), BF16),
                 jax.ShapeDtypeStruct((b, 2, s, PACKED), BF16),
                 jax.ShapeDtypeStruct((b, 2, s, 3 * HEAD_DIM), BF16),
                 jax.ShapeDtypeStruct((b, 2, s, 3 * HEAD_DIM), BF16)]
    for r in C_DILATIONS:
        out_specs.append(packed(r, PACKED, tm // r))
        out_shape.append(jax.ShapeDtypeStruct((b, r, s // r, PACKED), BF16))
    c_cols = (N_HEAD_BLOCKS - C_Q) * HEAD_DIM
    vmem = 2 * tm * d * 4 + 2 * tm * d * 2 + 2 * tm * (n_cols + 2 * HEAD_DIM) * 2 + d * n_cols * 2 \
        + 4 * tm * HEAD_DIM * 4 + tm * c_cols * 4 + tm * d * 4 + 4 * tm * 256 * 4 \
        + 8 * tm * HEAD_DIM * 4 + 2 * MIB
    return pl.pallas_call(
        _mixin_kernel,
        grid=(m // tm,),
        in_specs=[
            pl.BlockSpec((tm, d), lambda i: (i, 0)),
            pl.BlockSpec((1, d), lambda i: (0, 0)),
            pl.BlockSpec((d, n_cols), lambda i: (0, 0), pipeline_mode=pl.Buffered(1)),
            pl.BlockSpec((1, HEAD_DIM), lambda i: (0, 0)),
            pl.BlockSpec((1, HEAD_DIM), lambda i: (0, 0)),
            pl.BlockSpec((tm, HEAD_DIM), lambda i: (i % pos_tiles, 0)),
            pl.BlockSpec((tm, HEAD_DIM), lambda i: (i % pos_tiles, 0)),
        ],
        out_specs=out_specs,
        out_shape=out_shape,
        scratch_shapes=[pltpu.VMEM((N_HEAD_BLOCKS - C_Q, tm, HEAD_DIM), F32)],
        compiler_params=_cparams(1, vmem),
        name="mix_in",
    )(x, gain.reshape(1, d), w_in, q_gain.reshape(1, HEAD_DIM), k_gain.reshape(1, HEAD_DIM),
      cos_tab, sin_tab)


def _t5_bucket(rel):
    half = NUM_BUCKETS // 2
    max_exact = half // 2
    ret = jnp.where(rel > 0, half, 0)
    n = jnp.abs(rel)
    nf = jnp.maximum(n, 1).astype(jnp.float32)
    large = max_exact + (jnp.log(nf / max_exact) / math.log(REL_MAX_DISTANCE / max_exact)
                         * (half - max_exact)).astype(jnp.int32)
    large = jnp.minimum(large, half - 1)
    return ret + jnp.where(n < max_exact, n, large)


def _band_rel(ts, halo):
    return jnp.arange(ts + 2 * halo)[None, :] - halo - jnp.arange(ts)[:, None]


def _band_kernel(*refs, halo, ts, seq_len, n_heads, head0, n_table_cols, with_sink):
    if with_sink:
        bkt_ref, tab_ref, sink_ref, left_ref, main_ref, right_ref, o_ref, bias_ref = refs
        lse_ref = None
    else:
        bkt_ref, tab_ref, left_ref, main_ref, right_ref, o_ref, lse_ref, bias_ref = refs
        sink_ref = None
    n_sub, tq = main_ref.shape[1], main_ref.shape[2]
    nk = ts + 2 * halo
    first = (pl.program_id(0) == 0) & (pl.program_id(1) == 0) & (pl.program_id(2) == 0)

    @pl.when(first)
    def _():
        bkt = bkt_ref[...]
        rel = (lax.broadcasted_iota(jnp.int32, (ts, nk), 1) - halo
               - lax.broadcasted_iota(jnp.int32, (ts, nk), 0))
        in_band = jnp.abs(rel) <= halo
        tiles = [jnp.zeros((ts, nk), F32) for _ in range(n_heads)]
        for t in range(NUM_BUCKETS):
            hit = bkt == t
            for h in range(n_heads):
                tiles[h] = jnp.where(hit, tab_ref[t * n_table_cols + head0 + h], tiles[h])
        for h in range(n_heads):
            bias_ref[h] = jnp.where(in_band, tiles[h], MASK_VALUE)

    qi = pl.program_id(2)
    for sub in range(n_sub):
        def keys(slot):
            return jnp.concatenate([left_ref[0, sub, :, _lanes(slot)], main_ref[0, sub, :, _lanes(slot)],
                                    right_ref[0, sub, :, _lanes(slot)]], axis=0)

        k = keys(2)
        v = keys(3)
        for t in range(tq // ts):
            rows = slice(t * ts, (t + 1) * ts)
            kt = k[t * ts:t * ts + nk]
            vt = v[t * ts:t * ts + nk]
            key_pos = qi * tq + t * ts - halo + lax.broadcasted_iota(jnp.int32, (1, nk), 1)
            key_ok = (key_pos >= 0) & (key_pos < seq_len)
            lses = []
            for g in range(2):
                head = (pl.program_id(1) * n_sub + sub) * 2 + g if with_sink else g
                logits = lax.dot_general(main_ref[0, sub, rows, _lanes(g)], kt, (((1,), (1,)), ((), ())),
                                         preferred_element_type=F32)
                logits = jnp.where(key_ok, logits + bias_ref[head], MASK_VALUE)
                m = jnp.max(logits, axis=-1, keepdims=True)
                if with_sink:
                    sink = sink_ref[head]
                    m = jnp.maximum(m, sink)
                p = jnp.exp(logits - m)
                denom = jnp.sum(p, axis=-1, keepdims=True)
                if with_sink:
                    denom = denom + jnp.exp(sink - m)
                out = jnp.dot(p.astype(BF16), vt, preferred_element_type=F32) / denom
                if with_sink:
                    o_ref[0, rows, _lanes(2 * sub + g)] = out.astype(o_ref.dtype)
                else:
                    o_ref[0, sub, rows, _lanes(g)] = out.astype(o_ref.dtype)
                    lses.append(m + jnp.log(denom))
            if not with_sink:
                lane = lax.broadcasted_iota(jnp.int32, (ts, HEAD_DIM), 1)
                lse_ref[0, sub, rows, :] = jnp.where(lane < HEAD_DIM // 2, lses[0], lses[1])


def _band_specs(tq, halo, n_sub, n_halo_blocks):
    per = tq // halo
    return [
        pl.BlockSpec((1, n_sub, halo, PACKED), lambda b, c, i: (b, c, jnp.maximum(i * per - 1, 0), 0)),
        pl.BlockSpec((1, n_sub, tq, PACKED), lambda b, c, i: (b, c, i, 0)),
        pl.BlockSpec((1, n_sub, halo, PACKED),
                     lambda b, c, i: (b, c, jnp.minimum((i + 1) * per, n_halo_blocks - 1), 0)),
    ]


def _band_vmem(tq, ts, halo, n_sub, n_heads, out_bytes):
    nk = ts + 2 * halo
    return (n_heads * ts * nk * 4 + 2 * ts * nk * 4 + 2 * n_sub * (tq + 2 * halo) * PACKED * 2 + 2 * out_bytes
            + 10 * ts * nk * 4 + 4 * (tq + 2 * halo) * HEAD_DIM * 2 + 4 * MIB)


def _attn_a(za, rel_bias_flat, sink, *, tq=1024, n_sub=2):
    b, n_kv, s, _ = za.shape
    halo = A_HALF_WINDOW
    ts = A_SUBTILE
    assert s % tq == 0 and tq % halo == 0 and n_kv % n_sub == 0 and tq % ts == 0
    nk = ts + 2 * halo
    bkt = _t5_bucket(_band_rel(ts, halo)).astype(jnp.int32)
    smem = pl.BlockSpec(memory_space=pltpu.SMEM)
    out_w = n_sub * 2 * HEAD_DIM
    return pl.pallas_call(
        functools.partial(_band_kernel, halo=halo, ts=ts, seq_len=s, n_heads=A_HEADS, head0=0,
                          n_table_cols=rel_bias_flat.shape[0] // NUM_BUCKETS, with_sink=True),
        grid=(b, n_kv // n_sub, s // tq),
        in_specs=[pl.BlockSpec((ts, nk), lambda b_, c, i: (0, 0)), smem, smem]
        + _band_specs(tq, halo, n_sub, s // halo),
        out_specs=pl.BlockSpec((1, tq, out_w), lambda b_, c, i: (b_, i, c)),
        out_shape=jax.ShapeDtypeStruct((b, s, A_HEADS * HEAD_DIM), BF16),
        scratch_shapes=[pltpu.VMEM((A_HEADS, ts, nk), F32)],
        compiler_params=_cparams(3, _band_vmem(tq, ts, halo, n_sub, A_HEADS, tq * out_w * 2)),
        name="attn_a",
    )(bkt, rel_bias_flat, sink, za, za, za)


def _attn_c_pair(zc, rel_bias_flat, pair):
    b, r, sub, _ = zc.shape
    halo = C_HALF_WINDOW
    n_sub = min(r, C_ROWS_PER_STEP // C_SUBTILE)
    tq = min(sub, C_ROWS_PER_STEP // n_sub)
    ts = min(C_SUBTILE, tq)
    assert sub % tq == 0 and tq % halo == 0 and r % n_sub == 0 and tq % ts == 0
    nk = ts + 2 * halo
    bkt = _t5_bucket(_band_rel(ts, halo) * r).astype(jnp.int32)
    smem = pl.BlockSpec(memory_space=pltpu.SMEM)
    def out_spec(width):
        return pl.BlockSpec((1, n_sub, tq, width), lambda b_, c, i: (b_, c, i, 0))

    return pl.pallas_call(
        functools.partial(_band_kernel, halo=halo, ts=ts, seq_len=sub, n_heads=2, head0=A_HEADS + 2 * pair,
                          n_table_cols=rel_bias_flat.shape[0] // NUM_BUCKETS, with_sink=False),
        grid=(b, r // n_sub, sub // tq),
        in_specs=[pl.BlockSpec((ts, nk), lambda b_, c, i: (0, 0)), smem]
        + _band_specs(tq, halo, n_sub, sub // halo),
        out_specs=[out_spec(2 * HEAD_DIM), out_spec(HEAD_DIM)],
        out_shape=[jax.ShapeDtypeStruct((b, r, sub, 2 * HEAD_DIM), BF16),
                   jax.ShapeDtypeStruct((b, r, sub, HEAD_DIM), F32)],
        scratch_shapes=[pltpu.VMEM((2, ts, nk), F32)],
        compiler_params=_cparams(3, _band_vmem(tq, ts, halo, n_sub, 2, 2 * n_sub * tq * HEAD_DIM * 4)),
        name=f"attn_c{pair}",
    )(bkt, rel_bias_flat, zc, zc, zc)


def _c_merge_kernel(*refs):
    n = len(C_DILATIONS)
    o_refs, l_refs, y_ref = refs[:n], refs[n:2 * n], refs[2 * n]
    buf_ref = refs[2 * n + 1]
    tm = y_ref.shape[0]
    slots = iter(range(buf_ref.shape[0]))

    def token_order(ref, r, lanes):
        if r == 1:
            return ref[0, 0, :, lanes].astype(F32)
        slot = next(slots)
        for c in range(r):
            buf_ref[slot, pl.ds(c, tm // r, stride=r), :] = ref[0, c, :, lanes].astype(F32)
        return buf_ref[slot]

    lses = [token_order(ref, r, _lanes(0)) for ref, r in zip(l_refs, C_DILATIONS)]
    m = functools.reduce(jnp.maximum, lses)
    es = [jnp.exp(l - m) for l in lses]
    tot = functools.reduce(jnp.add, es)
    weights = [e / tot for e in es]
    half = HEAD_DIM // 2
    for g in range(y_ref.shape[1] // HEAD_DIM):
        lanes = _lanes(g)
        outs = [token_order(ref, r, lanes) for ref, r in zip(o_refs, C_DILATIONS)]
        y = functools.reduce(jnp.add, [jnp.broadcast_to(w[:, g * half:g * half + 1], (tm, HEAD_DIM)) * o
                                       for w, o in zip(weights, outs)])
        y_ref[:, lanes] = y.astype(y_ref.dtype)


def _attn_c(zcs, rel_bias_flat, *, tm=512):
    outs, lses = zip(*[_attn_c_pair(zc, rel_bias_flat, p) for p, zc in enumerate(zcs)])
    b, _, s, w = outs[0].shape
    m = b * s
    assert s % tm == 0
    pos_tiles = s // tm

    def spec(r, width):
        return pl.BlockSpec((1, r, tm // r, width), lambda i: (i // pos_tiles, 0, i % pos_tiles, 0))

    n_slots = (w // HEAD_DIM + 1) * sum(r > 1 for r in C_DILATIONS)
    return pl.pallas_call(
        _c_merge_kernel,
        grid=(m // tm,),
        in_specs=[spec(r, w) for r in C_DILATIONS] + [spec(r, HEAD_DIM) for r in C_DILATIONS],
        out_specs=pl.BlockSpec((tm, w), lambda i: (i, 0)),
        out_shape=jax.ShapeDtypeStruct((m, w), BF16),
        scratch_shapes=[pltpu.VMEM((n_slots, tm, HEAD_DIM), F32)],
        compiler_params=_cparams(1, 2 * 6 * tm * w * 4 + (n_slots + 16) * tm * HEAD_DIM * 4 + 2 * MIB),
        name="attn_c_merge",
    )(*outs, *lses)


def _attn_b_kernel(*refs, tk, n_casts):
    q_ref, kv_ref = refs[:2]
    cast_in = refs[2:2 + n_casts]
    o_ref = refs[2 + n_casts]
    cast_out = refs[3 + n_casts:]
    tq = q_ref.shape[2]
    s = kv_ref.shape[2]
    q = jnp.concatenate([q_ref[0, 0, :, _lanes(g)] for g in range(3)], axis=0)
    m = acc = None
    for c in range(s // tk):
        rows = slice(c * tk, (c + 1) * tk)
        logits = lax.dot_general(q, kv_ref[0, 0, rows, _lanes(0)], (((1,), (1,)), ((), ())),
                                 preferred_element_type=F32)
        cmax = jnp.max(logits, axis=-1, keepdims=True)
        m_new = cmax if c == 0 else jnp.maximum(m, cmax)
        p = jnp.exp2(logits - m_new)
        pv = jnp.dot(p.astype(BF16), kv_ref[0, 0, rows, HEAD_DIM:], preferred_element_type=F32)
        acc = pv if c == 0 else jnp.exp2(m - m_new) * acc + pv
        m = m_new
    out = acc[:, _lanes(0)] / acc[:, HEAD_DIM:HEAD_DIM + 1]
    for g in range(3):
        o_ref[0, :, _lanes(g)] = out[g * tq:(g + 1) * tq].astype(o_ref.dtype)
    for src, dst in zip(cast_in, cast_out):
        dst[...] = src[...].astype(BF16)


def _cast_blocks(n_rows, n_steps):
    n_blocks = n_steps
    while n_blocks > 1 and (n_rows % n_blocks or (n_rows // n_blocks) % BF16_SUBLANES):
        n_blocks //= 2
    assert n_steps % n_blocks == 0 and n_rows % n_blocks == 0
    return n_blocks


def _attn_b(zbq, zbkv, casts=(), *, tq=256, tk=256):
    b, n_kv, s, _ = zbq.shape
    assert s % tq == 0 and s % tk == 0
    n_qt = s // tq
    n_steps = b * n_kv * n_qt
    in_specs = [pl.BlockSpec((1, 1, tq, 3 * HEAD_DIM), lambda b_, c, i: (b_, c, i, 0)),
                pl.BlockSpec((1, 1, s, 3 * HEAD_DIM), lambda b_, c, i: (b_, c, 0, 0))]
    out_specs = [pl.BlockSpec((1, tq, 3 * HEAD_DIM), lambda b_, c, i: (b_, i, c))]
    out_shape = [jax.ShapeDtypeStruct((b, s, n_kv * 3 * HEAD_DIM), BF16)]
    vmem = 2 * tq * 384 * 2 + 2 * s * 384 * 2 + 2 * tq * 384 * 2 \
        + 8 * 3 * tq * tk * 4 + 12 * 3 * tq * HEAD_DIM * 4 + 2 * MIB
    for w, layer in casts:
        _, n_rows, n_cols = w.shape
        n_blocks = _cast_blocks(n_rows, n_steps)
        rep = n_steps // n_blocks
        rows = n_rows // n_blocks

        def block(b_, c, i, rep=rep):
            return ((b_ * n_kv + c) * n_qt + i) // rep

        in_specs.append(pl.BlockSpec((None, rows, n_cols),
                                     lambda b_, c, i, layer=layer, block=block: (layer, block(b_, c, i), 0)))
        out_specs.append(pl.BlockSpec((rows, n_cols), lambda b_, c, i, block=block: (block(b_, c, i), 0)))
        out_shape.append(jax.ShapeDtypeStruct((n_rows, n_cols), BF16))
        vmem += 2 * rows * n_cols * (4 + 2) + rows * n_cols * 4
    yb, *cast_out = pl.pallas_call(
        functools.partial(_attn_b_kernel, tk=tk, n_casts=len(casts)),
        grid=(b, n_kv, n_qt),
        in_specs=in_specs,
        out_specs=out_specs,
        out_shape=out_shape,
        compiler_params=_cparams(3, vmem),
        name="attn_b_cast" if casts else "attn_b",
    )(zbq, zbkv, *[w for w, _ in casts])
    return yb, cast_out


def _merge_kernel(x_ref, h_ref, ya_ref, yb_ref, yc_ref, wga_ref, wgb_ref, wgc_ref,
                  bga_ref, bgb_ref, bgc_ref, wa_ref, wb_ref, wc_ref, wo_ref, o_ref, *, n_chunk):
    def step(first):
        h = h_ref[...]

        def branch(y_ref, w_ref, wg_ref, bg_ref):
            gate = jax.nn.sigmoid(jnp.dot(h, wg_ref[...], preferred_element_type=F32) + bg_ref[...])
            return gate * jnp.dot(y_ref[...], w_ref[...], preferred_element_type=F32)

        merged = (branch(ya_ref, wa_ref, wga_ref, bga_ref) + branch(yb_ref, wb_ref, wgb_ref, bgb_ref)
                  + branch(yc_ref, wc_ref, wgc_ref, bgc_ref)).astype(BF16)
        acc_ref = x_ref if first else o_ref
        for n in range(o_ref.shape[1] // n_chunk):
            cols = slice(n * n_chunk, (n + 1) * n_chunk)
            o_ref[:, cols] = acc_ref[:, cols] + jnp.dot(merged, wo_ref[:, cols], preferred_element_type=F32)

    j = pl.program_id(1)
    pl.when(j == 0)(functools.partial(step, True))
    pl.when(j > 0)(functools.partial(step, False))


def _merge(x, h, ya, yb, yc, w_gate, b_gate, w_br_a, w_br_b, w_br_c, w_o, *, tm=512, tn=512, n_chunk=512):
    m, d = x.shape
    tn = min(tn, d)
    n_chunk = min(n_chunk, d)
    assert m % tm == 0 and d % tn == 0 and d % n_chunk == 0
    n_steps = d // tn
    ka, kb, kc = ya.shape[1], yb.shape[1], yc.shape[1]

    def row(width):
        return pl.BlockSpec((tm, width), lambda i, j: (i, 0))

    def gate_w(br):
        return pl.BlockSpec((d, tn), lambda i, j: (0, br * n_steps + j))

    def gate_b(br):
        return pl.BlockSpec((1, tn), lambda i, j: (0, br * n_steps + j))

    def br_w(k):
        return pl.BlockSpec((k, tn), lambda i, j: (0, j))

    bg = b_gate.reshape(1, 3 * d)
    vmem = 2 * (tm * d * 4 + tm * d * 2 + tm * (ka + kb + kc) * 2) + 2 * tm * d * 4 \
        + 2 * (3 * d + ka + kb + kc + d) * tn * 2 + 7 * tm * tn * 4 + 3 * tm * n_chunk * 4 + 2 * MIB
    return pl.pallas_call(
        functools.partial(_merge_kernel, n_chunk=n_chunk),
        grid=(m // tm, n_steps),
        in_specs=[row(d), row(d), row(ka), row(kb), row(kc),
                  gate_w(0), gate_w(1), gate_w(2), gate_b(0), gate_b(1), gate_b(2),
                  br_w(ka), br_w(kb), br_w(kc),
                  pl.BlockSpec((tn, d), lambda i, j: (j, 0))],
        out_specs=pl.BlockSpec((tm, d), lambda i, j: (i, 0)),
        out_shape=jax.ShapeDtypeStruct((m, d), F32),
        compiler_params=_cparams(2, vmem),
        name="merge",
    )(x, h, ya, yb, yc, w_gate, w_gate, w_gate, bg, bg, bg, w_br_a, w_br_b, w_br_c, w_o)


def _rope_tables(s):
    t = jnp.arange(s)
    row_ids = (t // GRID_W).astype(F32)
    col_ids = (t % GRID_W).astype(F32)
    axis_dim = HEAD_DIM // 2
    inv_freq = ROPE_THETA ** (-jnp.arange(0, axis_dim, 2, dtype=F32) / axis_dim)
    ang_r = row_ids[:, None] * inv_freq
    ang_c = col_ids[:, None] * inv_freq
    cos = jnp.concatenate([jnp.cos(ang_r), jnp.cos(ang_r), jnp.cos(ang_c), jnp.cos(ang_c)], axis=1)
    sin = jnp.concatenate([-jnp.sin(ang_r), jnp.sin(ang_r), -jnp.sin(ang_c), jnp.sin(ang_c)], axis=1)
    return cos, sin


_EARLY_WEIGHTS = ("ffn1_w13", "ffn1_w2", "w_in")
_LATE_WEIGHTS = ("w_gate", "w_br_a", "w_br_b", "w_br_c", "w_o", "ffn2_w13", "ffn2_w2")


def _trunk(x, p, wb, host_casts, cos_tab, sin_tab):
    b, s, d = x.shape
    depth = p["w_in"].shape[0]
    rel_flat = p["rel_bias"].reshape(-1)
    xf = x.reshape(b * s, d)
    for li in range(depth):
        xf = _ffn(xf, p["ffn1_norm"][li], wb["ffn1_w13", li], wb["ffn1_w2", li])
        h, za, zbq, zbkv, *zcs = _mix_in(xf, p["mix_norm"][li], wb["w_in", li], p["q_gain_b"][li],
                                        p["k_gain_b"][li], cos_tab, sin_tab)
        ya = _attn_a(za, rel_flat, p["sink_a"][li]).reshape(b * s, -1)
        keys = []
        if host_casts:
            keys = [(name, li) for name in _LATE_WEIGHTS]
            if li + 1 < depth:
                keys += [(name, li + 1) for name in _EARLY_WEIGHTS]
        yb, copies = _attn_b(zbq, zbkv, [(p[name], layer) for name, layer in keys])
        wb.update(zip(keys, copies))
        yc = _attn_c(zcs, rel_flat)
        xf = _merge(xf, h, ya, yb.reshape(b * s, -1), yc, wb["w_gate", li], p["b_gate"][li], wb["w_br_a", li],
                    wb["w_br_b", li], wb["w_br_c", li], wb["w_o", li])
        fin = p["final_norm"] if li == depth - 1 else None
        xf = _ffn(xf, p["ffn2_norm"][li], wb["ffn2_w13", li], wb["ffn2_w2", li], fin)
    return xf.reshape(b, s, d)


def kernel(x_prompt, x_sample, ffn1_norm, ffn1_w13, ffn1_w2, mix_norm, w_in, q_gain_b, k_gain_b, sink_a,
           w_gate, b_gate, w_br_a, w_br_b, w_br_c, w_o, ffn2_norm, ffn2_w13, ffn2_w2, rel_bias, final_norm):
    p = dict(ffn1_norm=ffn1_norm, ffn1_w13=ffn1_w13, ffn1_w2=ffn1_w2, mix_norm=mix_norm, w_in=w_in,
             q_gain_b=q_gain_b, k_gain_b=k_gain_b, sink_a=sink_a, w_gate=w_gate, b_gate=b_gate,
             w_br_a=w_br_a, w_br_b=w_br_b, w_br_c=w_br_c, w_o=w_o, ffn2_norm=ffn2_norm,
             ffn2_w13=ffn2_w13, ffn2_w2=ffn2_w2, rel_bias=rel_bias, final_norm=final_norm)
    wb = {(name, 0): p[name][0].astype(BF16) for name in _EARLY_WEIGHTS}
    cos_tab, sin_tab = _rope_tables(max(x_prompt.shape[1], x_sample.shape[1]))
    s_p, s_s = x_prompt.shape[1], x_sample.shape[1]
    y_sample = _trunk(x_sample, p, wb, True, cos_tab[:s_s], sin_tab[:s_s])
    y_prompt = _trunk(x_prompt, p, wb, False, cos_tab[:s_p], sin_tab[:s_p])
    return y_prompt, y_sample
```

```python
import functools
import math

import jax
import jax.numpy as jnp
from jax import lax
from jax.experimental import pallas as pl
from jax.experimental.pallas import tpu as pltpu

F32 = jnp.float32
BF16 = jnp.bfloat16

HEAD_DIM = 128
NORM_EPS = 1e-6
MASK_VALUE = -1e30
QK_SCALE = 1.0 / math.sqrt(HEAD_DIM)
LOG2E = math.log2(math.e)
FFN_RESIDUAL = 0.5
ROPE_THETA = 10000.0
GRID_W = 64
NUM_BUCKETS = 32
REL_MAX_DISTANCE = 2048

A_Q, A_K, A_V = 0, 4, 6
B_Q, B_K, B_V = 8, 14, 16
C_Q, C_K, C_V = 18, 24, 27
N_HEAD_BLOCKS = 30
A_HEADS = 4
A_HALF_WINDOW = 128
C_HALF_WINDOW = 64
C_DILATIONS = (1, 4, 16)
PACKED = 4 * HEAD_DIM
A_SUBTILE = 256
C_SUBTILE = 128
C_ROWS_PER_STEP = 2048
ROW_CHUNK = 256
BF16_SUBLANES = 16

V7X_VMEM_BYTES = 64 * 1024 * 1024
MIB = 1024 * 1024


def _cparams(n_grid, vmem_bytes):
    assert vmem_bytes < V7X_VMEM_BYTES, vmem_bytes
    return pltpu.CompilerParams(dimension_semantics=("arbitrary",) * n_grid,
                                vmem_limit_bytes=V7X_VMEM_BYTES)


def _rms(x, gain):
    ms = jnp.mean(x * x, axis=-1, keepdims=True)
    return x * lax.rsqrt(ms + NORM_EPS) * gain


def _lanes(slot):
    return slice(slot * HEAD_DIM, (slot + 1) * HEAD_DIM)


def _ffn_kernel(*refs, n_steps, n_chunk, final):
    if final:
        x_ref, g_ref, wg_ref, wu_ref, w2_ref, fin_ref, o_ref, xn_ref = refs
    else:
        x_ref, g_ref, wg_ref, wu_ref, w2_ref, o_ref, xn_ref = refs
    j = pl.program_id(1)
    row_chunks = [slice(r, r + ROW_CHUNK) for r in range(0, x_ref.shape[0], ROW_CHUNK)]

    def step(first):
        if first:
            for rows in row_chunks:
                xn_ref[rows, :] = _rms(x_ref[rows, :], g_ref[...]).astype(BF16)
        xn = xn_ref[...]
        gate = jnp.dot(xn, wg_ref[...], preferred_element_type=F32)
        up = jnp.dot(xn, wu_ref[...], preferred_element_type=F32)
        act = (gate * jax.nn.sigmoid(gate) * up * FFN_RESIDUAL).astype(BF16)
        acc_ref = x_ref if first else o_ref
        for n in range(o_ref.shape[1] // n_chunk):
            cols = slice(n * n_chunk, (n + 1) * n_chunk)
            o_ref[:, cols] = acc_ref[:, cols] + jnp.dot(act, w2_ref[:, cols], preferred_element_type=F32)

    pl.when(j == 0)(functools.partial(step, True))
    pl.when(j > 0)(functools.partial(step, False))

    if final:
        @pl.when(j == n_steps - 1)
        def _():
            for rows in row_chunks:
                o_ref[rows, :] = _rms(o_ref[rows, :], fin_ref[...])


def _ffn(x, gain, w13, w2, final_gain=None, *, tm=512, tf=512, n_chunk=512):
    m, d = x.shape
    d_ff = w2.shape[0]
    n_chunk = min(n_chunk, d)
    assert m % tm == 0 and d_ff % tf == 0 and d % n_chunk == 0 and tm % ROW_CHUNK == 0
    n_steps = d_ff // tf
    final = final_gain is not None
    in_specs = [
        pl.BlockSpec((tm, d), lambda i, j: (i, 0)),
        pl.BlockSpec((1, d), lambda i, j: (0, 0)),
        pl.BlockSpec((d, tf), lambda i, j: (0, j)),
        pl.BlockSpec((d, tf), lambda i, j: (0, j + n_steps)),
        pl.BlockSpec((tf, d), lambda i, j: (j, 0)),
    ]
    args = [x, gain.reshape(1, d), w13, w13, w2]
    if final:
        in_specs.append(pl.BlockSpec((1, d), lambda i, j: (0, 0)))
        args.append(final_gain.reshape(1, d))
    vmem = 2 * tm * d * 4 + 2 * tm * d * 4 + tm * d * 2 + 2 * (3 * d * tf * 2) \
        + 3 * tm * tf * 4 + tm * tf * 2 + 3 * tm * n_chunk * 4 + 3 * ROW_CHUNK * d * 4 + 2 * MIB
    return pl.pallas_call(
        functools.partial(_ffn_kernel, n_steps=n_steps, n_chunk=n_chunk, final=final),
        grid=(m // tm, n_steps),
        in_specs=in_specs,
        out_specs=pl.BlockSpec((tm, d), lambda i, j: (i, 0)),
        out_shape=jax.ShapeDtypeStruct((m, d), F32),
        scratch_shapes=[pltpu.VMEM((tm, d), BF16)],
        compiler_params=_cparams(2, vmem),
        name="ffn_final" if final else "ffn",
    )(*args)


def _swap_quarter_pairs(x):
    q = HEAD_DIM // 4
    lane = lax.broadcasted_iota(jnp.int32, x.shape, 1)
    from_right = pltpu.roll(x, HEAD_DIM - q, axis=1)
    from_left = pltpu.roll(x, q, axis=1)
    return jnp.where((lane % (2 * q)) < q, from_right, from_left)


def _mixin_kernel(x_ref, g_ref, w_ref, qg_ref, kg_ref, cos_ref, sin_ref,
                  h_ref, za_ref, zbq_ref, zbkv_ref, zc0_ref, zc1_ref, zc2_ref, cs_ref):
    tm = x_ref.shape[0]
    h = _rms(x_ref[...], g_ref[...]).astype(BF16)
    h_ref[...] = h
    cos = cos_ref[...]
    sin = sin_ref[...]

    def norm_rope(v, gain):
        ms = jnp.mean(v * v, axis=-1, keepdims=True)
        y = v * lax.rsqrt(ms + NORM_EPS) * gain
        return y * cos + _swap_quarter_pairs(y) * sin

    for c in range(N_HEAD_BLOCKS // 2):
        zc = jnp.dot(h, w_ref[:, c * 256:(c + 1) * 256], preferred_element_type=F32)
        for hh in range(2):
            blk = 2 * c + hh
            v = zc[:, _lanes(hh)]
            if blk < A_K:
                za_ref[0, blk // 2, :, _lanes(blk % 2)] = (v * QK_SCALE).astype(BF16)
            elif blk < A_V:
                za_ref[0, blk - A_K, :, _lanes(2)] = v.astype(BF16)
            elif blk < B_Q:
                za_ref[0, blk - A_V, :, _lanes(3)] = v.astype(BF16)
            elif blk < B_K:
                q = norm_rope(v, qg_ref[...]) * (QK_SCALE * LOG2E)
                zbq_ref[0, (blk - B_Q) // 3, :, _lanes((blk - B_Q) % 3)] = q.astype(BF16)
            elif blk < B_V:
                zbkv_ref[0, blk - B_K, :, _lanes(0)] = norm_rope(v, kg_ref[...]).astype(BF16)
            elif blk < C_Q:
                zbkv_ref[0, blk - B_V, :, _lanes(1)] = v.astype(BF16)
            elif blk < C_K:
                cs_ref[blk - C_Q] = v * QK_SCALE
            else:
                cs_ref[blk - C_Q] = v
    for kv in range(zbkv_ref.shape[1]):
        zbkv_ref[0, kv, :, _lanes(2)] = jnp.ones((tm, HEAD_DIM), BF16)

    n_pairs = len(C_DILATIONS)
    for pair, (r, zc_ref) in enumerate(zip(C_DILATIONS, (zc0_ref, zc1_ref, zc2_ref))):
        staged = (2 * pair, 2 * pair + 1, 2 * n_pairs + pair, 3 * n_pairs + pair)
        for slot, src in enumerate(staged):
            for c in range(r):
                rows = pl.ds(c, tm // r, stride=r) if r > 1 else slice(None)
                zc_ref[0, c, :, _lanes(slot)] = cs_ref[src, rows, :].astype(BF16)


def _mix_in(x, gain, w_in, q_gain, k_gain, cos_tab, sin_tab, *, tm=512):
    m, d = x.shape
    s = cos_tab.shape[0]
    b = m // s
    n_cols = w_in.shape[1]
    assert m % tm == 0 and s % tm == 0 and n_cols == N_HEAD_BLOCKS * HEAD_DIM
    assert all(tm % (16 * r) == 0 for r in C_DILATIONS)
    pos_tiles = s // tm

    def packed(n, width, rows):
        return pl.BlockSpec((1, n, rows, width), lambda i: (i // pos_tiles, 0, i % pos_tiles, 0))

    out_specs = [pl.BlockSpec((tm, d), lambda i: (i, 0)),
                 packed(2, PACKED, tm), packed(2, 3 * HEAD_DIM, tm), packed(2, 3 * HEAD_DIM, tm)]
    out_shape = [jax.ShapeDtypeStruct((m, d), BF16),
                 jax.ShapeDtypeStruct((b, 2, s, PACKED), BF16),
                 jax.ShapeDtypeStruct((b, 2, s, 3 * HEAD_DIM), BF16),
                 jax.ShapeDtypeStruct((b, 2, s, 3 * HEAD_DIM), BF16)]
    for r in C_DILATIONS:
        out_specs.append(packed(r, PACKED, tm // r))
        out_shape.append(jax.ShapeDtypeStruct((b, r, s // r, PACKED), BF16))
    c_cols = (N_HEAD_BLOCKS - C_Q) * HEAD_DIM
    vmem = 2 * tm * d * 4 + 2 * tm * d * 2 + 2 * tm * (n_cols + 2 * HEAD_DIM) * 2 + d * n_cols * 2 \
        + 4 * tm * HEAD_DIM * 4 + tm * c_cols * 4 + tm * d * 4 + 4 * tm * 256 * 4 \
        + 8 * tm * HEAD_DIM * 4 + 2 * MIB
    return pl.pallas_call(
        _mixin_kernel,
        grid=(m // tm,),
        in_specs=[
            pl.BlockSpec((tm, d), lambda i: (i, 0)),
            pl.BlockSpec((1, d), lambda i: (0, 0)),
            pl.BlockSpec((d, n_cols), lambda i: (0, 0), pipeline_mode=pl.Buffered(1)),
            pl.BlockSpec((1, HEAD_DIM), lambda i: (0, 0)),
            pl.BlockSpec((1, HEAD_DIM), lambda i: (0, 0)),
            pl.BlockSpec((tm, HEAD_DIM), lambda i: (i % pos_tiles, 0)),
            pl.BlockSpec((tm, HEAD_DIM), lambda i: (i % pos_tiles, 0)),
        ],
        out_specs=out_specs,
        out_shape=out_shape,
        scratch_shapes=[pltpu.VMEM((N_HEAD_BLOCKS - C_Q, tm, HEAD_DIM), F32)],
        compiler_params=_cparams(1, vmem),
        name="mix_in",
    )(x, gain.reshape(1, d), w_in, q_gain.reshape(1, HEAD_DIM), k_gain.reshape(1, HEAD_DIM),
      cos_tab, sin_tab)


def _t5_bucket(rel):
    half = NUM_BUCKETS // 2
    max_exact = half // 2
    ret = jnp.where(rel > 0, half, 0)
    n = jnp.abs(rel)
    nf = jnp.maximum(n, 1).astype(jnp.float32)
    large = max_exact + (jnp.log(nf / max_exact) / math.log(REL_MAX_DISTANCE / max_exact)
                         * (half - max_exact)).astype(jnp.int32)
    large = jnp.minimum(large, half - 1)
    return ret + jnp.where(n < max_exact, n, large)


def _band_rel(ts, halo):
    return jnp.arange(ts + 2 * halo)[None, :] - halo - jnp.arange(ts)[:, None]


def _band_kernel(*refs, halo, ts, seq_len, n_heads, head0, n_table_cols, with_sink):
    if with_sink:
        bkt_ref, tab_ref, sink_ref, left_ref, main_ref, right_ref, o_ref, bias_ref = refs
        lse_ref = None
    else:
        bkt_ref, tab_ref, left_ref, main_ref, right_ref, o_ref, lse_ref, bias_ref = refs
        sink_ref = None
    n_sub, tq = main_ref.shape[1], main_ref.shape[2]
    nk = ts + 2 * halo
    first = (pl.program_id(0) == 0) & (pl.program_id(1) == 0) & (pl.program_id(2) == 0)

    @pl.when(first)
    def _():
        bkt = bkt_ref[...]
        rel = (lax.broadcasted_iota(jnp.int32, (ts, nk), 1) - halo
               - lax.broadcasted_iota(jnp.int32, (ts, nk), 0))
        in_band = jnp.abs(rel) <= halo
        tiles = [jnp.zeros((ts, nk), F32) for _ in range(n_heads)]
        for t in range(NUM_BUCKETS):
            hit = bkt == t
            for h in range(n_heads):
                tiles[h] = jnp.where(hit, tab_ref[t * n_table_cols + head0 + h], tiles[h])
        for h in range(n_heads):
            bias_ref[h] = jnp.where(in_band, tiles[h], MASK_VALUE)

    qi = pl.program_id(2)
    for sub in range(n_sub):
        def keys(slot):
            return jnp.concatenate([left_ref[0, sub, :, _lanes(slot)], main_ref[0, sub, :, _lanes(slot)],
                                    right_ref[0, sub, :, _lanes(slot)]], axis=0)

        k = keys(2)
        v = keys(3)
        for t in range(tq // ts):
            rows = slice(t * ts, (t + 1) * ts)
            kt = k[t * ts:t * ts + nk]
            vt = v[t * ts:t * ts + nk]
            key_pos = qi * tq + t * ts - halo + lax.broadcasted_iota(jnp.int32, (1, nk), 1)
            key_ok = (key_pos >= 0) & (key_pos < seq_len)
            lses = []
            for g in range(2):
                head = (pl.program_id(1) * n_sub + sub) * 2 + g if with_sink else g
                logits = lax.dot_general(main_ref[0, sub, rows, _lanes(g)], kt, (((1,), (1,)), ((), ())),
                                         preferred_element_type=F32)
                logits = jnp.where(key_ok, logits + bias_ref[head], MASK_VALUE)
                m = jnp.max(logits, axis=-1, keepdims=True)
                if with_sink:
                    sink = sink_ref[head]
                    m = jnp.maximum(m, sink)
                p = jnp.exp(logits - m)
                denom = jnp.sum(p, axis=-1, keepdims=True)
                if with_sink:
                    denom = denom + jnp.exp(sink - m)
                out = jnp.dot(p.astype(BF16), vt, preferred_element_type=F32) / denom
                if with_sink:
                    o_ref[0, rows, _lanes(2 * sub + g)] = out.astype(o_ref.dtype)
                else:
                    o_ref[0, sub, rows, _lanes(g)] = out.astype(o_ref.dtype)
                    lses.append(m + jnp.log(denom))
            if not with_sink:
                lane = lax.broadcasted_iota(jnp.int32, (ts, HEAD_DIM), 1)
                lse_ref[0, sub, rows, :] = jnp.where(lane < HEAD_DIM // 2, lses[0], lses[1])


def _band_specs(tq, halo, n_sub, n_halo_blocks):
    per = tq // halo
    return [
        pl.BlockSpec((1, n_sub, halo, PACKED), lambda b, c, i: (b, c, jnp.maximum(i * per - 1, 0), 0)),
        pl.BlockSpec((1, n_sub, tq, PACKED), lambda b, c, i: (b, c, i, 0)),
        pl.BlockSpec((1, n_sub, halo, PACKED),
                     lambda b, c, i: (b, c, jnp.minimum((i + 1) * per, n_halo_blocks - 1), 0)),
    ]


def _band_vmem(tq, ts, halo, n_sub, n_heads, out_bytes):
    nk = ts + 2 * halo
    return (n_heads * ts * nk * 4 + 2 * ts * nk * 4 + 2 * n_sub * (tq + 2 * halo) * PACKED * 2 + 2 * out_bytes
            + 10 * ts * nk * 4 + 4 * (tq + 2 * halo) * HEAD_DIM * 2 + 4 * MIB)


def _attn_a(za, rel_bias_flat, sink, *, tq=1024, n_sub=2):
    b, n_kv, s, _ = za.shape
    halo = A_HALF_WINDOW
    ts = A_SUBTILE
    assert s % tq == 0 and tq % halo == 0 and n_kv % n_sub == 0 and tq % ts == 0
    nk = ts + 2 * halo
    bkt = _t5_bucket(_band_rel(ts, halo)).astype(jnp.int32)
    smem = pl.BlockSpec(memory_space=pltpu.SMEM)
    out_w = n_sub * 2 * HEAD_DIM
    return pl.pallas_call(
        functools.partial(_band_kernel, halo=halo, ts=ts, seq_len=s, n_heads=A_HEADS, head0=0,
                          n_table_cols=rel_bias_flat.shape[0] // NUM_BUCKETS, with_sink=True),
        grid=(b, n_kv // n_sub, s // tq),
        in_specs=[pl.BlockSpec((ts, nk), lambda b_, c, i: (0, 0)), smem, smem]
        + _band_specs(tq, halo, n_sub, s // halo),
        out_specs=pl.BlockSpec((1, tq, out_w), lambda b_, c, i: (b_, i, c)),
        out_shape=jax.ShapeDtypeStruct((b, s, A_HEADS * HEAD_DIM), BF16),
        scratch_shapes=[pltpu.VMEM((A_HEADS, ts, nk), F32)],
        compiler_params=_cparams(3, _band_vmem(tq, ts, halo, n_sub, A_HEADS, tq * out_w * 2)),
        name="attn_a",
    )(bkt, rel_bias_flat, sink, za, za, za)


def _attn_c_pair(zc, rel_bias_flat, pair):
    b, r, sub, _ = zc.shape
    halo = C_HALF_WINDOW
    n_sub = min(r, C_ROWS_PER_STEP // C_SUBTILE)
    tq = min(sub, C_ROWS_PER_STEP // n_sub)
    ts = min(C_SUBTILE, tq)
    assert sub % tq == 0 and tq % halo == 0 and r % n_sub == 0 and tq % ts == 0
    nk = ts + 2 * halo
    bkt = _t5_bucket(_band_rel(ts, halo) * r).astype(jnp.int32)
    smem = pl.BlockSpec(memory_space=pltpu.SMEM)
    def out_spec(width):
        return pl.BlockSpec((1, n_sub, tq, width), lambda b_, c, i: (b_, c, i, 0))

    return pl.pallas_call(
        functools.partial(_band_kernel, halo=halo, ts=ts, seq_len=sub, n_heads=2, head0=A_HEADS + 2 * pair,
                          n_table_cols=rel_bias_flat.shape[0] // NUM_BUCKETS, with_sink=False),
        grid=(b, r // n_sub, sub // tq),
        in_specs=[pl.BlockSpec((ts, nk), lambda b_, c, i: (0, 0)), smem]
        + _band_specs(tq, halo, n_sub, sub // halo),
        out_specs=[out_spec(2 * HEAD_DIM), out_spec(HEAD_DIM)],
        out_shape=[jax.ShapeDtypeStruct((b, r, sub, 2 * HEAD_DIM), BF16),
                   jax.ShapeDtypeStruct((b, r, sub, HEAD_DIM), F32)],
        scratch_shapes=[pltpu.VMEM((2, ts, nk), F32)],
        compiler_params=_cparams(3, _band_vmem(tq, ts, halo, n_sub, 2, 2 * n_sub * tq * HEAD_DIM * 4)),
        name=f"attn_c{pair}",
    )(bkt, rel_bias_flat, zc, zc, zc)


def _c_merge_kernel(*refs):
    n = len(C_DILATIONS)
    o_refs, l_refs, y_ref = refs[:n], refs[n:2 * n], refs[2 * n]
    buf_ref = refs[2 * n + 1]
    tm = y_ref.shape[0]
    slots = iter(range(buf_ref.shape[0]))

    def token_order(ref, r, lanes):
        if r == 1:
            return ref[0, 0, :, lanes].astype(F32)
        slot = next(slots)
        for c in range(r):
            buf_ref[slot, pl.ds(c, tm // r, stride=r), :] = ref[0, c, :, lanes].astype(F32)
        return buf_ref[slot]

    lses = [token_order(ref, r, _lanes(0)) for ref, r in zip(l_refs, C_DILATIONS)]
    m = functools.reduce(jnp.maximum, lses)
    es = [jnp.exp(l - m) for l in lses]
    tot = functools.reduce(jnp.add, es)
    weights = [e / tot for e in es]
    half = HEAD_DIM // 2
    for g in range(y_ref.shape[1] // HEAD_DIM):
        lanes = _lanes(g)
        outs = [token_order(ref, r, lanes) for ref, r in zip(o_refs, C_DILATIONS)]
        y = functools.reduce(jnp.add, [jnp.broadcast_to(w[:, g * half:g * half + 1], (tm, HEAD_DIM)) * o
                                       for w, o in zip(weights, outs)])
        y_ref[:, lanes] = y.astype(y_ref.dtype)


def _attn_c(zcs, rel_bias_flat, *, tm=1024):
    outs, lses = zip(*[_attn_c_pair(zc, rel_bias_flat, p) for p, zc in enumerate(zcs)])
    b, _, s, w = outs[0].shape
    m = b * s
    assert s % tm == 0
    pos_tiles = s // tm

    def spec(r, width):
        return pl.BlockSpec((1, r, tm // r, width), lambda i: (i // pos_tiles, 0, i % pos_tiles, 0))

    n_slots = (w // HEAD_DIM + 1) * sum(r > 1 for r in C_DILATIONS)
    return pl.pallas_call(
        _c_merge_kernel,
        grid=(m // tm,),
        in_specs=[spec(r, w) for r in C_DILATIONS] + [spec(r, HEAD_DIM) for r in C_DILATIONS],
        out_specs=pl.BlockSpec((tm, w), lambda i: (i, 0)),
        out_shape=jax.ShapeDtypeStruct((m, w), BF16),
        scratch_shapes=[pltpu.VMEM((n_slots, tm, HEAD_DIM), F32)],
        compiler_params=_cparams(1, 2 * 6 * tm * w * 4 + (n_slots + 16) * tm * HEAD_DIM * 4 + 2 * MIB),
        name="attn_c_merge",
    )(*outs, *lses)


def _attn_b_kernel(*refs, tk, n_casts):
    q_ref, kv_ref = refs[:2]
    cast_in = refs[2:2 + n_casts]
    o_ref = refs[2 + n_casts]
    cast_out = refs[3 + n_casts:]
    tq = q_ref.shape[2]
    s = kv_ref.shape[2]
    q = jnp.concatenate([q_ref[0, 0, :, _lanes(g)] for g in range(3)], axis=0)
    m = acc = None
    for c in range(s // tk):
        rows = slice(c * tk, (c + 1) * tk)
        logits = lax.dot_general(q, kv_ref[0, 0, rows, _lanes(0)], (((1,), (1,)), ((), ())),
                                 preferred_element_type=F32)
        cmax = jnp.max(logits, axis=-1, keepdims=True)
        m_new = cmax if c == 0 else jnp.maximum(m, cmax)
        p = jnp.exp2(logits - m_new)
        pv = jnp.dot(p.astype(BF16), kv_ref[0, 0, rows, HEAD_DIM:], preferred_element_type=F32)
        acc = pv if c == 0 else jnp.exp2(m - m_new) * acc + pv
        m = m_new
    out = acc[:, _lanes(0)] / acc[:, HEAD_DIM:HEAD_DIM + 1]
    for g in range(3):
        o_ref[0, :, _lanes(g)] = out[g * tq:(g + 1) * tq].astype(o_ref.dtype)
    for src, dst in zip(cast_in, cast_out):
        dst[...] = src[...].astype(BF16)


def _cast_blocks(n_rows, n_steps):
    n_blocks = n_steps
    while n_blocks > 1 and (n_rows % n_blocks or (n_rows // n_blocks) % BF16_SUBLANES):
        n_blocks //= 2
    assert n_steps % n_blocks == 0 and n_rows % n_blocks == 0
    return n_blocks


def _attn_b(zbq, zbkv, casts=(), *, tq=256, tk=256):
    b, n_kv, s, _ = zbq.shape
    assert s % tq == 0 and s % tk == 0
    n_qt = s // tq
    n_steps = b * n_kv * n_qt
    in_specs = [pl.BlockSpec((1, 1, tq, 3 * HEAD_DIM), lambda b_, c, i: (b_, c, i, 0)),
                pl.BlockSpec((1, 1, s, 3 * HEAD_DIM), lambda b_, c, i: (b_, c, 0, 0))]
    out_specs = [pl.BlockSpec((1, tq, 3 * HEAD_DIM), lambda b_, c, i: (b_, i, c))]
    out_shape = [jax.ShapeDtypeStruct((b, s, n_kv * 3 * HEAD_DIM), BF16)]
    vmem = 2 * tq * 384 * 2 + 2 * s * 384 * 2 + 2 * tq * 384 * 2 \
        + 8 * 3 * tq * tk * 4 + 12 * 3 * tq * HEAD_DIM * 4 + 2 * MIB
    for w, layer in casts:
        _, n_rows, n_cols = w.shape
        n_blocks = _cast_blocks(n_rows, n_steps)
        rep = n_steps // n_blocks
        rows = n_rows // n_blocks

        def block(b_, c, i, rep=rep):
            return ((b_ * n_kv + c) * n_qt + i) // rep

        in_specs.append(pl.BlockSpec((None, rows, n_cols),
                                     lambda b_, c, i, layer=layer, block=block: (layer, block(b_, c, i), 0)))
        out_specs.append(pl.BlockSpec((rows, n_cols), lambda b_, c, i, block=block: (block(b_, c, i), 0)))
        out_shape.append(jax.ShapeDtypeStruct((n_rows, n_cols), BF16))
        vmem += 2 * rows * n_cols * (4 + 2) + rows * n_cols * 4
    yb, *cast_out = pl.pallas_call(
        functools.partial(_attn_b_kernel, tk=tk, n_casts=len(casts)),
        grid=(b, n_kv, n_qt),
        in_specs=in_specs,
        out_specs=out_specs,
        out_shape=out_shape,
        compiler_params=_cparams(3, vmem),
        name="attn_b_cast" if casts else "attn_b",
    )(zbq, zbkv, *[w for w, _ in casts])
    return yb, cast_out


def _merge_kernel(x_ref, h_ref, ya_ref, yb_ref, yc_ref, wga_ref, wgb_ref, wgc_ref,
                  bga_ref, bgb_ref, bgc_ref, wa_ref, wb_ref, wc_ref, wo_ref, o_ref, *, n_chunk):
    def step(first):
        h = h_ref[...]

        def branch(y_ref, w_ref, wg_ref, bg_ref):
            gate = jax.nn.sigmoid(jnp.dot(h, wg_ref[...], preferred_element_type=F32) + bg_ref[...])
            return gate * jnp.dot(y_ref[...], w_ref[...], preferred_element_type=F32)

        merged = (branch(ya_ref, wa_ref, wga_ref, bga_ref) + branch(yb_ref, wb_ref, wgb_ref, bgb_ref)
                  + branch(yc_ref, wc_ref, wgc_ref, bgc_ref)).astype(BF16)
        acc_ref = x_ref if first else o_ref
        for n in range(o_ref.shape[1] // n_chunk):
            cols = slice(n * n_chunk, (n + 1) * n_chunk)
            o_ref[:, cols] = acc_ref[:, cols] + jnp.dot(merged, wo_ref[:, cols], preferred_element_type=F32)

    j = pl.program_id(1)
    pl.when(j == 0)(functools.partial(step, True))
    pl.when(j > 0)(functools.partial(step, False))


def _merge(x, h, ya, yb, yc, w_gate, b_gate, w_br_a, w_br_b, w_br_c, w_o, *, tm=512, tn=512, n_chunk=512):
    m, d = x.shape
    tn = min(tn, d)
    n_chunk = min(n_chunk, d)
    assert m % tm == 0 and d % tn == 0 and d % n_chunk == 0
    n_steps = d // tn
    ka, kb, kc = ya.shape[1], yb.shape[1], yc.shape[1]

    def row(width):
        return pl.BlockSpec((tm, width), lambda i, j: (i, 0))

    def gate_w(br):
        return pl.BlockSpec((d, tn), lambda i, j: (0, br * n_steps + j))

    def gate_b(br):
        return pl.BlockSpec((1, tn), lambda i, j: (0, br * n_steps + j))

    def br_w(k):
        return pl.BlockSpec((k, tn), lambda i, j: (0, j))

    bg = b_gate.reshape(1, 3 * d)
    vmem = 2 * (tm * d * 4 + tm * d * 2 + tm * (ka + kb + kc) * 2) + 2 * tm * d * 4 \
        + 2 * (3 * d + ka + kb + kc + d) * tn * 2 + 7 * tm * tn * 4 + 3 * tm * n_chunk * 4 + 2 * MIB
    return pl.pallas_call(
        functools.partial(_merge_kernel, n_chunk=n_chunk),
        grid=(m // tm, n_steps),
        in_specs=[row(d), row(d), row(ka), row(kb), row(kc),
                  gate_w(0), gate_w(1), gate_w(2), gate_b(0), gate_b(1), gate_b(2),
                  br_w(ka), br_w(kb), br_w(kc),
                  pl.BlockSpec((tn, d), lambda i, j: (j, 0))],
        out_specs=pl.BlockSpec((tm, d), lambda i, j: (i, 0)),
        out_shape=jax.ShapeDtypeStruct((m, d), F32),
        compiler_params=_cparams(2, vmem),
        name="merge",
    )(x, h, ya, yb, yc, w_gate, w_gate, w_gate, bg, bg, bg, w_br_a, w_br_b, w_br_c, w_o)


def _rope_tables(s):
    t = jnp.arange(s)
    row_ids = (t // GRID_W).astype(F32)
    col_ids = (t % GRID_W).astype(F32)
    axis_dim = HEAD_DIM // 2
    inv_freq = ROPE_THETA ** (-jnp.arange(0, axis_dim, 2, dtype=F32) / axis_dim)
    ang_r = row_ids[:, None] * inv_freq
    ang_c = col_ids[:, None] * inv_freq
    cos = jnp.concatenate([jnp.cos(ang_r), jnp.cos(ang_r), jnp.cos(ang_c), jnp.cos(ang_c)], axis=1)
    sin = jnp.concatenate([-jnp.sin(ang_r), jnp.sin(ang_r), -jnp.sin(ang_c), jnp.sin(ang_c)], axis=1)
    return cos, sin


_EARLY_WEIGHTS = ("ffn1_w13", "ffn1_w2", "w_in")
_LATE_WEIGHTS = ("w_gate", "w_br_a", "w_br_b", "w_br_c", "w_o", "ffn2_w13", "ffn2_w2")


def _trunk(x, p, wb, host_casts, cos_tab, sin_tab):
    b, s, d = x.shape
    depth = p["w_in"].shape[0]
    rel_flat = p["rel_bias"].reshape(-1)
    xf = x.reshape(b * s, d)
    for li in range(depth):
        xf = _ffn(xf, p["ffn1_norm"][li], wb["ffn1_w13", li], wb["ffn1_w2", li])
        h, za, zbq, zbkv, *zcs = _mix_in(xf, p["mix_norm"][li], wb["w_in", li], p["q_gain_b"][li],
                                        p["k_gain_b"][li], cos_tab, sin_tab)
        ya = _attn_a(za, rel_flat, p["sink_a"][li]).reshape(b * s, -1)
        keys = []
        if host_casts:
            keys = [(name, li) for name in _LATE_WEIGHTS]
            if li + 1 < depth:
                keys += [(name, li + 1) for name in _EARLY_WEIGHTS]
        yb, copies = _attn_b(zbq, zbkv, [(p[name], layer) for name, layer in keys])
        wb.update(zip(keys, copies))
        yc = _attn_c(zcs, rel_flat)
        xf = _merge(xf, h, ya, yb.reshape(b * s, -1), yc, wb["w_gate", li], p["b_gate"][li], wb["w_br_a", li],
                    wb["w_br_b", li], wb["w_br_c", li], wb["w_o", li])
        fin = p["final_norm"] if li == depth - 1 else None
        xf = _ffn(xf, p["ffn2_norm"][li], wb["ffn2_w13", li], wb["ffn2_w2", li], fin)
    return xf.reshape(b, s, d)


def kernel(x_prompt, x_sample, ffn1_norm, ffn1_w13, ffn1_w2, mix_norm, w_in, q_gain_b, k_gain_b, sink_a,
           w_gate, b_gate, w_br_a, w_br_b, w_br_c, w_o, ffn2_norm, ffn2_w13, ffn2_w2, rel_bias, final_norm):
    p = dict(ffn1_norm=ffn1_norm, ffn1_w13=ffn1_w13, ffn1_w2=ffn1_w2, mix_norm=mix_norm, w_in=w_in,
             q_gain_b=q_gain_b, k_gain_b=k_gain_b, sink_a=sink_a, w_gate=w_gate, b_gate=b_gate,
             w_br_a=w_br_a, w_br_b=w_br_b, w_br_c=w_br_c, w_o=w_o, ffn2_norm=ffn2_norm,
             ffn2_w13=ffn2_w13, ffn2_w2=ffn2_w2, rel_bias=rel_bias, final_norm=final_norm)
    wb = {(name, 0): p[name][0].astype(BF16) for name in _EARLY_WEIGHTS}
    cos_tab, sin_tab = _rope_tables(max(x_prompt.shape[1], x_sample.shape[1]))
    s_p, s_s = x_prompt.shape[1], x_sample.shape[1]
    y_sample = _trunk(x_sample, p, wb, True, cos_tab[:s_s], sin_tab[:s_s])
    y_prompt = _trunk(x_prompt, p, wb, False, cos_tab[:s_p], sin_tab[:s_p])
    return y_prompt, y_sample
```
